```python
import math
import jax, jax.numpy as jnp
from jax import lax
import numpy as np


D_MODEL = 1024
BATCH = 8
SEQ = 2048
DEPTH = 2

HEAD_DIM = 64
RWKV_DIM = D_MODEL // 2
RWKV_HEADS = RWKV_DIM // HEAD_DIM
ATT_Q_HEADS = (D_MODEL - RWKV_DIM) // HEAD_DIM
ATT_KV_HEADS = 2
ATT_GROUP = ATT_Q_HEADS // ATT_KV_HEADS
ATT_DIM = ATT_Q_HEADS * HEAD_DIM
KV_DIM = ATT_KV_HEADS * HEAD_DIM
LORA_W = 64
LORA_A = 64
LORA_G = 128
SHIFT_DIM = 3 * RWKV_DIM + LORA_W + LORA_A + LORA_G
IN_DIM = SHIFT_DIM + ATT_DIM + 2 * KV_DIM
WINDOW = 128
BLOCK = 128
CONV_WIDTH = 31
D_FF = 4 * D_MODEL
RMS_EPS = 1e-6
LN_EPS = 1e-5
GN_EPS = 64e-5
N_EVEN = (DEPTH + 1) // 2
N_ODD = DEPTH // 2

kernel_name = 'hybrid_rwkv7_swa_sink_conformer_conv'


def rmsnorm(x, g):
    x32 = x.astype(jnp.float32)
    y = x32 * lax.rsqrt(jnp.mean(x32 * x32, axis=-1, keepdims=True) + RMS_EPS)
    return (y * g.astype(jnp.float32)).astype(x.dtype)


def token_shift(p):
    return jnp.pad(p[:, :-1], ((0, 0), (1, 0), (0, 0)))


def rwkv7_scan(r, w, k, v, a, b):
    bsz, _, nh, n = r.shape

    def step(state, inp):
        r_t, w_t, k_t, v_t, a_t, b_t = inp
        sa = jnp.einsum('bhvk,bhk->bhv', state, a_t)
        state = (state * w_t[:, :, None, :]
                 + sa[..., None] * b_t[:, :, None, :]
                 + v_t[..., None] * k_t[:, :, None, :])
        y_t = jnp.einsum('bhvk,bhk->bhv', state, r_t)
        return state, y_t

    xs = tuple(jnp.moveaxis(t, 1, 0) for t in (r, w, k, v, a, b))
    s0 = jnp.zeros((bsz, nh, n, n), jnp.float32)
    _, y = lax.scan(step, s0, xs)
    return jnp.moveaxis(y, 0, 1)


def rwkv7_time_mix(p, mu, w0, w_up, a0, a_up, g_up, k_k, k_a, r_k, gn_g, gn_b):
    bsz, seq, _ = p.shape
    f32 = jnp.float32
    p = p + (token_shift(p) - p) * mu
    c1 = RWKV_DIM
    c2 = 2 * RWKV_DIM
    c3 = 3 * RWKV_DIM
    c4 = c3 + LORA_W
    c5 = c4 + LORA_A
    r, k, v, wl, al, gl = jnp.split(p, [c1, c2, c3, c4, c5], axis=-1)
    w = -jax.nn.softplus(-(w0 + jnp.tanh(wl) @ w_up)) - 0.5
    decay = jnp.exp(-jnp.exp(w.astype(f32)))
    a = jax.nn.sigmoid(a0 + al @ a_up)
    g = jax.nn.sigmoid(gl) @ g_up

    def heads(t):
        return t.astype(f32).reshape(bsz, seq, RWKV_HEADS, HEAD_DIM)

    r, k, v, a, decay = heads(r), heads(k), heads(v), heads(a), heads(decay)
    kk = k * k_k.astype(f32)
    kk = kk / jnp.maximum(jnp.sqrt(jnp.sum(kk * kk, axis=-1, keepdims=True)), 1e-12)
    k = k * (1.0 + (a - 1.0) * k_a.astype(f32))
    y = rwkv7_scan(r, decay, k, v, -kk, kk * a)
    mean = jnp.mean(y, axis=-1, keepdims=True)
    var = jnp.mean(jnp.square(y - mean), axis=-1, keepdims=True)
    y = (y - mean) * lax.rsqrt(var + GN_EPS) * gn_g.astype(f32) + gn_b.astype(f32)
    y = y + jnp.sum(r * k * r_k.astype(f32), axis=-1, keepdims=True) * v
    return (y.reshape(bsz, seq, RWKV_DIM) * g.astype(f32)).astype(p.dtype)


def sliding_window_sink_attention(q, k, v, sinks):
    bsz, seq, _, _ = q.shape
    nb = seq // BLOCK
    f32 = jnp.float32
    qb = q.reshape(bsz, nb, BLOCK, ATT_KV_HEADS, ATT_GROUP, HEAD_DIM)

    def with_prev(t):
        tb = t.reshape(bsz, nb, BLOCK, ATT_KV_HEADS, HEAD_DIM)
        prev = jnp.pad(tb[:, :-1], ((0, 0), (1, 0), (0, 0), (0, 0), (0, 0)))
        return jnp.concatenate([prev, tb], axis=2)

    kb, vb = with_prev(k), with_prev(v)
    scores = jnp.einsum('bnqhgd,bnkhd->bnhgqk', qb, kb).astype(f32) * (HEAD_DIM ** -0.5)
    qi = jnp.arange(BLOCK)[:, None]
    kj = jnp.arange(2 * BLOCK)[None, :]
    rel = qi + BLOCK - kj
    band = (rel >= 0) & (rel < WINDOW)
    key_pos = jnp.arange(nb)[:, None] * BLOCK + jnp.arange(2 * BLOCK)[None, :] - BLOCK
    mask = band[None] & (key_pos >= 0)[:, None, :]
    slopes = jnp.exp2(-8.0 * jnp.arange(1, ATT_Q_HEADS + 1, dtype=f32) / ATT_Q_HEADS)
    slopes = slopes.reshape(ATT_KV_HEADS, ATT_GROUP)
    scores = scores - slopes[:, :, None, None] * rel.astype(f32)
    scores = jnp.where(mask[None, :, None, None], scores, -jnp.inf)
    sink = sinks.astype(f32).reshape(ATT_KV_HEADS, ATT_GROUP)[None, None, :, :, None, None]
    m = jnp.maximum(jnp.max(scores, axis=-1, keepdims=True), sink)
    e = jnp.exp(scores - m)
    probs = e / (jnp.sum(e, axis=-1, keepdims=True) + jnp.exp(sink - m))
    out = jnp.einsum('bnhgqk,bnkhd->bnqhgd', probs.astype(v.dtype), vb)
    return out.reshape(bsz, seq, ATT_DIM)


def hybrid_mixer(h, w_in, mu, w0, w_up, a0, a_up, g_up, k_k, k_a, r_k, gn_g, gn_b, sinks, w_out):
    bsz, seq, _ = h.shape
    p = h @ w_in
    o1 = SHIFT_DIM
    o2 = o1 + ATT_DIM
    o3 = o2 + KV_DIM
    q = p[..., o1:o2].reshape(bsz, seq, ATT_Q_HEADS, HEAD_DIM)
    k = p[..., o2:o3].reshape(bsz, seq, ATT_KV_HEADS, HEAD_DIM)
    v = p[..., o3:].reshape(bsz, seq, ATT_KV_HEADS, HEAD_DIM)
    y_rwkv = rwkv7_time_mix(p[..., :o1], mu, w0, w_up, a0, a_up, g_up, k_k, k_a, r_k, gn_g, gn_b)
    y_att = sliding_window_sink_attention(q, k, v, sinks)
    return jnp.concatenate([y_rwkv, y_att.astype(y_rwkv.dtype)], axis=-1) @ w_out


def conformer_conv(h, pw1_w, pw1_b, dw_w, dw_b, ln_g, ln_b, pw2_w, pw2_b):
    u = h @ pw1_w + pw1_b
    u = u[..., :D_MODEL] * jax.nn.sigmoid(u[..., D_MODEL:])
    u = lax.conv_general_dilated(
        u, dw_w[:, None, :].astype(u.dtype), window_strides=(1,),
        padding=[(CONV_WIDTH - 1, 0)],
        dimension_numbers=('NWC', 'WIO', 'NWC'),
        feature_group_count=D_MODEL) + dw_b
    u32 = u.astype(jnp.float32)
    mean = jnp.mean(u32, axis=-1, keepdims=True)
    var = jnp.mean(jnp.square(u32 - mean), axis=-1, keepdims=True)
    u32 = (u32 - mean) * lax.rsqrt(var + LN_EPS) * ln_g.astype(jnp.float32) + ln_b.astype(jnp.float32)
    u = jax.nn.silu(u32).astype(h.dtype)
    return u @ pw2_w + pw2_b


def sqrelu_mlp(h, w1, w2):
    return jnp.square(jax.nn.relu(h @ w1)) @ w2


def setup_inputs(seed: int = 0) -> dict:
    key = jax.random.key(seed)
    ks = iter(jax.random.split(key, 40))

    def nrm(shape, scale):
        return scale * jax.random.normal(next(ks), shape, jnp.float32)

    def uni(shape, lo, hi):
        return jax.random.uniform(next(ks), shape, jnp.float32, lo, hi)

    hn = (N_EVEN, RWKV_HEADS, HEAD_DIM)
    return {
        'x': nrm((BATCH, SEQ, D_MODEL), 1.0),
        'norm_mix_g': 1.0 + nrm((DEPTH, D_MODEL), 0.02),
        'norm_ffn_g': 1.0 + nrm((DEPTH, D_MODEL), 0.02),
        'final_norm_g': 1.0 + nrm((D_MODEL,), 0.02),
        'hy_w_in': nrm((N_EVEN, D_MODEL, IN_DIM), D_MODEL ** -0.5),
        'hy_mu': uni((N_EVEN, SHIFT_DIM), 0.0, 1.0),
        'hy_w0': uni((N_EVEN, RWKV_DIM), -5.0, -0.5),
        'hy_w_up': nrm((N_EVEN, LORA_W, RWKV_DIM), 0.1 * LORA_W ** -0.5),
        'hy_a0': nrm((N_EVEN, RWKV_DIM), 0.1),
        'hy_a_up': nrm((N_EVEN, LORA_A, RWKV_DIM), 0.1 * LORA_A ** -0.5),
        'hy_g_up': nrm((N_EVEN, LORA_G, RWKV_DIM), LORA_G ** -0.5),
        'hy_k_k': 0.85 + nrm(hn, 0.02),
        'hy_k_a': 1.0 + nrm(hn, 0.02),
        'hy_r_k': nrm(hn, 0.1),
        'hy_gn_g': 1.0 + nrm(hn, 0.02),
        'hy_gn_b': nrm(hn, 0.02),
        'hy_sinks': nrm((N_EVEN, ATT_Q_HEADS), 1.0),
        'hy_w_out': nrm((N_EVEN, D_MODEL, D_MODEL), D_MODEL ** -0.5),
        'cv_pw1_w': nrm((N_ODD, D_MODEL, 2 * D_MODEL), D_MODEL ** -0.5),
        'cv_pw1_b': nrm((N_ODD, 2 * D_MODEL), 0.02),
        'cv_dw_w': nrm((N_ODD, CONV_WIDTH, D_MODEL), CONV_WIDTH ** -0.5),
        'cv_dw_b': nrm((N_ODD, D_MODEL), 0.02),
        'cv_ln_g': 1.0 + nrm((N_ODD, D_MODEL), 0.02),
        'cv_ln_b': nrm((N_ODD, D_MODEL), 0.02),
        'cv_pw2_w': nrm((N_ODD, D_MODEL, D_MODEL), D_MODEL ** -0.5),
        'cv_pw2_b': nrm((N_ODD, D_MODEL), 0.02),
        'mlp_w1': nrm((DEPTH, D_MODEL, D_FF), D_MODEL ** -0.5),
        'mlp_w2': nrm((DEPTH, D_FF, D_MODEL), D_FF ** -0.5),
    }


def reference(x, norm_mix_g, norm_ffn_g, final_norm_g,
              hy_w_in, hy_mu, hy_w0, hy_w_up, hy_a0, hy_a_up, hy_g_up,
              hy_k_k, hy_k_a, hy_r_k, hy_gn_g, hy_gn_b, hy_sinks, hy_w_out,
              cv_pw1_w, cv_pw1_b, cv_dw_w, cv_dw_b, cv_ln_g, cv_ln_b, cv_pw2_w, cv_pw2_b,
              mlp_w1, mlp_w2):
    for layer in range(DEPTH):
        i = layer // 2
        h = rmsnorm(x, norm_mix_g[layer])
        if layer % 2 == 0:
            mix = hybrid_mixer(h, hy_w_in[i], hy_mu[i], hy_w0[i], hy_w_up[i], hy_a0[i],
                               hy_a_up[i], hy_g_up[i], hy_k_k[i], hy_k_a[i], hy_r_k[i],
                               hy_gn_g[i], hy_gn_b[i], hy_sinks[i], hy_w_out[i])
        else:
            mix = conformer_conv(h, cv_pw1_w[i], cv_pw1_b[i], cv_dw_w[i], cv_dw_b[i],
                                 cv_ln_g[i], cv_ln_b[i], cv_pw2_w[i], cv_pw2_b[i])
        x = x + mix.astype(x.dtype)
        h = rmsnorm(x, norm_ffn_g[layer])
        x = x + sqrelu_mlp(h, mlp_w1[layer], mlp_w2[layer]).astype(x.dtype)
    return rmsnorm(x, final_norm_g)
```

```python
import functools

import jax
import jax.numpy as jnp
from jax import lax
from jax.experimental import pallas as pl
from jax.experimental.pallas import tpu as pltpu

D_MODEL = 1024
HEAD_DIM = 64
RWKV_DIM = 512
ATT_Q_HEADS = 8
ATT_DIM = 512
KV_DIM = 128
LORA_W = 64
LORA_A = 64
LORA_G = 128
SHIFT_DIM = 3 * RWKV_DIM + LORA_W + LORA_A + LORA_G
ATT_COLS = ATT_DIM + 2 * KV_DIM
WINDOW = 128
CONV_WIDTH = 31
CONV_HALO = 32
D_FF = 4 * D_MODEL
RMS_EPS = 1e-6
LN_EPS = 1e-5
GN_EPS = 64e-5

LANES = 128
CHUNK = 64
VMEM_LIMIT = 56 * 1024 * 1024

F32 = jnp.float32
BF16 = jnp.bfloat16
HIGHEST = lax.Precision.HIGHEST
NT = (((1,), (1,)), ((), ()))
TN = (((0,), (0,)), ((), ()))


def _params(*sem):
    return pltpu.CompilerParams(dimension_semantics=sem, vmem_limit_bytes=VMEM_LIMIT)


def _rms(x, g):
    return x * lax.rsqrt(jnp.mean(x * x, axis=-1, keepdims=True) + RMS_EPS) * g


def _norm_proj_kernel(x_ref, g_ref, w_ref, o1_ref, o2_ref):
    h = _rms(x_ref[...], g_ref[...]).astype(BF16)
    acc = jnp.dot(h, w_ref[...], preferred_element_type=F32)
    o1_ref[...] = acc[:, :SHIFT_DIM]
    o2_ref[...] = acc[:, SHIFT_DIM:]


def _norm_proj(x2, g, w, tm=512):
    t = x2.shape[0]
    n = w.shape[1]
    return pl.pallas_call(
        _norm_proj_kernel,
        grid=(t // tm,),
        in_specs=[
            pl.BlockSpec((tm, D_MODEL), lambda i: (i, 0)),
            pl.BlockSpec((1, D_MODEL), lambda i: (0, 0)),
            pl.BlockSpec((D_MODEL, n), lambda i: (0, 0)),
        ],
        out_specs=[
            pl.BlockSpec((tm, SHIFT_DIM), lambda i: (i, 0)),
            pl.BlockSpec((tm, ATT_COLS), lambda i: (i, 0)),
        ],
        out_shape=[
            jax.ShapeDtypeStruct((t, SHIFT_DIM), F32),
            jax.ShapeDtypeStruct((t, ATT_COLS), F32),
        ],
        compiler_params=_params("parallel"),
        name="norm_proj",
    )(x2, g, w)


def _norm_glu_kernel(x_ref, g_ref, w_ref, b_ref, o_ref):
    h = _rms(x_ref[...], g_ref[...]).astype(BF16)
    acc = jnp.dot(h, w_ref[...], preferred_element_type=F32) + b_ref[...]
    o_ref[...] = acc[:, :D_MODEL] * jax.nn.sigmoid(acc[:, D_MODEL:])


def _norm_glu(x2, g, w, b, tm=512):
    t = x2.shape[0]
    return pl.pallas_call(
        _norm_glu_kernel,
        grid=(t // tm,),
        in_specs=[
            pl.BlockSpec((tm, D_MODEL), lambda i: (i, 0)),
            pl.BlockSpec((1, D_MODEL), lambda i: (0, 0)),
            pl.BlockSpec((D_MODEL, 2 * D_MODEL), lambda i: (0, 0)),
            pl.BlockSpec((1, 2 * D_MODEL), lambda i: (0, 0)),
        ],
        out_specs=pl.BlockSpec((tm, D_MODEL), lambda i: (i, 0)),
        out_shape=jax.ShapeDtypeStruct((t, D_MODEL), F32),
        compiler_params=_params("parallel"),
        name="norm_glu",
    )(x2, g, w, b)


def _rwkv_prep_kernel(p_ref, halo_ref, mu_ref, wl_ref, bias_ref,
                      r_ref, k_ref, v_ref, lw_ref, a_ref, g_ref):
    i = pl.program_id(1)
    x = p_ref[0]
    prev = jnp.where(i > 0, halo_ref[0, 7:8, :], 0.0)
    row = lax.broadcasted_iota(jnp.int32, x.shape, 0)
    shifted = jnp.where(row == 0, prev, pltpu.roll(x, 1, axis=0))
    xl = x + (shifted - x) * mu_ref[...]
    c1, c2, c3 = RWKV_DIM, 2 * RWKV_DIM, 3 * RWKV_DIM
    r_ref[0] = xl[:, :c1]
    k_ref[0] = xl[:, c1:c2]
    v_ref[0] = xl[:, c2:c3]
    z = xl[:, c3:]
    lane = lax.broadcasted_iota(jnp.int32, z.shape, 1)
    act = jnp.where(lane < LORA_W, jnp.tanh(z),
                    jnp.where(lane < LORA_W + LORA_A, z, jax.nn.sigmoid(z)))
    lo = jnp.dot(act.astype(BF16), wl_ref[...], preferred_element_type=F32) + bias_ref[...]
    wpre = lo[:, :c1]
    softplus = jnp.maximum(-wpre, 0.0) + jnp.log(1.0 + jnp.exp(-jnp.abs(wpre)))
    lw_ref[0] = -jnp.exp(-softplus - 0.5)
    a_ref[0] = jax.nn.sigmoid(lo[:, c1:c2])
    g_ref[0] = lo[:, c2:]


def _rwkv_prep(p3, mu, w_lora, bias, ts=512):
    b, s, _ = p3.shape
    out = jax.ShapeDtypeStruct((b, s, RWKV_DIM), F32)
    ospec = pl.BlockSpec((1, ts, RWKV_DIM), lambda bi, i: (bi, i, 0))
    return pl.pallas_call(
        _rwkv_prep_kernel,
        grid=(b, s // ts),
        in_specs=[
            pl.BlockSpec((1, ts, SHIFT_DIM), lambda bi, i: (bi, i, 0)),
            pl.BlockSpec((1, 8, SHIFT_DIM),
                         lambda bi, i: (bi, jnp.maximum(i * (ts // 8) - 1, 0), 0)),
            pl.BlockSpec((1, SHIFT_DIM), lambda bi, i: (0, 0)),
            pl.BlockSpec(w_lora.shape, lambda bi, i: (0, 0)),
            pl.BlockSpec((1, 3 * RWKV_DIM), lambda bi, i: (0, 0)),
        ],
        out_specs=[ospec] * 6,
        out_shape=[out] * 6,
        compiler_params=_params("parallel", "parallel"),
        name="rwkv_prep",
    )(p3, p3, mu, w_lora, bias)


def _rwkv_scan_kernel(r_ref, k_ref, v_ref, lw_ref, a_ref, g_ref,
                      kk_ref, ka_ref, rk_ref, gg_ref, gb_ref, o_ref):
    c_len = CHUNK
    n2 = 2 * c_len
    seq = r_ref.shape[1]
    lane = lax.broadcasted_iota(jnp.int32, (c_len, LANES), 1)
    lo = lane < HEAD_DIM
    lo2 = lax.broadcasted_iota(jnp.int32, (n2, LANES), 1) < HEAD_DIM
    top = lax.broadcasted_iota(jnp.int32, (n2, LANES), 0) < c_len
    own = lo2 == top

    def stack(x):
        return jnp.concatenate([jnp.where(lo, x, 0.0), jnp.where(lo, 0.0, x)], axis=0)

    ri = lax.broadcasted_iota(jnp.int32, (n2, n2), 0)
    ci = lax.broadcasted_iota(jnp.int32, (n2, n2), 1)
    strict = ri > ci
    incl = ri >= ci
    eye = (ri == ci).astype(F32)
    tri = (lax.broadcasted_iota(jnp.int32, (c_len, c_len), 0)
           >= lax.broadcasted_iota(jnp.int32, (c_len, c_len), 1)).astype(F32)
    kkp, kap, rkp = kk_ref[...], ka_ref[...], rk_ref[...]
    ggp, gbp = gg_ref[...], gb_ref[...]

    def dot(a, b):
        return jnp.dot(a, b, precision=HIGHEST, preferred_element_type=F32)

    def body(c, state):
        sl = pl.ds(pl.multiple_of(c * c_len, c_len), c_len)
        r, k, v = r_ref[0, sl, :], k_ref[0, sl, :], v_ref[0, sl, :]
        lw, a, g = lw_ref[0, sl, :], a_ref[0, sl, :], g_ref[0, sl, :]
        cum = dot(tri, lw)
        cum_last = cum[c_len - 1:c_len, :]
        r_s, k_s, v_s, a_s = stack(r), stack(k), stack(v), stack(a)
        cum_s = jnp.concatenate([cum, cum], axis=0)
        lw_s = jnp.concatenate([lw, lw], axis=0)

        kk = k_s * kkp
        kk = kk / jnp.maximum(jnp.sqrt(jnp.sum(kk * kk, axis=-1, keepdims=True)), 1e-12)
        kmod = k_s * (1.0 + (a_s - 1.0) * kap)
        avec = -kk
        bvec = kk * a_s

        at = avec * jnp.exp(cum_s - lw_s)
        rt = r_s * jnp.exp(cum_s)
        einv = jnp.exp(-cum_s)
        bt = bvec * einv
        kt = kmod * einv
        edec = jnp.exp(cum_last - cum_s)
        bh = bvec * edec
        kh = kmod * edec

        gram = lax.dot_general(jnp.concatenate([at, rt], axis=0),
                               jnp.concatenate([bt, kt], axis=0), NT,
                               precision=HIGHEST, preferred_element_type=F32)
        a_ab = jnp.where(strict, gram[:n2, :n2], 0.0)
        a_ak = jnp.where(strict, gram[:n2, n2:], 0.0)
        r_b = jnp.where(incl, gram[n2:, :n2], 0.0)
        r_k = jnp.where(incl, gram[n2:, n2:], 0.0)

        tinv = eye + a_ab
        apow = a_ab
        for _ in range(5):
            apow = dot(apow, apow)
            tinv = tinv + dot(tinv, apow)

        s_t = lax.dot_general(jnp.concatenate([at, rt], axis=0), state, NT,
                              precision=HIGHEST, preferred_element_type=F32)
        u = dot(tinv, s_t[:n2] + dot(a_ak, v_s))
        y = s_t[n2:] + dot(r_b, u) + dot(r_k, v_s)
        new_state = state * jnp.exp(cum_last) + lax.dot_general(
            jnp.concatenate([u, v_s], axis=0), jnp.concatenate([bh, kh], axis=0), TN,
            precision=HIGHEST, preferred_element_type=F32)

        inv_n = 1.0 / HEAD_DIM
        mean = jnp.sum(y, axis=-1, keepdims=True) * inv_n
        yc = jnp.where(own, y - mean, 0.0)
        var = jnp.sum(yc * yc, axis=-1, keepdims=True) * inv_n
        yn = yc * lax.rsqrt(var + GN_EPS) * ggp + gbp
        bonus = jnp.sum(r_s * kmod * rkp, axis=-1, keepdims=True) * v_s
        out = yn + bonus
        out = jnp.where(lo, out[:c_len], out[c_len:])
        o_ref[0, sl, :] = (out * g).astype(o_ref.dtype)
        return new_state

    lax.fori_loop(0, seq // c_len, body, jnp.zeros((LANES, LANES), F32))


def _rwkv_scan(r, k, v, lw, a, g, kkp, kap, rkp, ggp, gbp):
    b, s, _ = r.shape
    xspec = pl.BlockSpec((1, s, LANES), lambda bi, j: (bi, 0, j))
    pspec = pl.BlockSpec((1, LANES), lambda bi, j: (0, j))
    return pl.pallas_call(
        _rwkv_scan_kernel,
        grid=(b, RWKV_DIM // LANES),
        in_specs=[xspec] * 6 + [pspec] * 5,
        out_specs=xspec,
        out_shape=jax.ShapeDtypeStruct((b, s, RWKV_DIM), BF16),
        compiler_params=_params("parallel", "parallel"),
        name="rwkv_scan",
    )(r, k, v, lw, a, g, kkp, kap, rkp, ggp, gbp)


def _swa_kernel(sink_ref, q_ref, kvc_ref, kvp_ref, o_ref):
    n = pl.program_id(1)
    blk = WINDOW
    kv = jnp.concatenate([kvp_ref[0], kvc_ref[0]], axis=0)
    kmat, vmat = kv[:, :KV_DIM], kv[:, KV_DIM:]
    lo_kv = lax.broadcasted_iota(jnp.int32, (2 * blk, LANES), 1) < HEAD_DIM
    k_sw = pltpu.roll(kmat, HEAD_DIM, axis=1)
    v_sw = pltpu.roll(vmat, HEAD_DIM, axis=1)
    k_dup = [jnp.where(lo_kv, kmat, k_sw).astype(BF16), jnp.where(lo_kv, k_sw, kmat).astype(BF16)]
    v_dup = [jnp.where(lo_kv, vmat, v_sw).astype(BF16), jnp.where(lo_kv, v_sw, vmat).astype(BF16)]

    lo_q = lax.broadcasted_iota(jnp.int32, (blk, LANES), 1) < HEAD_DIM
    rows = lax.broadcasted_iota(jnp.int32, (2 * blk, 2 * blk), 0)
    cols = lax.broadcasted_iota(jnp.int32, (2 * blk, 2 * blk), 1)
    rel = (rows & (blk - 1)) + blk - cols
    valid = (rel >= 0) & (rel < WINDOW) & ((cols >= blk) | (n > 0))
    relf = rel.astype(F32)
    first = lax.broadcasted_iota(jnp.int32, (2 * blk, 1), 0) < blk

    for j in range(ATT_Q_HEADS // 2):
        grp = j // 2
        qp = q_ref[0, :, j * LANES:(j + 1) * LANES] * (HEAD_DIM ** -0.5)
        qs = jnp.concatenate([jnp.where(lo_q, qp, 0.0), jnp.where(lo_q, 0.0, qp)], axis=0)
        s = lax.dot_general(qs.astype(BF16), k_dup[grp], NT, preferred_element_type=F32)
        slope = jnp.where(first, 2.0 ** -(2 * j + 1), 2.0 ** -(2 * j + 2))
        sink = jnp.where(first, sink_ref[2 * j], sink_ref[2 * j + 1])
        s = jnp.where(valid, s - slope * relf, -jnp.inf)
        m = jnp.maximum(jnp.max(s, axis=-1, keepdims=True), sink)
        e = jnp.exp(s - m)
        probs = e / (jnp.sum(e, axis=-1, keepdims=True) + jnp.exp(sink - m))
        o = jnp.dot(probs.astype(BF16), v_dup[grp], preferred_element_type=F32)
        o_ref[0, :, j * LANES:(j + 1) * LANES] = jnp.where(lo_q, o[:blk], o[blk:]).astype(o_ref.dtype)


def _swa(p_att3, sinks):
    b, s, _ = p_att3.shape
    blk = WINDOW
    return pl.pallas_call(
        _swa_kernel,
        grid=(b, s // blk),
        in_specs=[
            pl.BlockSpec(memory_space=pltpu.SMEM),
            pl.BlockSpec((1, blk, ATT_DIM), lambda bi, n: (bi, n, 0)),
            pl.BlockSpec((1, blk, 2 * KV_DIM), lambda bi, n: (bi, n, ATT_DIM // (2 * KV_DIM))),
            pl.BlockSpec((1, blk, 2 * KV_DIM),
                         lambda bi, n: (bi, jnp.maximum(n - 1, 0), ATT_DIM // (2 * KV_DIM))),
        ],
        out_specs=pl.BlockSpec((1, blk, ATT_DIM), lambda bi, n: (bi, n, 0)),
        out_shape=jax.ShapeDtypeStruct((b, s, ATT_DIM), BF16),
        compiler_params=_params("parallel", "parallel"),
        name="swa_attention",
    )(sinks, p_att3, p_att3, p_att3)


def _conv_kernel(u_ref, halo_ref, w_ref, b_ref, lg_ref, lb_ref, o_ref, buf_ref, acc_ref):
    i = pl.program_id(1)
    ts = u_ref.shape[1]
    rb = 64
    buf_ref[0:CONV_HALO, :] = jnp.where(i > 0, halo_ref[0], 0.0)
    buf_ref[CONV_HALO:, :] = u_ref[0]
    off = CONV_HALO - (CONV_WIDTH - 1)
    for cb in range(D_MODEL // LANES):
        cs = slice(cb * LANES, (cb + 1) * LANES)
        for r0 in range(0, ts, rb):
            acc = jnp.zeros((rb, LANES), F32) + b_ref[:, cs]
            for j in range(CONV_WIDTH):
                acc = acc + w_ref[j:j + 1, cs] * buf_ref[r0 + off + j:r0 + off + j + rb, cs]
            acc_ref[r0:r0 + rb, cs] = acc
    y = acc_ref[...]
    mean = jnp.mean(y, axis=-1, keepdims=True)
    yc = y - mean
    var = jnp.mean(yc * yc, axis=-1, keepdims=True)
    yn = yc * lax.rsqrt(var + LN_EPS) * lg_ref[...] + lb_ref[...]
    o_ref[0] = (yn * jax.nn.sigmoid(yn)).astype(o_ref.dtype)


def _conv_ln_silu(u3, dw_w, dw_b, ln_g, ln_b, ts=256):
    b, s, _ = u3.shape
    vec = pl.BlockSpec((1, D_MODEL), lambda bi, i: (0, 0))
    return pl.pallas_call(
        _conv_kernel,
        grid=(b, s // ts),
        in_specs=[
            pl.BlockSpec((1, ts, D_MODEL), lambda bi, i: (bi, i, 0)),
            pl.BlockSpec((1, CONV_HALO, D_MODEL),
                         lambda bi, i: (bi, jnp.maximum(i * (ts // CONV_HALO) - 1, 0), 0)),
            pl.BlockSpec((CONV_WIDTH, D_MODEL), lambda bi, i: (0, 0)),
            vec, vec, vec,
        ],
        out_specs=pl.BlockSpec((1, ts, D_MODEL), lambda bi, i: (bi, i, 0)),
        out_shape=jax.ShapeDtypeStruct((b, s, D_MODEL), BF16),
        scratch_shapes=[pltpu.VMEM((ts + CONV_HALO, D_MODEL), F32),
                        pltpu.VMEM((ts, D_MODEL), F32)],
        compiler_params=_params("parallel", "parallel"),
        name="conv_ln_silu",
    )(u3, u3, dw_w, dw_b, ln_g, ln_b)


def _proj_mlp_kernel(*refs, n_proj, final_norm):
    x_ref = refs[0]
    y_refs = refs[1:1 + n_proj]
    wp_refs = refs[1 + n_proj:1 + 2 * n_proj]
    pb_ref, g_ref, w1_ref, w2_ref, gf_ref, o_ref, x1_ref, h_ref, acc_ref = refs[1 + 2 * n_proj:]
    j = pl.program_id(1)

    @pl.when(j == 0)
    def _():
        x1 = x_ref[...] + pb_ref[...]
        for y_ref, wp_ref in zip(y_refs, wp_refs):
            x1 = x1 + jnp.dot(y_ref[...], wp_ref[...], preferred_element_type=F32)
        x1_ref[...] = x1
        h_ref[...] = _rms(x1, g_ref[...]).astype(BF16)
        acc_ref[...] = jnp.zeros_like(acc_ref)

    hid = jnp.dot(h_ref[...], w1_ref[...], preferred_element_type=F32)
    hid = jnp.square(jnp.maximum(hid, 0.0)).astype(BF16)
    acc_ref[...] += jnp.dot(hid, w2_ref[...], preferred_element_type=F32)

    @pl.when(j == pl.num_programs(1) - 1)
    def _():
        out = x1_ref[...] + acc_ref[...]
        if final_norm:
            out = _rms(out, gf_ref[...])
        o_ref[...] = out


def _proj_mlp(x2, ys, wps, pb, g, w1, w2, gf, final_norm, tm=512, tf=1024):
    t = x2.shape[0]
    n_proj = len(ys)
    vec = pl.BlockSpec((1, D_MODEL), lambda i, j: (0, 0))
    in_specs = [pl.BlockSpec((tm, D_MODEL), lambda i, j: (i, 0))]
    in_specs += [pl.BlockSpec((tm, y.shape[1]), lambda i, j: (i, 0)) for y in ys]
    in_specs += [pl.BlockSpec(w.shape, lambda i, j: (0, 0)) for w in wps]
    in_specs += [vec, vec,
                 pl.BlockSpec((D_MODEL, tf), lambda i, j: (0, j)),
                 pl.BlockSpec((tf, D_MODEL), lambda i, j: (j, 0)),
                 vec]
    return pl.pallas_call(
        functools.partial(_proj_mlp_kernel, n_proj=n_proj, final_norm=final_norm),
        grid=(t // tm, D_FF // tf),
        in_specs=in_specs,
        out_specs=pl.BlockSpec((tm, D_MODEL), lambda i, j: (i, 0)),
        out_shape=jax.ShapeDtypeStruct((t, D_MODEL), F32),
        scratch_shapes=[pltpu.VMEM((tm, D_MODEL), F32),
                        pltpu.VMEM((tm, D_MODEL), BF16),
                        pltpu.VMEM((tm, D_MODEL), F32)],
        compiler_params=_params("parallel", "arbitrary"),
        name="proj_mlp",
    )(x2, *ys, *wps, pb, g, w1, w2, gf)


def _row(v):
    return v.reshape(1, -1).astype(F32)


def kernel(x, norm_mix_g, norm_ffn_g, final_norm_g, hy_w_in, hy_mu, hy_w0, hy_w_up, hy_a0, hy_a_up, hy_g_up, hy_k_k, hy_k_a, hy_r_k, hy_gn_g, hy_gn_b, hy_sinks, hy_w_out, cv_pw1_w, cv_pw1_b, cv_dw_w, cv_dw_b, cv_ln_g, cv_ln_b, cv_pw2_w, cv_pw2_b, mlp_w1, mlp_w2):
    bsz, seq, d = x.shape
    depth = norm_mix_g.shape[0]
    t = bsz * seq
    x2 = x.reshape(t, d)
    zero_row = jnp.zeros((1, D_MODEL), F32)
    gf = _row(final_norm_g)

    for layer in range(depth):
        i = layer // 2
        g_mix = _row(norm_mix_g[layer])
        if layer % 2 == 0:
            p_rwkv, p_att = _norm_proj(x2, g_mix, hy_w_in[i].astype(BF16))
            w_lora = jnp.zeros((LORA_W + LORA_A + LORA_G, 3 * RWKV_DIM), F32)
            w_lora = w_lora.at[:LORA_W, :RWKV_DIM].set(hy_w_up[i])
            w_lora = w_lora.at[LORA_W:LORA_W + LORA_A, RWKV_DIM:2 * RWKV_DIM].set(hy_a_up[i])
            w_lora = w_lora.at[LORA_W + LORA_A:, 2 * RWKV_DIM:].set(hy_g_up[i])
            bias = jnp.concatenate([hy_w0[i], hy_a0[i], jnp.zeros((RWKV_DIM,), F32)]).reshape(1, -1)
            r, k, v, lw, a, g = _rwkv_prep(p_rwkv.reshape(bsz, seq, SHIFT_DIM), _row(hy_mu[i]),
                                           w_lora.astype(BF16), bias)
            y_rwkv = _rwkv_scan(r, k, v, lw, a, g, _row(hy_k_k[i]), _row(hy_k_a[i]),
                                _row(hy_r_k[i]), _row(hy_gn_g[i]), _row(hy_gn_b[i]))
            y_att = _swa(p_att.reshape(bsz, seq, ATT_COLS), hy_sinks[i].astype(F32))
            w_out = hy_w_out[i].astype(BF16)
            ys = [y_rwkv.reshape(t, RWKV_DIM), y_att.reshape(t, ATT_DIM)]
            wps = [w_out[:RWKV_DIM], w_out[RWKV_DIM:]]
            pb = zero_row
        else:
            u = _norm_glu(x2, g_mix, cv_pw1_w[i].astype(BF16), _row(cv_pw1_b[i]))
            u = _conv_ln_silu(u.reshape(bsz, seq, D_MODEL), cv_dw_w[i].astype(F32),
                              _row(cv_dw_b[i]), _row(cv_ln_g[i]), _row(cv_ln_b[i]))
            ys = [u.reshape(t, D_MODEL)]
            wps = [cv_pw2_w[i].astype(BF16)]
            pb = _row(cv_pw2_b[i])
        x2 = _proj_mlp(x2, ys, wps, pb, _row(norm_ffn_g[layer]),
                       mlp_w1[layer].astype(BF16), mlp_w2[layer].astype(BF16), gf,
                       final_norm=(layer == depth - 1))
    return x2.reshape(bsz, seq, d)
```

```python
import functools

import jax
import jax.numpy as jnp
from jax import lax
from jax.experimental import pallas as pl
from jax.experimental.pallas import tpu as pltpu

D_MODEL = 1024
HEAD_DIM = 64
RWKV_DIM = 512
ATT_Q_HEADS = 8
ATT_DIM = 512
KV_DIM = 128
LORA_W = 64
LORA_A = 64
LORA_G = 128
SHIFT_DIM = 3 * RWKV_DIM + LORA_W + LORA_A + LORA_G
ATT_COLS = ATT_DIM + 2 * KV_DIM
WINDOW = 128
CONV_WIDTH = 31
CONV_HALO = 32
D_FF = 4 * D_MODEL
RMS_EPS = 1e-6
LN_EPS = 1e-5
GN_EPS = 64e-5

LANES = 128
CHUNK = 64
VMEM_LIMIT = 56 * 1024 * 1024

F32 = jnp.float32
BF16 = jnp.bfloat16
NN = (((1,), (0,)), ((), ()))
NT = (((1,), (1,)), ((), ()))
TN = (((0,), (0,)), ((), ()))


def _params(*sem):
    return pltpu.CompilerParams(dimension_semantics=sem, vmem_limit_bytes=VMEM_LIMIT)


def _rms(x, g):
    return x * lax.rsqrt(jnp.mean(x * x, axis=-1, keepdims=True) + RMS_EPS) * g


def _norm_proj_kernel(x_ref, g_ref, w_ref, o1_ref, o2_ref):
    h = _rms(x_ref[...], g_ref[...]).astype(BF16)
    acc = jnp.dot(h, w_ref[...], preferred_element_type=F32)
    o1_ref[...] = acc[:, :SHIFT_DIM]
    o2_ref[...] = acc[:, SHIFT_DIM:]


def _norm_proj(x2, g, w, tm=512):
    t = x2.shape[0]
    n = w.shape[1]
    return pl.pallas_call(
        _norm_proj_kernel,
        grid=(t // tm,),
        in_specs=[
            pl.BlockSpec((tm, D_MODEL), lambda i: (i, 0)),
            pl.BlockSpec((1, D_MODEL), lambda i: (0, 0)),
            pl.BlockSpec((D_MODEL, n), lambda i: (0, 0)),
        ],
        out_specs=[
            pl.BlockSpec((tm, SHIFT_DIM), lambda i: (i, 0)),
            pl.BlockSpec((tm, ATT_COLS), lambda i: (i, 0)),
        ],
        out_shape=[
            jax.ShapeDtypeStruct((t, SHIFT_DIM), F32),
            jax.ShapeDtypeStruct((t, ATT_COLS), F32),
        ],
        compiler_params=_params("parallel"),
        name="norm_proj",
    )(x2, g, w)


def _norm_glu_kernel(x_ref, g_ref, w_ref, b_ref, o_ref):
    h = _rms(x_ref[...], g_ref[...]).astype(BF16)
    acc = jnp.dot(h, w_ref[...], preferred_element_type=F32) + b_ref[...]
    o_ref[...] = acc[:, :D_MODEL] * jax.nn.sigmoid(acc[:, D_MODEL:])


def _norm_glu(x2, g, w, b, tm=512):
    t = x2.shape[0]
    return pl.pallas_call(
        _norm_glu_kernel,
        grid=(t // tm,),
        in_specs=[
            pl.BlockSpec((tm, D_MODEL), lambda i: (i, 0)),
            pl.BlockSpec((1, D_MODEL), lambda i: (0, 0)),
            pl.BlockSpec((D_MODEL, 2 * D_MODEL), lambda i: (0, 0)),
            pl.BlockSpec((1, 2 * D_MODEL), lambda i: (0, 0)),
        ],
        out_specs=pl.BlockSpec((tm, D_MODEL), lambda i: (i, 0)),
        out_shape=jax.ShapeDtypeStruct((t, D_MODEL), F32),
        compiler_params=_params("parallel"),
        name="norm_glu",
    )(x2, g, w, b)


def _rwkv_prep_kernel(p_ref, halo_ref, mu_ref, wl_ref, bias_ref,
                      r_ref, k_ref, v_ref, lw_ref, a_ref, g_ref):
    i = pl.program_id(1)
    x = p_ref[0]
    prev = jnp.where(i > 0, halo_ref[0, 7:8, :], 0.0)
    row = lax.broadcasted_iota(jnp.int32, x.shape, 0)
    shifted = jnp.where(row == 0, prev, pltpu.roll(x, 1, axis=0))
    xl = x + (shifted - x) * mu_ref[...]
    c1, c2, c3 = RWKV_DIM, 2 * RWKV_DIM, 3 * RWKV_DIM
    r_ref[0] = xl[:, :c1]
    k_ref[0] = xl[:, c1:c2]
    v_ref[0] = xl[:, c2:c3]
    z = xl[:, c3:]
    lane = lax.broadcasted_iota(jnp.int32, z.shape, 1)
    act = jnp.where(lane < LORA_W, jnp.tanh(z),
                    jnp.where(lane < LORA_W + LORA_A, z, jax.nn.sigmoid(z)))
    lo = jnp.dot(act.astype(BF16), wl_ref[...], preferred_element_type=F32) + bias_ref[...]
    wpre = lo[:, :c1]
    softplus = jnp.maximum(-wpre, 0.0) + jnp.log(1.0 + jnp.exp(-jnp.abs(wpre)))
    lw_ref[0] = -jnp.exp(-softplus - 0.5)
    a_ref[0] = jax.nn.sigmoid(lo[:, c1:c2])
    g_ref[0] = lo[:, c2:]


def _rwkv_prep(p3, mu, w_lora, bias, ts=512):
    b, s, _ = p3.shape
    out = jax.ShapeDtypeStruct((b, s, RWKV_DIM), F32)
    ospec = pl.BlockSpec((1, ts, RWKV_DIM), lambda bi, i: (bi, i, 0))
    return pl.pallas_call(
        _rwkv_prep_kernel,
        grid=(b, s // ts),
        in_specs=[
            pl.BlockSpec((1, ts, SHIFT_DIM), lambda bi, i: (bi, i, 0)),
            pl.BlockSpec((1, 8, SHIFT_DIM),
                         lambda bi, i: (bi, jnp.maximum(i * (ts // 8) - 1, 0), 0)),
            pl.BlockSpec((1, SHIFT_DIM), lambda bi, i: (0, 0)),
            pl.BlockSpec(w_lora.shape, lambda bi, i: (0, 0)),
            pl.BlockSpec((1, 3 * RWKV_DIM), lambda bi, i: (0, 0)),
        ],
        out_specs=[ospec] * 6,
        out_shape=[out] * 6,
        compiler_params=_params("parallel", "parallel"),
        name="rwkv_prep",
    )(p3, p3, mu, w_lora, bias)


def _split3(x):
    hi = x.astype(BF16)
    r1 = x - hi.astype(F32)
    mid = r1.astype(BF16)
    lo = (r1 - mid.astype(F32)).astype(BF16)
    return hi, mid, lo


def _rwkv_scan_kernel(r_ref, k_ref, v_ref, lw_ref, a_ref, g_ref,
                      kk_ref, ka_ref, rk_ref, gg_ref, gb_ref, o_ref, state_ref):
    c_len = CHUNK
    n2 = 2 * c_len
    ts = r_ref.shape[1]
    n_pairs = RWKV_DIM // LANES
    lane = lax.broadcasted_iota(jnp.int32, (c_len, LANES), 1)
    lo = lane < HEAD_DIM
    lo2 = lax.broadcasted_iota(jnp.int32, (n2, LANES), 1) < HEAD_DIM
    top = lax.broadcasted_iota(jnp.int32, (n2, LANES), 0) < c_len
    own = lo2 == top

    def stack(x):
        return jnp.concatenate([jnp.where(lo, x, 0.0), jnp.where(lo, 0.0, x)], axis=0)

    ri = lax.broadcasted_iota(jnp.int32, (n2, n2), 0)
    ci = lax.broadcasted_iota(jnp.int32, (n2, n2), 1)
    strict = ri > ci
    incl = ri >= ci
    eye = (ri == ci).astype(F32)
    blk = [(ri >> l) == (ci >> l) for l in range(7)]
    tri = (lax.broadcasted_iota(jnp.int32, (c_len, c_len), 0)
           >= lax.broadcasted_iota(jnp.int32, (c_len, c_len), 1)).astype(BF16)

    @pl.when(pl.program_id(1) == 0)
    def _():
        state_ref[...] = jnp.zeros_like(state_ref)

    pairs = range(n_pairs)

    def mm(a, b, dims=NN):
        return lax.dot_general(a.astype(BF16), b.astype(BF16), dims, preferred_element_type=F32)

    def body(c, carry):
        sl = pl.ds(pl.multiple_of(c * c_len, c_len), c_len)
        lw_all = lw_ref[0, sl, :]
        cum3 = jnp.dot(tri, jnp.concatenate(_split3(lw_all), axis=1), preferred_element_type=F32)
        cum_all = cum3[:, :RWKV_DIM] + cum3[:, RWKV_DIM:2 * RWKV_DIM] + cum3[:, 2 * RWKV_DIM:]

        at_rt, bt_kt, bh_kh, v_st, decay, bonus = [], [], [], [], [], []
        for j in pairs:
            cs = slice(j * LANES, (j + 1) * LANES)
            cum, lw = cum_all[:, cs], lw_all[:, cs]
            cum_last = cum[c_len - 1:c_len, :]
            r_s, k_s = stack(r_ref[0, sl, cs]), stack(k_ref[0, sl, cs])
            v_s, a_s = stack(v_ref[0, sl, cs]), stack(a_ref[0, sl, cs])
            cum_s = jnp.concatenate([cum, cum], axis=0)
            lw_s = jnp.concatenate([lw, lw], axis=0)
            kk = k_s * kk_ref[:, cs]
            kk = kk / jnp.maximum(jnp.sqrt(jnp.sum(kk * kk, axis=-1, keepdims=True)), 1e-12)
            kmod = k_s * (1.0 + (a_s - 1.0) * ka_ref[:, cs])
            bvec = kk * a_s
            einv = jnp.exp(-cum_s)
            edec = jnp.exp(cum_last - cum_s)
            at_rt.append(jnp.concatenate([-kk * jnp.exp(cum_s - lw_s), r_s * jnp.exp(cum_s)],
                                         axis=0).astype(BF16))
            bt_kt.append(jnp.concatenate([bvec * einv, kmod * einv], axis=0).astype(BF16))
            bh_kh.append(jnp.concatenate([bvec * edec, kmod * edec], axis=0).astype(BF16))
            v_st.append(v_s)
            decay.append(jnp.exp(cum_last))
            bonus.append(jnp.sum(r_s * kmod * rk_ref[:, cs], axis=-1, keepdims=True) * v_s)

        gram = [mm(at_rt[j], bt_kt[j], NT) for j in pairs]
        a_ab = [jnp.where(strict, gram[j][:n2, :n2], 0.0) for j in pairs]
        a_ak = [jnp.where(strict, gram[j][:n2, n2:], 0.0).astype(BF16) for j in pairs]
        r_b = [jnp.where(incl, gram[j][n2:, :n2], 0.0).astype(BF16) for j in pairs]
        r_k = [jnp.where(incl, gram[j][n2:, n2:], 0.0).astype(BF16) for j in pairs]
        v_bf = [v_st[j].astype(BF16) for j in pairs]

        tinv = [eye + jnp.where(blk[1], a_ab[j], 0.0) for j in pairs]
        for lvl in range(2, 7):
            off = blk[lvl] & ~blk[lvl - 1]
            t_bf = [tinv[j].astype(BF16) for j in pairs]
            w = [mm(jnp.where(off, a_ab[j], 0.0), t_bf[j]) for j in pairs]
            tinv = [tinv[j] + mm(t_bf[j], w[j]) for j in pairs]

        akv = [mm(a_ak[j], v_bf[j]) for j in pairs]
        state = [state_ref[j] for j in pairs]
        s_t = [mm(at_rt[j], state[j], NT) for j in pairs]
        u = [mm(tinv[j], s_t[j][:n2] + akv[j]) for j in pairs]
        u_bf = [u[j].astype(BF16) for j in pairs]
        for j in pairs:
            state_ref[j] = state[j] * decay[j] + mm(
                jnp.concatenate([u_bf[j], v_bf[j]], axis=0), bh_kh[j], TN)
        y = [s_t[j][n2:] + mm(r_b[j], u_bf[j]) + mm(r_k[j], v_bf[j]) for j in pairs]

        inv_n = 1.0 / HEAD_DIM
        for j in pairs:
            cs = slice(j * LANES, (j + 1) * LANES)
            mean = jnp.sum(y[j], axis=-1, keepdims=True) * inv_n
            yc = jnp.where(own, y[j] - mean, 0.0)
            var = jnp.sum(yc * yc, axis=-1, keepdims=True) * inv_n
            out = yc * lax.rsqrt(var + GN_EPS) * gg_ref[:, cs] + gb_ref[:, cs] + bonus[j]
            out = jnp.where(lo, out[:c_len], out[c_len:])
            o_ref[0, sl, cs] = (out * g_ref[0, sl, cs]).astype(o_ref.dtype)
        return carry

    lax.fori_loop(0, ts // c_len, body, 0)


def _rwkv_scan(r, k, v, lw, a, g, kkp, kap, rkp, ggp, gbp, ts=512):
    b, s, _ = r.shape
    xspec = pl.BlockSpec((1, ts, RWKV_DIM), lambda bi, i: (bi, i, 0))
    pspec = pl.BlockSpec((1, RWKV_DIM), lambda bi, i: (0, 0))
    return pl.pallas_call(
        _rwkv_scan_kernel,
        grid=(b, s // ts),
        in_specs=[xspec] * 6 + [pspec] * 5,
        out_specs=xspec,
        out_shape=jax.ShapeDtypeStruct((b, s, RWKV_DIM), BF16),
        scratch_shapes=[pltpu.VMEM((RWKV_DIM // LANES, LANES, LANES), F32)],
        compiler_params=_params("parallel", "arbitrary"),
        name="rwkv_scan",
    )(r, k, v, lw, a, g, kkp, kap, rkp, ggp, gbp)


def _swa_kernel(sink_ref, q_ref, kvc_ref, kvp_ref, o_ref):
    n = pl.program_id(1)
    blk = WINDOW
    kv = jnp.concatenate([kvp_ref[0], kvc_ref[0]], axis=0)
    kmat, vmat = kv[:, :KV_DIM], kv[:, KV_DIM:]
    lo_kv = lax.broadcasted_iota(jnp.int32, (2 * blk, LANES), 1) < HEAD_DIM
    k_sw = pltpu.roll(kmat, HEAD_DIM, axis=1)
    v_sw = pltpu.roll(vmat, HEAD_DIM, axis=1)
    k_dup = [jnp.where(lo_kv, kmat, k_sw).astype(BF16), jnp.where(lo_kv, k_sw, kmat).astype(BF16)]
    v_dup = [jnp.where(lo_kv, vmat, v_sw).astype(BF16), jnp.where(lo_kv, v_sw, vmat).astype(BF16)]

    lo_q = lax.broadcasted_iota(jnp.int32, (blk, LANES), 1) < HEAD_DIM
    rows = lax.broadcasted_iota(jnp.int32, (2 * blk, 2 * blk), 0)
    cols = lax.broadcasted_iota(jnp.int32, (2 * blk, 2 * blk), 1)
    rel = (rows & (blk - 1)) + blk - cols
    valid = (rel >= 0) & (rel < WINDOW) & ((cols >= blk) | (n > 0))
    relf = rel.astype(F32)
    first = lax.broadcasted_iota(jnp.int32, (2 * blk, 1), 0) < blk

    for j in range(ATT_Q_HEADS // 2):
        grp = j // 2
        qp = q_ref[0, :, j * LANES:(j + 1) * LANES] * (HEAD_DIM ** -0.5)
        qs = jnp.concatenate([jnp.where(lo_q, qp, 0.0), jnp.where(lo_q, 0.0, qp)], axis=0)
        s = lax.dot_general(qs.astype(BF16), k_dup[grp], NT, preferred_element_type=F32)
        slope = jnp.where(first, 2.0 ** -(2 * j + 1), 2.0 ** -(2 * j + 2))
        sink = jnp.where(first, sink_ref[2 * j], sink_ref[2 * j + 1])
        s = jnp.where(valid, s - slope * relf, -jnp.inf)
        m = jnp.maximum(jnp.max(s, axis=-1, keepdims=True), sink)
        e = jnp.exp(s - m)
        probs = e / (jnp.sum(e, axis=-1, keepdims=True) + jnp.exp(sink - m))
        o = jnp.dot(probs.astype(BF16), v_dup[grp], preferred_element_type=F32)
        o_ref[0, :, j * LANES:(j + 1) * LANES] = jnp.where(lo_q, o[:blk], o[blk:]).astype(o_ref.dtype)


def _swa(p_att3, sinks):
    b, s, _ = p_att3.shape
    blk = WINDOW
    return pl.pallas_call(
        _swa_kernel,
        grid=(b, s // blk),
        in_specs=[
            pl.BlockSpec(memory_space=pltpu.SMEM),
            pl.BlockSpec((1, blk, ATT_DIM), lambda bi, n: (bi, n, 0)),
            pl.BlockSpec((1, blk, 2 * KV_DIM), lambda bi, n: (bi, n, ATT_DIM // (2 * KV_DIM))),
            pl.BlockSpec((1, blk, 2 * KV_DIM),
                         lambda bi, n: (bi, jnp.maximum(n - 1, 0), ATT_DIM // (2 * KV_DIM))),
        ],
        out_specs=pl.BlockSpec((1, blk, ATT_DIM), lambda bi, n: (bi, n, 0)),
        out_shape=jax.ShapeDtypeStruct((b, s, ATT_DIM), BF16),
        compiler_params=_params("parallel", "parallel"),
        name="swa_attention",
    )(sinks, p_att3, p_att3, p_att3)


def _conv_kernel(u_ref, halo_ref, w_ref, b_ref, lg_ref, lb_ref, o_ref, buf_ref, acc_ref):
    i = pl.program_id(1)
    ts = u_ref.shape[1]
    rb = 64
    buf_ref[0:CONV_HALO, :] = jnp.where(i > 0, halo_ref[0], 0.0)
    buf_ref[CONV_HALO:, :] = u_ref[0]
    off = CONV_HALO - (CONV_WIDTH - 1)
    for cb in range(D_MODEL // LANES):
        cs = slice(cb * LANES, (cb + 1) * LANES)
        for r0 in range(0, ts, rb):
            acc = jnp.zeros((rb, LANES), F32) + b_ref[:, cs]
            for j in range(CONV_WIDTH):
                acc = acc + w_ref[j:j + 1, cs] * buf_ref[r0 + off + j:r0 + off + j + rb, cs]
            acc_ref[r0:r0 + rb, cs] = acc
    y = acc_ref[...]
    mean = jnp.mean(y, axis=-1, keepdims=True)
    yc = y - mean
    var = jnp.mean(yc * yc, axis=-1, keepdims=True)
    yn = yc * lax.rsqrt(var + LN_EPS) * lg_ref[...] + lb_ref[...]
    o_ref[0] = (yn * jax.nn.sigmoid(yn)).astype(o_ref.dtype)


def _conv_ln_silu(u3, dw_w, dw_b, ln_g, ln_b, ts=256):
    b, s, _ = u3.shape
    vec = pl.BlockSpec((1, D_MODEL), lambda bi, i: (0, 0))
    return pl.pallas_call(
        _conv_kernel,
        grid=(b, s // ts),
        in_specs=[
            pl.BlockSpec((1, ts, D_MODEL), lambda bi, i: (bi, i, 0)),
            pl.BlockSpec((1, CONV_HALO, D_MODEL),
                         lambda bi, i: (bi, jnp.maximum(i * (ts // CONV_HALO) - 1, 0), 0)),
            pl.BlockSpec((CONV_WIDTH, D_MODEL), lambda bi, i: (0, 0)),
            vec, vec, vec,
        ],
        out_specs=pl.BlockSpec((1, ts, D_MODEL), lambda bi, i: (bi, i, 0)),
        out_shape=jax.ShapeDtypeStruct((b, s, D_MODEL), BF16),
        scratch_shapes=[pltpu.VMEM((ts + CONV_HALO, D_MODEL), F32),
                        pltpu.VMEM((ts, D_MODEL), F32)],
        compiler_params=_params("parallel", "parallel"),
        name="conv_ln_silu",
    )(u3, u3, dw_w, dw_b, ln_g, ln_b)


def _proj_mlp_kernel(*refs, n_proj, final_norm):
    x_ref = refs[0]
    y_refs = refs[1:1 + n_proj]
    wp_refs = refs[1 + n_proj:1 + 2 * n_proj]
    pb_ref, g_ref, w1_ref, w2_ref, gf_ref, o_ref, x1_ref, h_ref, acc_ref = refs[1 + 2 * n_proj:]
    j = pl.program_id(1)

    @pl.when(j == 0)
    def _():
        x1 = x_ref[...] + pb_ref[...]
        for y_ref, wp_ref in zip(y_refs, wp_refs):
            x1 = x1 + jnp.dot(y_ref[...], wp_ref[...], preferred_element_type=F32)
        x1_ref[...] = x1
        h_ref[...] = _rms(x1, g_ref[...]).astype(BF16)
        acc_ref[...] = jnp.zeros_like(acc_ref)

    hid = jnp.dot(h_ref[...], w1_ref[...], preferred_element_type=F32)
    hid = jnp.square(jnp.maximum(hid, 0.0)).astype(BF16)
    acc_ref[...] += jnp.dot(hid, w2_ref[...], preferred_element_type=F32)

    @pl.when(j == pl.num_programs(1) - 1)
    def _():
        out = x1_ref[...] + acc_ref[...]
        if final_norm:
            out = _rms(out, gf_ref[...])
        o_ref[...] = out


def _proj_mlp(x2, ys, wps, pb, g, w1, w2, gf, final_norm, tm=512, tf=1024):
    t = x2.shape[0]
    n_proj = len(ys)
    vec = pl.BlockSpec((1, D_MODEL), lambda i, j: (0, 0))
    in_specs = [pl.BlockSpec((tm, D_MODEL), lambda i, j: (i, 0))]
    in_specs += [pl.BlockSpec((tm, y.shape[1]), lambda i, j: (i, 0)) for y in ys]
    in_specs += [pl.BlockSpec(w.shape, lambda i, j: (0, 0)) for w in wps]
    in_specs += [vec, vec,
                 pl.BlockSpec((D_MODEL, tf), lambda i, j: (0, j)),
                 pl.BlockSpec((tf, D_MODEL), lambda i, j: (j, 0)),
                 vec]
    return pl.pallas_call(
        functools.partial(_proj_mlp_kernel, n_proj=n_proj, final_norm=final_norm),
        grid=(t // tm, D_FF // tf),
        in_specs=in_specs,
        out_specs=pl.BlockSpec((tm, D_MODEL), lambda i, j: (i, 0)),
        out_shape=jax.ShapeDtypeStruct((t, D_MODEL), F32),
        scratch_shapes=[pltpu.VMEM((tm, D_MODEL), F32),
                        pltpu.VMEM((tm, D_MODEL), BF16),
                        pltpu.VMEM((tm, D_MODEL), F32)],
        compiler_params=_params("parallel", "arbitrary"),
        name="proj_mlp",
    )(x2, *ys, *wps, pb, g, w1, w2, gf)


def _row(v):
    return v.reshape(1, -1).astype(F32)


def _rwkv_time_mix(p3, mu, w0, w_up, a0, a_up, g_up, k_k, k_a, r_k, gn_g, gn_b):
    w_lora = jnp.zeros((LORA_W + LORA_A + LORA_G, 3 * RWKV_DIM), F32)
    w_lora = w_lora.at[:LORA_W, :RWKV_DIM].set(w_up)
    w_lora = w_lora.at[LORA_W:LORA_W + LORA_A, RWKV_DIM:2 * RWKV_DIM].set(a_up)
    w_lora = w_lora.at[LORA_W + LORA_A:, 2 * RWKV_DIM:].set(g_up)
    bias = jnp.concatenate([w0, a0, jnp.zeros((RWKV_DIM,), F32)]).reshape(1, -1)
    r, k, v, lw, a, g = _rwkv_prep(p3, _row(mu), w_lora.astype(BF16), bias)
    return _rwkv_scan(r, k, v, lw, a, g, _row(k_k), _row(k_a), _row(r_k), _row(gn_g), _row(gn_b))


def kernel(x, norm_mix_g, norm_ffn_g, final_norm_g, hy_w_in, hy_mu, hy_w0, hy_w_up, hy_a0, hy_a_up, hy_g_up, hy_k_k, hy_k_a, hy_r_k, hy_gn_g, hy_gn_b, hy_sinks, hy_w_out, cv_pw1_w, cv_pw1_b, cv_dw_w, cv_dw_b, cv_ln_g, cv_ln_b, cv_pw2_w, cv_pw2_b, mlp_w1, mlp_w2):
    bsz, seq, d = x.shape
    depth = norm_mix_g.shape[0]
    t = bsz * seq
    x2 = x.reshape(t, d)
    zero_row = jnp.zeros((1, D_MODEL), F32)
    gf = _row(final_norm_g)

    for layer in range(depth):
        i = layer // 2
        g_mix = _row(norm_mix_g[layer])
        if layer % 2 == 0:
            p_rwkv, p_att = _norm_proj(x2, g_mix, hy_w_in[i].astype(BF16))
            y_rwkv = _rwkv_time_mix(p_rwkv.reshape(bsz, seq, SHIFT_DIM), hy_mu[i], hy_w0[i],
                                    hy_w_up[i], hy_a0[i], hy_a_up[i], hy_g_up[i], hy_k_k[i],
                                    hy_k_a[i], hy_r_k[i], hy_gn_g[i], hy_gn_b[i])
            y_att = _swa(p_att.reshape(bsz, seq, ATT_COLS), hy_sinks[i].astype(F32))
            w_out = hy_w_out[i].astype(BF16)
            ys = [y_rwkv.reshape(t, RWKV_DIM), y_att.reshape(t, ATT_DIM)]
            wps = [w_out[:RWKV_DIM], w_out[RWKV_DIM:]]
            pb = zero_row
        else:
            u = _norm_glu(x2, g_mix, cv_pw1_w[i].astype(BF16), _row(cv_pw1_b[i]))
            u = _conv_ln_silu(u.reshape(bsz, seq, D_MODEL), cv_dw_w[i].astype(F32),
                              _row(cv_dw_b[i]), _row(cv_ln_g[i]), _row(cv_ln_b[i]))
            ys = [u.reshape(t, D_MODEL)]
            wps = [cv_pw2_w[i].astype(BF16)]
            pb = _row(cv_pw2_b[i])
        x2 = _proj_mlp(x2, ys, wps, pb, _row(norm_ffn_g[layer]),
                       mlp_w1[layer].astype(BF16), mlp_w2[layer].astype(BF16), gf,
                       final_norm=(layer == depth - 1))
    return x2.reshape(bsz, seq, d)
```

```python
import functools

import jax
import jax.numpy as jnp
from jax import lax
from jax.experimental import pallas as pl
from jax.experimental.pallas import tpu as pltpu

D_MODEL = 1024
HEAD_DIM = 64
RWKV_DIM = 512
ATT_Q_HEADS = 8
ATT_DIM = 512
KV_DIM = 128
LORA_W = 64
LORA_A = 64
LORA_G = 128
SHIFT_DIM = 3 * RWKV_DIM + LORA_W + LORA_A + LORA_G
ATT_COLS = ATT_DIM + 2 * KV_DIM
WINDOW = 128
CONV_WIDTH = 31
CONV_HALO = 32
D_FF = 4 * D_MODEL
RMS_EPS = 1e-6
LN_EPS = 1e-5
GN_EPS = 64e-5

LANES = 128
SUBLANES = 8
CHUNK = 64
VMEM_LIMIT = 56 * 1024 * 1024

F32 = jnp.float32
BF16 = jnp.bfloat16
NN = (((1,), (0,)), ((), ()))
NT = (((1,), (1,)), ((), ()))
TN = (((0,), (0,)), ((), ()))


def _params(*sem):
    return pltpu.CompilerParams(dimension_semantics=sem, vmem_limit_bytes=VMEM_LIMIT)


def _rms(x, g):
    return x * lax.rsqrt(jnp.mean(x * x, axis=-1, keepdims=True) + RMS_EPS) * g


def _norm_proj_kernel(x_ref, g_ref, w_ref, o1_ref, o2_ref):
    h = _rms(x_ref[...], g_ref[...]).astype(BF16)
    acc = jnp.dot(h, w_ref[...], preferred_element_type=F32)
    o1_ref[...] = acc[:, :SHIFT_DIM]
    o2_ref[...] = acc[:, SHIFT_DIM:]


def _norm_proj(x2, g, w, tm=512):
    t = x2.shape[0]
    n = w.shape[1]
    return pl.pallas_call(
        _norm_proj_kernel,
        grid=(t // tm,),
        in_specs=[
            pl.BlockSpec((tm, D_MODEL), lambda i: (i, 0)),
            pl.BlockSpec((1, D_MODEL), lambda i: (0, 0)),
            pl.BlockSpec((D_MODEL, n), lambda i: (0, 0)),
        ],
        out_specs=[
            pl.BlockSpec((tm, SHIFT_DIM), lambda i: (i, 0)),
            pl.BlockSpec((tm, ATT_COLS), lambda i: (i, 0)),
        ],
        out_shape=[
            jax.ShapeDtypeStruct((t, SHIFT_DIM), F32),
            jax.ShapeDtypeStruct((t, ATT_COLS), F32),
        ],
        compiler_params=_params("parallel"),
        name="norm_proj",
    )(x2, g, w)


def _norm_glu_kernel(x_ref, g_ref, w_ref, b_ref, o_ref):
    h = _rms(x_ref[...], g_ref[...]).astype(BF16)
    acc = jnp.dot(h, w_ref[...], preferred_element_type=F32) + b_ref[...]
    o_ref[...] = acc[:, :D_MODEL] * jax.nn.sigmoid(acc[:, D_MODEL:])


def _norm_glu(x2, g, w, b, tm=512):
    t = x2.shape[0]
    return pl.pallas_call(
        _norm_glu_kernel,
        grid=(t // tm,),
        in_specs=[
            pl.BlockSpec((tm, D_MODEL), lambda i: (i, 0)),
            pl.BlockSpec((1, D_MODEL), lambda i: (0, 0)),
            pl.BlockSpec((D_MODEL, 2 * D_MODEL), lambda i: (0, 0)),
            pl.BlockSpec((1, 2 * D_MODEL), lambda i: (0, 0)),
        ],
        out_specs=pl.BlockSpec((tm, D_MODEL), lambda i: (i, 0)),
        out_shape=jax.ShapeDtypeStruct((t, D_MODEL), F32),
        compiler_params=_params("parallel"),
        name="norm_glu",
    )(x2, g, w, b)


def _rwkv_prep_kernel(p_ref, halo_ref, mu_ref, wl_ref, bias_ref,
                      r_ref, k_ref, v_ref, lw_ref, a_ref, g_ref):
    i = pl.program_id(1)
    x = p_ref[0]
    prev = jnp.where(i > 0, halo_ref[0, 7:8, :], 0.0)
    row = lax.broadcasted_iota(jnp.int32, x.shape, 0)
    shifted = jnp.where(row == 0, prev, pltpu.roll(x, 1, axis=0))
    xl = x + (shifted - x) * mu_ref[...]
    c1, c2, c3 = RWKV_DIM, 2 * RWKV_DIM, 3 * RWKV_DIM
    r_ref[0] = xl[:, :c1]
    k_ref[0] = xl[:, c1:c2]
    v_ref[0] = xl[:, c2:c3]
    z = xl[:, c3:]
    lane = lax.broadcasted_iota(jnp.int32, z.shape, 1)
    act = jnp.where(lane < LORA_W, jnp.tanh(z),
                    jnp.where(lane < LORA_W + LORA_A, z, jax.nn.sigmoid(z)))
    lo = jnp.dot(act.astype(BF16), wl_ref[...], preferred_element_type=F32) + bias_ref[...]
    wpre = lo[:, :c1]
    softplus = jnp.maximum(-wpre, 0.0) + jnp.log(1.0 + jnp.exp(-jnp.abs(wpre)))
    lw_ref[0] = -jnp.exp(-softplus - 0.5)
    a_ref[0] = jax.nn.sigmoid(lo[:, c1:c2])
    g_ref[0] = lo[:, c2:]


def _rwkv_prep(p3, mu, w_lora, bias, ts=512):
    b, s, _ = p3.shape
    out = jax.ShapeDtypeStruct((b, s, RWKV_DIM), F32)
    ospec = pl.BlockSpec((1, ts, RWKV_DIM), lambda bi, i: (bi, i, 0))
    return pl.pallas_call(
        _rwkv_prep_kernel,
        grid=(b, s // ts),
        in_specs=[
            pl.BlockSpec((1, ts, SHIFT_DIM), lambda bi, i: (bi, i, 0)),
            pl.BlockSpec((1, 8, SHIFT_DIM),
                         lambda bi, i: (bi, jnp.maximum(i * (ts // 8) - 1, 0), 0)),
            pl.BlockSpec((1, SHIFT_DIM), lambda bi, i: (0, 0)),
            pl.BlockSpec(w_lora.shape, lambda bi, i: (0, 0)),
            pl.BlockSpec((1, 3 * RWKV_DIM), lambda bi, i: (0, 0)),
        ],
        out_specs=[ospec] * 6,
        out_shape=[out] * 6,
        compiler_params=_params("parallel", "parallel"),
        name="rwkv_prep",
    )(p3, p3, mu, w_lora, bias)


def _split3(x):
    hi = x.astype(BF16)
    r1 = x - hi.astype(F32)
    mid = r1.astype(BF16)
    lo = (r1 - mid.astype(F32)).astype(BF16)
    return hi, mid, lo


def _rwkv_scan_kernel(r_ref, k_ref, v_ref, lw_ref, a_ref, g_ref,
                      kk_ref, ka_ref, rk_ref, gg_ref, gb_ref, o_ref, state_ref):
    c_len = CHUNK
    n2 = 2 * c_len
    nb, ts = r_ref.shape[0], r_ref.shape[1]
    n_pairs = RWKV_DIM // LANES
    lane = lax.broadcasted_iota(jnp.int32, (c_len, LANES), 1)
    lo = lane < HEAD_DIM
    lo2 = lax.broadcasted_iota(jnp.int32, (n2, LANES), 1) < HEAD_DIM
    top = lax.broadcasted_iota(jnp.int32, (n2, LANES), 0) < c_len
    own = lo2 == top

    def stack(x):
        return jnp.concatenate([jnp.where(lo, x, 0.0), jnp.where(lo, 0.0, x)], axis=0)

    ri = lax.broadcasted_iota(jnp.int32, (n2, n2), 0)
    ci = lax.broadcasted_iota(jnp.int32, (n2, n2), 1)
    strict = ri > ci
    incl = ri >= ci
    eye = (ri == ci).astype(F32)
    blk = [(ri >> l) == (ci >> l) for l in range(7)]
    tri = (lax.broadcasted_iota(jnp.int32, (c_len, c_len), 0)
           >= lax.broadcasted_iota(jnp.int32, (c_len, c_len), 1)).astype(BF16)

    @pl.when(pl.program_id(1) == 0)
    def _():
        state_ref[...] = jnp.zeros_like(state_ref)

    units = [(bb, j) for bb in range(nb) for j in range(n_pairs)]
    pairs = range(len(units))

    def mm(a, b, dims=NN):
        return lax.dot_general(a.astype(BF16), b.astype(BF16), dims, preferred_element_type=F32)

    def body(c, carry):
        sl = pl.ds(pl.multiple_of(c * c_len, c_len), c_len)
        lw_all, cum_all = [], []
        for bb in range(nb):
            lw_b = lw_ref[bb, sl, :]
            cum3 = jnp.dot(tri, jnp.concatenate(_split3(lw_b), axis=1), preferred_element_type=F32)
            lw_all.append(lw_b)
            cum_all.append(cum3[:, :RWKV_DIM] + cum3[:, RWKV_DIM:2 * RWKV_DIM] + cum3[:, 2 * RWKV_DIM:])

        at_rt, bt_kt, bh_kh, v_st, decay, bonus = [], [], [], [], [], []
        for bb, j in units:
            cs = slice(j * LANES, (j + 1) * LANES)
            cum, lw = cum_all[bb][:, cs], lw_all[bb][:, cs]
            cum_last = cum[c_len - 1:c_len, :]
            r_s, k_s = stack(r_ref[bb, sl, cs]), stack(k_ref[bb, sl, cs])
            v_s, a_s = stack(v_ref[bb, sl, cs]), stack(a_ref[bb, sl, cs])
            cum_s = jnp.concatenate([cum, cum], axis=0)
            lw_s = jnp.concatenate([lw, lw], axis=0)
            kk = k_s * kk_ref[:, cs]
            kk = kk / jnp.maximum(jnp.sqrt(jnp.sum(kk * kk, axis=-1, keepdims=True)), 1e-12)
            kmod = k_s * (1.0 + (a_s - 1.0) * ka_ref[:, cs])
            bvec = kk * a_s
            einv = jnp.exp(-cum_s)
            edec = jnp.exp(cum_last - cum_s)
            at_rt.append(jnp.concatenate([-kk * jnp.exp(cum_s - lw_s), r_s * jnp.exp(cum_s)],
                                         axis=0).astype(BF16))
            bt_kt.append(jnp.concatenate([bvec * einv, kmod * einv], axis=0).astype(BF16))
            bh_kh.append(jnp.concatenate([bvec * edec, kmod * edec], axis=0).astype(BF16))
            v_st.append(v_s)
            decay.append(jnp.exp(cum_last))
            bonus.append(jnp.sum(r_s * kmod * rk_ref[:, cs], axis=-1, keepdims=True) * v_s)

        gram = [mm(at_rt[j], bt_kt[j], NT) for j in pairs]
        a_ab = [jnp.where(strict, gram[j][:n2, :n2], 0.0) for j in pairs]
        a_ak = [jnp.where(strict, gram[j][:n2, n2:], 0.0).astype(BF16) for j in pairs]
        r_b = [jnp.where(incl, gram[j][n2:, :n2], 0.0).astype(BF16) for j in pairs]
        r_k = [jnp.where(incl, gram[j][n2:, n2:], 0.0).astype(BF16) for j in pairs]
        v_bf = [v_st[j].astype(BF16) for j in pairs]

        tinv = [eye + jnp.where(blk[1], a_ab[j], 0.0) for j in pairs]
        for lvl in range(2, 7):
            off = blk[lvl] & ~blk[lvl - 1]
            t_bf = [tinv[j].astype(BF16) for j in pairs]
            w = [mm(jnp.where(off, a_ab[j], 0.0), t_bf[j]) for j in pairs]
            tinv = [tinv[j] + mm(t_bf[j], w[j]) for j in pairs]

        akv = [mm(a_ak[j], v_bf[j]) for j in pairs]
        state = [state_ref[j] for j in pairs]
        s_t = [mm(at_rt[j], state[j], NT) for j in pairs]
        u = [mm(tinv[j], s_t[j][:n2] + akv[j]) for j in pairs]
        u_bf = [u[j].astype(BF16) for j in pairs]
        for j in pairs:
            state_ref[j] = state[j] * decay[j] + mm(
                jnp.concatenate([u_bf[j], v_bf[j]], axis=0), bh_kh[j], TN)
        y = [s_t[j][n2:] + mm(r_b[j], u_bf[j]) + mm(r_k[j], v_bf[j]) for j in pairs]

        inv_n = 1.0 / HEAD_DIM
        for idx, (bb, j) in enumerate(units):
            cs = slice(j * LANES, (j + 1) * LANES)
            mean = jnp.sum(y[idx], axis=-1, keepdims=True) * inv_n
            yc = jnp.where(own, y[idx] - mean, 0.0)
            var = jnp.sum(yc * yc, axis=-1, keepdims=True) * inv_n
            out = yc * lax.rsqrt(var + GN_EPS) * gg_ref[:, cs] + gb_ref[:, cs] + bonus[idx]
            out = jnp.where(lo, out[:c_len], out[c_len:])
            o_ref[bb, sl, cs] = (out * g_ref[bb, sl, cs]).astype(o_ref.dtype)
        return carry

    lax.fori_loop(0, ts // c_len, body, 0)


def _rwkv_scan(r, k, v, lw, a, g, kkp, kap, rkp, ggp, gbp, ts=512, nb=2):
    b, s, _ = r.shape
    xspec = pl.BlockSpec((nb, ts, RWKV_DIM), lambda bi, i: (bi, i, 0))
    pspec = pl.BlockSpec((1, RWKV_DIM), lambda bi, i: (0, 0))
    return pl.pallas_call(
        _rwkv_scan_kernel,
        grid=(b // nb, s // ts),
        in_specs=[xspec] * 6 + [pspec] * 5,
        out_specs=xspec,
        out_shape=jax.ShapeDtypeStruct((b, s, RWKV_DIM), BF16),
        scratch_shapes=[pltpu.VMEM((nb * RWKV_DIM // LANES, LANES, LANES), F32)],
        compiler_params=_params("parallel", "arbitrary"),
        name="rwkv_scan",
    )(r, k, v, lw, a, g, kkp, kap, rkp, ggp, gbp)


def _swa_kernel(sink_ref, q_ref, kvc_ref, kvp_ref, o_ref):
    n = pl.program_id(1)
    blk = WINDOW
    kv = jnp.concatenate([kvp_ref[0], kvc_ref[0]], axis=0)
    kmat, vmat = kv[:, :KV_DIM], kv[:, KV_DIM:]
    lo_kv = lax.broadcasted_iota(jnp.int32, (2 * blk, LANES), 1) < HEAD_DIM
    k_sw = pltpu.roll(kmat, HEAD_DIM, axis=1)
    v_sw = pltpu.roll(vmat, HEAD_DIM, axis=1)
    k_dup = [jnp.where(lo_kv, kmat, k_sw).astype(BF16), jnp.where(lo_kv, k_sw, kmat).astype(BF16)]
    v_dup = [jnp.where(lo_kv, vmat, v_sw).astype(BF16), jnp.where(lo_kv, v_sw, vmat).astype(BF16)]

    lo_q = lax.broadcasted_iota(jnp.int32, (blk, LANES), 1) < HEAD_DIM
    rows = lax.broadcasted_iota(jnp.int32, (2 * blk, 2 * blk), 0)
    cols = lax.broadcasted_iota(jnp.int32, (2 * blk, 2 * blk), 1)
    rel = (rows & (blk - 1)) + blk - cols
    valid = (rel >= 0) & (rel < WINDOW) & ((cols >= blk) | (n > 0))
    relf = rel.astype(F32)
    first = lax.broadcasted_iota(jnp.int32, (2 * blk, 1), 0) < blk

    pairs = range(ATT_Q_HEADS // 2)
    scores = []
    for j in pairs:
        qp = q_ref[0, :, j * LANES:(j + 1) * LANES] * (HEAD_DIM ** -0.5)
        qs = jnp.concatenate([jnp.where(lo_q, qp, 0.0), jnp.where(lo_q, 0.0, qp)], axis=0)
        scores.append(lax.dot_general(qs.astype(BF16), k_dup[j // 2], NT, preferred_element_type=F32))
    probs = []
    for j in pairs:
        slope = jnp.where(first, 2.0 ** -(2 * j + 1), 2.0 ** -(2 * j + 2))
        sink = jnp.where(first, sink_ref[2 * j], sink_ref[2 * j + 1])
        s = jnp.where(valid, scores[j] - slope * relf, -jnp.inf)
        m = jnp.maximum(jnp.max(s, axis=-1, keepdims=True), sink)
        e = jnp.exp(s - m)
        p = e / (jnp.sum(e, axis=-1, keepdims=True) + jnp.exp(sink - m))
        probs.append(p.astype(BF16))
    for j in pairs:
        o = jnp.dot(probs[j], v_dup[j // 2], preferred_element_type=F32)
        o_ref[0, :, j * LANES:(j + 1) * LANES] = jnp.where(lo_q, o[:blk], o[blk:]).astype(o_ref.dtype)


def _swa(p_att3, sinks):
    b, s, _ = p_att3.shape
    blk = WINDOW
    return pl.pallas_call(
        _swa_kernel,
        grid=(b, s // blk),
        in_specs=[
            pl.BlockSpec(memory_space=pltpu.SMEM),
            pl.BlockSpec((1, blk, ATT_DIM), lambda bi, n: (bi, n, 0)),
            pl.BlockSpec((1, blk, 2 * KV_DIM), lambda bi, n: (bi, n, ATT_DIM // (2 * KV_DIM))),
            pl.BlockSpec((1, blk, 2 * KV_DIM),
                         lambda bi, n: (bi, jnp.maximum(n - 1, 0), ATT_DIM // (2 * KV_DIM))),
        ],
        out_specs=pl.BlockSpec((1, blk, ATT_DIM), lambda bi, n: (bi, n, 0)),
        out_shape=jax.ShapeDtypeStruct((b, s, ATT_DIM), BF16),
        compiler_params=_params("parallel", "parallel"),
        name="swa_attention",
    )(sinks, p_att3, p_att3, p_att3)


def _conv_kernel(u_ref, halo_ref, w_ref, b_ref, lg_ref, lb_ref, o_ref, sh_ref, acc_ref):
    i = pl.program_id(1)
    ts = u_ref.shape[1]
    rb = 64
    sh_ref[0, 0:CONV_HALO, :] = jnp.where(i > 0, halo_ref[0], 0.0)
    sh_ref[0, CONV_HALO:, :] = u_ref[0]
    n_sh = ts + CONV_HALO - SUBLANES
    for q in range(1, SUBLANES):
        for cb in range(D_MODEL // LANES):
            cs = slice(cb * LANES, (cb + 1) * LANES)
            sh_ref[q, 0:n_sh, cs] = sh_ref[0, q:q + n_sh, cs]
    off = CONV_HALO - (CONV_WIDTH - 1)
    for cb in range(D_MODEL // LANES):
        cs = slice(cb * LANES, (cb + 1) * LANES)

        def rows(rblk, carry, cs=cs):
            r0 = pl.multiple_of(rblk * rb, rb)
            acc = jnp.zeros((rb, LANES), F32) + b_ref[:, cs]
            for q in range(SUBLANES):
                taps = [j for j in range(CONV_WIDTH) if (off + j) % SUBLANES == q]
                a_max = (off + taps[-1]) // SUBLANES
                x = sh_ref[q, pl.ds(r0, rb + a_max * SUBLANES), cs]
                for j in taps:
                    a = (off + j) // SUBLANES
                    acc = acc + w_ref[j:j + 1, cs] * x[a * SUBLANES:a * SUBLANES + rb]
            acc_ref[pl.ds(r0, rb), cs] = acc
            return carry

        lax.fori_loop(0, ts // rb, rows, 0)
    y = acc_ref[...]
    mean = jnp.mean(y, axis=-1, keepdims=True)
    yc = y - mean
    var = jnp.mean(yc * yc, axis=-1, keepdims=True)
    yn = yc * lax.rsqrt(var + LN_EPS) * lg_ref[...] + lb_ref[...]
    o_ref[0] = (yn * jax.nn.sigmoid(yn)).astype(o_ref.dtype)


def _conv_ln_silu(u3, dw_w, dw_b, ln_g, ln_b, ts=512):
    b, s, _ = u3.shape
    vec = pl.BlockSpec((1, D_MODEL), lambda bi, i: (0, 0))
    return pl.pallas_call(
        _conv_kernel,
        grid=(b, s // ts),
        in_specs=[
            pl.BlockSpec((1, ts, D_MODEL), lambda bi, i: (bi, i, 0)),
            pl.BlockSpec((1, CONV_HALO, D_MODEL),
                         lambda bi, i: (bi, jnp.maximum(i * (ts // CONV_HALO) - 1, 0), 0)),
            pl.BlockSpec((CONV_WIDTH, D_MODEL), lambda bi, i: (0, 0)),
            vec, vec, vec,
        ],
        out_specs=pl.BlockSpec((1, ts, D_MODEL), lambda bi, i: (bi, i, 0)),
        out_shape=jax.ShapeDtypeStruct((b, s, D_MODEL), BF16),
        scratch_shapes=[pltpu.VMEM((SUBLANES, ts + CONV_HALO, D_MODEL), F32),
                        pltpu.VMEM((ts, D_MODEL), F32)],
        compiler_params=_params("parallel", "parallel"),
        name="conv_ln_silu",
    )(u3, u3, dw_w, dw_b, ln_g, ln_b)


def _proj_mlp_kernel(*refs, n_proj, final_norm):
    x_ref = refs[0]
    y_refs = refs[1:1 + n_proj]
    wp_refs = refs[1 + n_proj:1 + 2 * n_proj]
    pb_ref, g_ref, w1_ref, w2_ref, gf_ref, o_ref, x1_ref, h_ref, acc_ref = refs[1 + 2 * n_proj:]
    j = pl.program_id(1)

    @pl.when(j == 0)
    def _():
        x1 = x_ref[...] + pb_ref[...]
        for y_ref, wp_ref in zip(y_refs, wp_refs):
            x1 = x1 + jnp.dot(y_ref[...], wp_ref[...], preferred_element_type=F32)
        x1_ref[...] = x1
        h_ref[...] = _rms(x1, g_ref[...]).astype(BF16)
        acc_ref[...] = jnp.zeros_like(acc_ref)

    hid = jnp.dot(h_ref[...], w1_ref[...], preferred_element_type=F32)
    hid = jnp.square(jnp.maximum(hid, 0.0)).astype(BF16)
    acc_ref[...] += jnp.dot(hid, w2_ref[...], preferred_element_type=F32)

    @pl.when(j == pl.num_programs(1) - 1)
    def _():
        out = x1_ref[...] + acc_ref[...]
        if final_norm:
            out = _rms(out, gf_ref[...])
        o_ref[...] = out


def _proj_mlp(x2, ys, wps, pb, g, w1, w2, gf, final_norm, tm=1024, tf=1024):
    t = x2.shape[0]
    n_proj = len(ys)
    vec = pl.BlockSpec((1, D_MODEL), lambda i, j: (0, 0))
    in_specs = [pl.BlockSpec((tm, D_MODEL), lambda i, j: (i, 0))]
    in_specs += [pl.BlockSpec((tm, y.shape[1]), lambda i, j: (i, 0)) for y in ys]
    in_specs += [pl.BlockSpec(w.shape, lambda i, j: (0, 0)) for w in wps]
    in_specs += [vec, vec,
                 pl.BlockSpec((D_MODEL, tf), lambda i, j: (0, j)),
                 pl.BlockSpec((tf, D_MODEL), lambda i, j: (j, 0)),
                 vec]
    return pl.pallas_call(
        functools.partial(_proj_mlp_kernel, n_proj=n_proj, final_norm=final_norm),
        grid=(t // tm, D_FF // tf),
        in_specs=in_specs,
        out_specs=pl.BlockSpec((tm, D_MODEL), lambda i, j: (i, 0)),
        out_shape=jax.ShapeDtypeStruct((t, D_MODEL), F32),
        scratch_shapes=[pltpu.VMEM((tm, D_MODEL), F32),
                        pltpu.VMEM((tm, D_MODEL), BF16),
                        pltpu.VMEM((tm, D_MODEL), F32)],
        compiler_params=_params("parallel", "arbitrary"),
        name="proj_mlp",
    )(x2, *ys, *wps, pb, g, w1, w2, gf)


def _row(v):
    return v.reshape(1, -1).astype(F32)


def _rwkv_time_mix(p3, mu, w0, w_up, a0, a_up, g_up, k_k, k_a, r_k, gn_g, gn_b):
    w_lora = jnp.zeros((LORA_W + LORA_A + LORA_G, 3 * RWKV_DIM), F32)
    w_lora = w_lora.at[:LORA_W, :RWKV_DIM].set(w_up)
    w_lora = w_lora.at[LORA_W:LORA_W + LORA_A, RWKV_DIM:2 * RWKV_DIM].set(a_up)
    w_lora = w_lora.at[LORA_W + LORA_A:, 2 * RWKV_DIM:].set(g_up)
    bias = jnp.concatenate([w0, a0, jnp.zeros((RWKV_DIM,), F32)]).reshape(1, -1)
    r, k, v, lw, a, g = _rwkv_prep(p3, _row(mu), w_lora.astype(BF16), bias)
    return _rwkv_scan(r, k, v, lw, a, g, _row(k_k), _row(k_a), _row(r_k), _row(gn_g), _row(gn_b))


def kernel(x, norm_mix_g, norm_ffn_g, final_norm_g, hy_w_in, hy_mu, hy_w0, hy_w_up, hy_a0, hy_a_up, hy_g_up, hy_k_k, hy_k_a, hy_r_k, hy_gn_g, hy_gn_b, hy_sinks, hy_w_out, cv_pw1_w, cv_pw1_b, cv_dw_w, cv_dw_b, cv_ln_g, cv_ln_b, cv_pw2_w, cv_pw2_b, mlp_w1, mlp_w2):
    bsz, seq, d = x.shape
    depth = norm_mix_g.shape[0]
    t = bsz * seq
    x2 = x.reshape(t, d)
    zero_row = jnp.zeros((1, D_MODEL), F32)
    gf = _row(final_norm_g)

    for layer in range(depth):
        i = layer // 2
        g_mix = _row(norm_mix_g[layer])
        if layer % 2 == 0:
            p_rwkv, p_att = _norm_proj(x2, g_mix, hy_w_in[i].astype(BF16))
            y_rwkv = _rwkv_time_mix(p_rwkv.reshape(bsz, seq, SHIFT_DIM), hy_mu[i], hy_w0[i],
                                    hy_w_up[i], hy_a0[i], hy_a_up[i], hy_g_up[i], hy_k_k[i],
                                    hy_k_a[i], hy_r_k[i], hy_gn_g[i], hy_gn_b[i])
            y_att = _swa(p_att.reshape(bsz, seq, ATT_COLS), hy_sinks[i].astype(F32))
            w_out = hy_w_out[i].astype(BF16)
            ys = [y_rwkv.reshape(t, RWKV_DIM), y_att.reshape(t, ATT_DIM)]
            wps = [w_out[:RWKV_DIM], w_out[RWKV_DIM:]]
            pb = zero_row
        else:
            u = _norm_glu(x2, g_mix, cv_pw1_w[i].astype(BF16), _row(cv_pw1_b[i]))
            u = _conv_ln_silu(u.reshape(bsz, seq, D_MODEL), cv_dw_w[i].astype(F32),
                              _row(cv_dw_b[i]), _row(cv_ln_g[i]), _row(cv_ln_b[i]))
            ys = [u.reshape(t, D_MODEL)]
            wps = [cv_pw2_w[i].astype(BF16)]
            pb = _row(cv_pw2_b[i])
        x2 = _proj_mlp(x2, ys, wps, pb, _row(norm_ffn_g[layer]),
                       mlp_w1[layer].astype(BF16), mlp_w2[layer].astype(BF16), gf,
                       final_norm=(layer == depth - 1))
    return x2.reshape(bsz, seq, d)
```

```python
import functools

import jax
import jax.numpy as jnp
from jax import lax
from jax.experimental import pallas as pl
from jax.experimental.pallas import tpu as pltpu

D_MODEL = 1024
HEAD_DIM = 64
RWKV_DIM = 512
ATT_Q_HEADS = 8
ATT_DIM = 512
KV_DIM = 128
LORA_W = 64
LORA_A = 64
LORA_G = 128
SHIFT_DIM = 3 * RWKV_DIM + LORA_W + LORA_A + LORA_G
ATT_COLS = ATT_DIM + 2 * KV_DIM
WINDOW = 128
CONV_WIDTH = 31
CONV_HALO = 32
D_FF = 4 * D_MODEL
RMS_EPS = 1e-6
LN_EPS = 1e-5
GN_EPS = 64e-5

LANES = 128
SUBLANES = 8
CHUNK = 64
VMEM_LIMIT = 56 * 1024 * 1024

F32 = jnp.float32
BF16 = jnp.bfloat16
NN = (((1,), (0,)), ((), ()))
NT = (((1,), (1,)), ((), ()))
TN = (((0,), (0,)), ((), ()))


def _params(*sem):
    return pltpu.CompilerParams(dimension_semantics=sem, vmem_limit_bytes=VMEM_LIMIT)


def _rms(x, g):
    return x * lax.rsqrt(jnp.mean(x * x, axis=-1, keepdims=True) + RMS_EPS) * g


LORA_DIM = LORA_W + LORA_A + LORA_G


def _in_proj_kernel(x_ref, g_ref, w_ref, mu_ref, wl_ref, bias_ref,
                    r_ref, k_ref, v_ref, lw_ref, a_ref, gate_ref, att_ref, prev_ref):
    tm = x_ref.shape[1]
    c1, c2, c3 = RWKV_DIM, 2 * RWKV_DIM, 3 * RWKV_DIM

    @pl.when(pl.program_id(1) == 0)
    def _():
        prev_ref[...] = jnp.zeros_like(prev_ref)

    def shift_lerp(p, cs):
        first = lax.broadcasted_iota(jnp.int32, p.shape, 0) == 0
        shifted = jnp.where(first, prev_ref[SUBLANES - 1:SUBLANES, cs], pltpu.roll(p, 1, axis=0))
        prev_ref[:, cs] = p[tm - SUBLANES:, :]
        return p + (shifted - p) * mu_ref[:, cs]

    h = _rms(x_ref[0], g_ref[...]).astype(BF16)
    z = jnp.dot(h, w_ref[:, :LORA_DIM], preferred_element_type=F32)
    z = shift_lerp(z, slice(0, LORA_DIM))
    lane = lax.broadcasted_iota(jnp.int32, z.shape, 1)
    act = jnp.where(lane < LORA_W, jnp.tanh(z),
                    jnp.where(lane < LORA_W + LORA_A, z, jax.nn.sigmoid(z)))
    lo = jnp.dot(act.astype(BF16), wl_ref[...], preferred_element_type=F32) + bias_ref[...]
    rkv = jnp.dot(h, w_ref[:, LORA_DIM:SHIFT_DIM], preferred_element_type=F32)
    att_ref[0] = jnp.dot(h, w_ref[:, SHIFT_DIM:], preferred_element_type=F32)
    rkv = shift_lerp(rkv, slice(LORA_DIM, SHIFT_DIM))
    r_ref[0] = rkv[:, :c1]
    k_ref[0] = rkv[:, c1:c2]
    v_ref[0] = rkv[:, c2:c3]
    wpre = lo[:, :c1]
    softplus = jnp.maximum(-wpre, 0.0) + jnp.log(1.0 + jnp.exp(-jnp.abs(wpre)))
    lw_ref[0] = -jnp.exp(-softplus - 0.5)
    a_ref[0] = jax.nn.sigmoid(lo[:, c1:c2])
    gate_ref[0] = lo[:, c2:]


def _in_proj(x3, g, w, mu, w_lora, bias, tm=512):
    b, s, _ = x3.shape
    rspec = pl.BlockSpec((1, tm, RWKV_DIM), lambda bi, i: (bi, i, 0))
    rshape = jax.ShapeDtypeStruct((b, s, RWKV_DIM), F32)
    const = lambda shape: pl.BlockSpec(shape, lambda bi, i: (0, 0))
    return pl.pallas_call(
        _in_proj_kernel,
        grid=(b, s // tm),
        in_specs=[
            pl.BlockSpec((1, tm, D_MODEL), lambda bi, i: (bi, i, 0)),
            const((1, D_MODEL)), const(w.shape), const((1, SHIFT_DIM)),
            const(w_lora.shape), const((1, 3 * RWKV_DIM)),
        ],
        out_specs=[rspec] * 6 + [pl.BlockSpec((1, tm, ATT_COLS), lambda bi, i: (bi, i, 0))],
        out_shape=[rshape] * 6 + [jax.ShapeDtypeStruct((b, s, ATT_COLS), F32)],
        scratch_shapes=[pltpu.VMEM((SUBLANES, SHIFT_DIM), F32)],
        compiler_params=_params("parallel", "arbitrary"),
        name="in_proj",
    )(x3, g, w, mu, w_lora, bias)


def _norm_glu_kernel(x_ref, g_ref, w_ref, b_ref, o_ref):
    h = _rms(x_ref[...], g_ref[...]).astype(BF16)
    acc = jnp.dot(h, w_ref[...], preferred_element_type=F32) + b_ref[...]
    o_ref[...] = acc[:, :D_MODEL] * jax.nn.sigmoid(acc[:, D_MODEL:])


def _norm_glu(x2, g, w, b, tm=512):
    t = x2.shape[0]
    return pl.pallas_call(
        _norm_glu_kernel,
        grid=(t // tm,),
        in_specs=[
            pl.BlockSpec((tm, D_MODEL), lambda i: (i, 0)),
            pl.BlockSpec((1, D_MODEL), lambda i: (0, 0)),
            pl.BlockSpec((D_MODEL, 2 * D_MODEL), lambda i: (0, 0)),
            pl.BlockSpec((1, 2 * D_MODEL), lambda i: (0, 0)),
        ],
        out_specs=pl.BlockSpec((tm, D_MODEL), lambda i: (i, 0)),
        out_shape=jax.ShapeDtypeStruct((t, D_MODEL), F32),
        compiler_params=_params("parallel"),
        name="norm_glu",
    )(x2, g, w, b)


def _split3(x):
    hi = x.astype(BF16)
    r1 = x - hi.astype(F32)
    mid = r1.astype(BF16)
    lo = (r1 - mid.astype(F32)).astype(BF16)
    return hi, mid, lo


def _rwkv_scan_kernel(r_ref, k_ref, v_ref, lw_ref, a_ref, g_ref,
                      kk_ref, ka_ref, rk_ref, gg_ref, gb_ref, o_ref, state_ref):
    c_len = CHUNK
    n2 = 2 * c_len
    nb, ts = r_ref.shape[0], r_ref.shape[1]
    n_pairs = RWKV_DIM // LANES
    lane = lax.broadcasted_iota(jnp.int32, (c_len, LANES), 1)
    lo = lane < HEAD_DIM
    lo2 = lax.broadcasted_iota(jnp.int32, (n2, LANES), 1) < HEAD_DIM
    top = lax.broadcasted_iota(jnp.int32, (n2, LANES), 0) < c_len
    own = lo2 == top

    def stack(x):
        return jnp.concatenate([jnp.where(lo, x, 0.0), jnp.where(lo, 0.0, x)], axis=0)

    ri = lax.broadcasted_iota(jnp.int32, (n2, n2), 0)
    ci = lax.broadcasted_iota(jnp.int32, (n2, n2), 1)
    strict = ri > ci
    incl = ri >= ci
    eye = (ri == ci).astype(F32)
    blk = [(ri >> l) == (ci >> l) for l in range(7)]
    tri = (lax.broadcasted_iota(jnp.int32, (c_len, c_len), 0)
           >= lax.broadcasted_iota(jnp.int32, (c_len, c_len), 1)).astype(BF16)

    @pl.when(pl.program_id(1) == 0)
    def _():
        state_ref[...] = jnp.zeros_like(state_ref)

    units = [(bb, j) for bb in range(nb) for j in range(n_pairs)]
    pairs = range(len(units))

    def mm(a, b, dims=NN):
        return lax.dot_general(a.astype(BF16), b.astype(BF16), dims, preferred_element_type=F32)

    def body(c, carry):
        sl = pl.ds(pl.multiple_of(c * c_len, c_len), c_len)
        lw_all, cum_all = [], []
        for bb in range(nb):
            lw_b = lw_ref[bb, sl, :]
            cum3 = jnp.dot(tri, jnp.concatenate(_split3(lw_b), axis=1), preferred_element_type=F32)
            lw_all.append(lw_b)
            cum_all.append(cum3[:, :RWKV_DIM] + cum3[:, RWKV_DIM:2 * RWKV_DIM] + cum3[:, 2 * RWKV_DIM:])

        at_rt, bt_kt, bh_kh, v_st, decay, bonus = [], [], [], [], [], []
        for bb, j in units:
            cs = slice(j * LANES, (j + 1) * LANES)
            cum, lw = cum_all[bb][:, cs], lw_all[bb][:, cs]
            cum_last = cum[c_len - 1:c_len, :]
            r_s, k_s = stack(r_ref[bb, sl, cs]), stack(k_ref[bb, sl, cs])
            v_s, a_s = stack(v_ref[bb, sl, cs]), stack(a_ref[bb, sl, cs])
            cum_s = jnp.concatenate([cum, cum], axis=0)
            lw_s = jnp.concatenate([lw, lw], axis=0)
            kk = k_s * kk_ref[:, cs]
            kk = kk / jnp.maximum(jnp.sqrt(jnp.sum(kk * kk, axis=-1, keepdims=True)), 1e-12)
            kmod = k_s * (1.0 + (a_s - 1.0) * ka_ref[:, cs])
            bvec = kk * a_s
            einv = jnp.exp(-cum_s)
            edec = jnp.exp(cum_last - cum_s)
            at_rt.append(jnp.concatenate([-kk * jnp.exp(cum_s - lw_s), r_s * jnp.exp(cum_s)],
                                         axis=0).astype(BF16))
            bt_kt.append(jnp.concatenate([bvec * einv, kmod * einv], axis=0).astype(BF16))
            bh_kh.append(jnp.concatenate([bvec * edec, kmod * edec], axis=0).astype(BF16))
            v_st.append(v_s)
            decay.append(jnp.exp(cum_last))
            bonus.append(jnp.sum(r_s * kmod * rk_ref[:, cs], axis=-1, keepdims=True) * v_s)

        gram = [mm(at_rt[j], bt_kt[j], NT) for j in pairs]
        a_ab = [jnp.where(strict, gram[j][:n2, :n2], 0.0) for j in pairs]
        a_ak = [jnp.where(strict, gram[j][:n2, n2:], 0.0).astype(BF16) for j in pairs]
        r_b = [jnp.where(incl, gram[j][n2:, :n2], 0.0).astype(BF16) for j in pairs]
        r_k = [jnp.where(incl, gram[j][n2:, n2:], 0.0).astype(BF16) for j in pairs]
        v_bf = [v_st[j].astype(BF16) for j in pairs]

        tinv = [eye + jnp.where(blk[1], a_ab[j], 0.0) for j in pairs]
        for lvl in range(2, 7):
            off = blk[lvl] & ~blk[lvl - 1]
            t_bf = [tinv[j].astype(BF16) for j in pairs]
            w = [mm(jnp.where(off, a_ab[j], 0.0), t_bf[j]) for j in pairs]
            tinv = [tinv[j] + mm(t_bf[j], w[j]) for j in pairs]

        akv = [mm(a_ak[j], v_bf[j]) for j in pairs]
        state = [state_ref[j] for j in pairs]
        s_t = [mm(at_rt[j], state[j], NT) for j in pairs]
        u = [mm(tinv[j], s_t[j][:n2] + akv[j]) for j in pairs]
        u_bf = [u[j].astype(BF16) for j in pairs]
        for j in pairs:
            state_ref[j] = state[j] * decay[j] + mm(
                jnp.concatenate([u_bf[j], v_bf[j]], axis=0), bh_kh[j], TN)
        y = [s_t[j][n2:] + mm(r_b[j], u_bf[j]) + mm(r_k[j], v_bf[j]) for j in pairs]

        inv_n = 1.0 / HEAD_DIM
        for idx, (bb, j) in enumerate(units):
            cs = slice(j * LANES, (j + 1) * LANES)
            mean = jnp.sum(y[idx], axis=-1, keepdims=True) * inv_n
            yc = jnp.where(own, y[idx] - mean, 0.0)
            var = jnp.sum(yc * yc, axis=-1, keepdims=True) * inv_n
            out = yc * lax.rsqrt(var + GN_EPS) * gg_ref[:, cs] + gb_ref[:, cs] + bonus[idx]
            out = jnp.where(lo, out[:c_len], out[c_len:])
            o_ref[bb, sl, cs] = (out * g_ref[bb, sl, cs]).astype(o_ref.dtype)
        return carry

    lax.fori_loop(0, ts // c_len, body, 0)


def _rwkv_scan(r, k, v, lw, a, g, kkp, kap, rkp, ggp, gbp, ts=512, nb=2):
    b, s, _ = r.shape
    xspec = pl.BlockSpec((nb, ts, RWKV_DIM), lambda bi, i: (bi, i, 0))
    pspec = pl.BlockSpec((1, RWKV_DIM), lambda bi, i: (0, 0))
    return pl.pallas_call(
        _rwkv_scan_kernel,
        grid=(b // nb, s // ts),
        in_specs=[xspec] * 6 + [pspec] * 5,
        out_specs=xspec,
        out_shape=jax.ShapeDtypeStruct((b, s, RWKV_DIM), BF16),
        scratch_shapes=[pltpu.VMEM((nb * RWKV_DIM // LANES, LANES, LANES), F32)],
        compiler_params=_params("parallel", "arbitrary"),
        name="rwkv_scan",
    )(r, k, v, lw, a, g, kkp, kap, rkp, ggp, gbp)


def _swa_kernel(sink_ref, q_ref, kvc_ref, kvp_ref, o_ref):
    n = pl.program_id(1)
    blk = WINDOW
    n_sub = q_ref.shape[1] // blk
    kv_all = jnp.concatenate([kvp_ref[0], kvc_ref[0]], axis=0)
    kmat, vmat = kv_all[:, :KV_DIM], kv_all[:, KV_DIM:]
    lo_kv = lax.broadcasted_iota(jnp.int32, kmat.shape, 1) < HEAD_DIM
    k_sw = pltpu.roll(kmat, HEAD_DIM, axis=1)
    v_sw = pltpu.roll(vmat, HEAD_DIM, axis=1)
    k_dup = [jnp.where(lo_kv, kmat, k_sw).astype(BF16), jnp.where(lo_kv, k_sw, kmat).astype(BF16)]
    v_dup = [jnp.where(lo_kv, vmat, v_sw).astype(BF16), jnp.where(lo_kv, v_sw, vmat).astype(BF16)]

    lo_q = lax.broadcasted_iota(jnp.int32, (blk, LANES), 1) < HEAD_DIM
    rows = lax.broadcasted_iota(jnp.int32, (2 * blk, 2 * blk), 0)
    cols = lax.broadcasted_iota(jnp.int32, (2 * blk, 2 * blk), 1)
    rel = (rows & (blk - 1)) + blk - cols
    band = (rel >= 0) & (rel < WINDOW)
    has_prev = (cols >= blk) | (n > 0)
    relf = rel.astype(F32)
    first = lax.broadcasted_iota(jnp.int32, (2 * blk, 1), 0) < blk

    pairs = range(ATT_Q_HEADS // 2)
    subs = range(n_sub)
    scores = {}
    for sb in subs:
        for j in pairs:
            qp = q_ref[0, sb * blk:(sb + 1) * blk, j * LANES:(j + 1) * LANES] * (HEAD_DIM ** -0.5)
            qs = jnp.concatenate([jnp.where(lo_q, qp, 0.0), jnp.where(lo_q, 0.0, qp)], axis=0)
            keys = k_dup[j // 2][sb * blk:(sb + 2) * blk]
            scores[sb, j] = lax.dot_general(qs.astype(BF16), keys, NT, preferred_element_type=F32)
    probs = {}
    for j in pairs:
        slope = jnp.where(first, 2.0 ** -(2 * j + 1), 2.0 ** -(2 * j + 2))
        sink = jnp.where(first, sink_ref[2 * j], sink_ref[2 * j + 1])
        bias = jnp.where(band, -slope * relf, -jnp.inf)
        for sb in subs:
            s = scores[sb, j] + bias
            if sb == 0:
                s = jnp.where(has_prev, s, -jnp.inf)
            m = jnp.maximum(jnp.max(s, axis=-1, keepdims=True), sink)
            e = jnp.exp(s - m)
            p = e / (jnp.sum(e, axis=-1, keepdims=True) + jnp.exp(sink - m))
            probs[sb, j] = p.astype(BF16)
    for sb in subs:
        for j in pairs:
            o = jnp.dot(probs[sb, j], v_dup[j // 2][sb * blk:(sb + 2) * blk], preferred_element_type=F32)
            o_ref[0, sb * blk:(sb + 1) * blk, j * LANES:(j + 1) * LANES] = (
                jnp.where(lo_q, o[:blk], o[blk:]).astype(o_ref.dtype))


def _swa(p_att3, sinks, n_sub=2):
    b, s, _ = p_att3.shape
    blk = WINDOW
    tq = n_sub * blk
    kv_col = ATT_DIM // (2 * KV_DIM)
    return pl.pallas_call(
        _swa_kernel,
        grid=(b, s // tq),
        in_specs=[
            pl.BlockSpec(memory_space=pltpu.SMEM),
            pl.BlockSpec((1, tq, ATT_DIM), lambda bi, n: (bi, n, 0)),
            pl.BlockSpec((1, tq, 2 * KV_DIM), lambda bi, n: (bi, n, kv_col)),
            pl.BlockSpec((1, blk, 2 * KV_DIM),
                         lambda bi, n: (bi, jnp.maximum(n * n_sub - 1, 0), kv_col)),
        ],
        out_specs=pl.BlockSpec((1, tq, ATT_DIM), lambda bi, n: (bi, n, 0)),
        out_shape=jax.ShapeDtypeStruct((b, s, ATT_DIM), BF16),
        compiler_params=_params("parallel", "parallel"),
        name="swa_attention",
    )(sinks, p_att3, p_att3, p_att3)


def _conv_kernel(u_ref, halo_ref, w_ref, b_ref, lg_ref, lb_ref, o_ref, sh_ref, acc_ref):
    i = pl.program_id(1)
    ts = u_ref.shape[1]
    rb = 128
    sh_ref[0, 0:CONV_HALO, :] = jnp.where(i > 0, halo_ref[0], 0.0)
    sh_ref[0, CONV_HALO:, :] = u_ref[0]
    n_sh = ts + CONV_HALO - SUBLANES
    for q in range(1, SUBLANES):
        for cb in range(D_MODEL // LANES):
            cs = slice(cb * LANES, (cb + 1) * LANES)
            sh_ref[q, 0:n_sh, cs] = sh_ref[0, q:q + n_sh, cs]
    off = CONV_HALO - (CONV_WIDTH - 1)
    for cb in range(D_MODEL // LANES):
        cs = slice(cb * LANES, (cb + 1) * LANES)

        def rows(rblk, carry, cs=cs, qs=(), init=True):
            r0 = pl.multiple_of(rblk * rb, rb)
            acc = (jnp.zeros((rb, LANES), F32) + b_ref[:, cs]) if init else acc_ref[pl.ds(r0, rb), cs]
            for q in qs:
                taps = [j for j in range(CONV_WIDTH) if (off + j) % SUBLANES == q]
                a_max = (off + taps[-1]) // SUBLANES
                x = sh_ref[q, pl.ds(r0, rb + a_max * SUBLANES), cs]
                part = None
                for j in taps:
                    a = (off + j) // SUBLANES
                    term = w_ref[j:j + 1, cs] * x[a * SUBLANES:a * SUBLANES + rb]
                    part = term if part is None else part + term
                acc = acc + part
            acc_ref[pl.ds(r0, rb), cs] = acc
            return carry

        half = SUBLANES // 2
        lax.fori_loop(0, ts // rb, functools.partial(rows, qs=range(half), init=True), 0)
        lax.fori_loop(0, ts // rb, functools.partial(rows, qs=range(half, SUBLANES), init=False), 0)
    y = acc_ref[...]
    mean = jnp.mean(y, axis=-1, keepdims=True)
    yc = y - mean
    var = jnp.mean(yc * yc, axis=-1, keepdims=True)
    yn = yc * lax.rsqrt(var + LN_EPS) * lg_ref[...] + lb_ref[...]
    o_ref[0] = (yn * jax.nn.sigmoid(yn)).astype(o_ref.dtype)


def _conv_ln_silu(u3, dw_w, dw_b, ln_g, ln_b, ts=512):
    b, s, _ = u3.shape
    vec = pl.BlockSpec((1, D_MODEL), lambda bi, i: (0, 0))
    return pl.pallas_call(
        _conv_kernel,
        grid=(b, s // ts),
        in_specs=[
            pl.BlockSpec((1, ts, D_MODEL), lambda bi, i: (bi, i, 0)),
            pl.BlockSpec((1, CONV_HALO, D_MODEL),
                         lambda bi, i: (bi, jnp.maximum(i * (ts // CONV_HALO) - 1, 0), 0)),
            pl.BlockSpec((CONV_WIDTH, D_MODEL), lambda bi, i: (0, 0)),
            vec, vec, vec,
        ],
        out_specs=pl.BlockSpec((1, ts, D_MODEL), lambda bi, i: (bi, i, 0)),
        out_shape=jax.ShapeDtypeStruct((b, s, D_MODEL), BF16),
        scratch_shapes=[pltpu.VMEM((SUBLANES, ts + CONV_HALO, D_MODEL), F32),
                        pltpu.VMEM((ts, D_MODEL), F32)],
        compiler_params=_params("parallel", "parallel"),
        name="conv_ln_silu",
    )(u3, u3, dw_w, dw_b, ln_g, ln_b)


def _proj_mlp_kernel(*refs, n_proj, final_norm):
    x_ref = refs[0]
    y_refs = refs[1:1 + n_proj]
    wp_refs = refs[1 + n_proj:1 + 2 * n_proj]
    pb_ref, g_ref, w1_ref, w2_ref, gf_ref, o_ref, x1_ref, h_ref, acc_ref = refs[1 + 2 * n_proj:]
    j = pl.program_id(1)

    @pl.when(j == 0)
    def _():
        x1 = x_ref[...] + pb_ref[...]
        for y_ref, wp_ref in zip(y_refs, wp_refs):
            x1 = x1 + jnp.dot(y_ref[...], wp_ref[...], preferred_element_type=F32)
        x1_ref[...] = x1
        h_ref[...] = _rms(x1, g_ref[...]).astype(BF16)
        acc_ref[...] = jnp.zeros_like(acc_ref)

    hid = jnp.dot(h_ref[...], w1_ref[...], preferred_element_type=F32)
    hid = jnp.square(jnp.maximum(hid, 0.0)).astype(BF16)
    acc_ref[...] += jnp.dot(hid, w2_ref[...], preferred_element_type=F32)

    @pl.when(j == pl.num_programs(1) - 1)
    def _():
        out = x1_ref[...] + acc_ref[...]
        if final_norm:
            out = _rms(out, gf_ref[...])
        o_ref[...] = out


def _proj_mlp(x2, ys, wps, pb, g, w1, w2, gf, final_norm, tm=1024, tf=1024):
    t = x2.shape[0]
    n_proj = len(ys)
    vec = pl.BlockSpec((1, D_MODEL), lambda i, j: (0, 0))
    in_specs = [pl.BlockSpec((tm, D_MODEL), lambda i, j: (i, 0))]
    in_specs += [pl.BlockSpec((tm, y.shape[1]), lambda i, j: (i, 0)) for y in ys]
    in_specs += [pl.BlockSpec(w.shape, lambda i, j: (0, 0)) for w in wps]
    in_specs += [vec, vec,
                 pl.BlockSpec((D_MODEL, tf), lambda i, j: (0, j)),
                 pl.BlockSpec((tf, D_MODEL), lambda i, j: (j, 0)),
                 vec]
    return pl.pallas_call(
        functools.partial(_proj_mlp_kernel, n_proj=n_proj, final_norm=final_norm),
        grid=(t // tm, D_FF // tf),
        in_specs=in_specs,
        out_specs=pl.BlockSpec((tm, D_MODEL), lambda i, j: (i, 0)),
        out_shape=jax.ShapeDtypeStruct((t, D_MODEL), F32),
        scratch_shapes=[pltpu.VMEM((tm, D_MODEL), F32),
                        pltpu.VMEM((tm, D_MODEL), BF16),
                        pltpu.VMEM((tm, D_MODEL), F32)],
        compiler_params=_params("parallel", "arbitrary"),
        name="proj_mlp",
    )(x2, *ys, *wps, pb, g, w1, w2, gf)


def _row(v):
    return v.reshape(1, -1).astype(F32)


def _hybrid_front(x3, g_mix, w_in, mu, w0, w_up, a0, a_up, g_up):
    c3 = 3 * RWKV_DIM
    w = jnp.concatenate([w_in[:, c3:SHIFT_DIM], w_in[:, :c3], w_in[:, SHIFT_DIM:]], axis=1).astype(BF16)
    mu_p = jnp.concatenate([mu[c3:], mu[:c3]]).reshape(1, -1).astype(F32)
    w_lora = jnp.zeros((LORA_DIM, c3), F32)
    w_lora = w_lora.at[:LORA_W, :RWKV_DIM].set(w_up)
    w_lora = w_lora.at[LORA_W:LORA_W + LORA_A, RWKV_DIM:2 * RWKV_DIM].set(a_up)
    w_lora = w_lora.at[LORA_W + LORA_A:, 2 * RWKV_DIM:].set(g_up)
    bias = jnp.concatenate([w0, a0, jnp.zeros((RWKV_DIM,), F32)]).reshape(1, -1)
    return _in_proj(x3, g_mix, w, mu_p, w_lora.astype(BF16), bias)


def kernel(x, norm_mix_g, norm_ffn_g, final_norm_g, hy_w_in, hy_mu, hy_w0, hy_w_up, hy_a0, hy_a_up, hy_g_up, hy_k_k, hy_k_a, hy_r_k, hy_gn_g, hy_gn_b, hy_sinks, hy_w_out, cv_pw1_w, cv_pw1_b, cv_dw_w, cv_dw_b, cv_ln_g, cv_ln_b, cv_pw2_w, cv_pw2_b, mlp_w1, mlp_w2):
    bsz, seq, d = x.shape
    depth = norm_mix_g.shape[0]
    t = bsz * seq
    x2 = x.reshape(t, d)
    zero_row = jnp.zeros((1, D_MODEL), F32)
    gf = _row(final_norm_g)

    for layer in range(depth):
        i = layer // 2
        g_mix = _row(norm_mix_g[layer])
        if layer % 2 == 0:
            r, k, v, lw, a, gate, p_att = _hybrid_front(
                x2.reshape(bsz, seq, d), g_mix, hy_w_in[i], hy_mu[i], hy_w0[i], hy_w_up[i],
                hy_a0[i], hy_a_up[i], hy_g_up[i])
            y_rwkv = _rwkv_scan(r, k, v, lw, a, gate, _row(hy_k_k[i]), _row(hy_k_a[i]),
                                _row(hy_r_k[i]), _row(hy_gn_g[i]), _row(hy_gn_b[i]))
            y_att = _swa(p_att, hy_sinks[i].astype(F32))
            w_out = hy_w_out[i].astype(BF16)
            ys = [y_rwkv.reshape(t, RWKV_DIM), y_att.reshape(t, ATT_DIM)]
            wps = [w_out[:RWKV_DIM], w_out[RWKV_DIM:]]
            pb = zero_row
        else:
            u = _norm_glu(x2, g_mix, cv_pw1_w[i].astype(BF16), _row(cv_pw1_b[i]))
            u = _conv_ln_silu(u.reshape(bsz, seq, D_MODEL), cv_dw_w[i].astype(F32),
                              _row(cv_dw_b[i]), _row(cv_ln_g[i]), _row(cv_ln_b[i]))
            ys = [u.reshape(t, D_MODEL)]
            wps = [cv_pw2_w[i].astype(BF16)]
            pb = _row(cv_pw2_b[i])
        x2 = _proj_mlp(x2, ys, wps, pb, _row(norm_ffn_g[layer]),
                       mlp_w1[layer].astype(BF16), mlp_w2[layer].astype(BF16), gf,
                       final_norm=(layer == depth - 1))
    return x2.reshape(bsz, seq, d)
```

```python
import functools

import jax
import jax.numpy as jnp
from jax import lax
from jax.experimental import pallas as pl
from jax.experimental.pallas import tpu as pltpu

D_MODEL = 1024
HEAD_DIM = 64
RWKV_DIM = 512
ATT_Q_HEADS = 8
ATT_DIM = 512
KV_DIM = 128
LORA_W = 64
LORA_A = 64
LORA_G = 128
SHIFT_DIM = 3 * RWKV_DIM + LORA_W + LORA_A + LORA_G
ATT_COLS = ATT_DIM + 2 * KV_DIM
WINDOW = 128
CONV_WIDTH = 31
CONV_HALO = 32
D_FF = 4 * D_MODEL
RMS_EPS = 1e-6
LN_EPS = 1e-5
GN_EPS = 64e-5

LANES = 128
SUBLANES = 8
CHUNK = 64
SCAN_SLOTS = 4
VMEM_LIMIT = 56 * 1024 * 1024

F32 = jnp.float32
BF16 = jnp.bfloat16
NN = (((1,), (0,)), ((), ()))
NT = (((1,), (1,)), ((), ()))
TN = (((0,), (0,)), ((), ()))


def _params(*sem):
    return pltpu.CompilerParams(dimension_semantics=sem, vmem_limit_bytes=VMEM_LIMIT)


def _rms(x, g):
    return x * lax.rsqrt(jnp.mean(x * x, axis=-1, keepdims=True) + RMS_EPS) * g


LORA_DIM = LORA_W + LORA_A + LORA_G


def _in_proj_kernel(x_ref, g_ref, w_ref, mu_ref, wl_ref, bias_ref,
                    r_ref, k_ref, v_ref, lw_ref, a_ref, gate_ref, att_ref, prev_ref):
    tm = x_ref.shape[1]
    c1, c2, c3 = RWKV_DIM, 2 * RWKV_DIM, 3 * RWKV_DIM

    @pl.when(pl.program_id(1) == 0)
    def _():
        prev_ref[...] = jnp.zeros_like(prev_ref)

    def shift_lerp(p, cs):
        first = lax.broadcasted_iota(jnp.int32, p.shape, 0) == 0
        shifted = jnp.where(first, prev_ref[SUBLANES - 1:SUBLANES, cs], pltpu.roll(p, 1, axis=0))
        prev_ref[:, cs] = p[tm - SUBLANES:, :]
        return p + (shifted - p) * mu_ref[:, cs]

    h = _rms(x_ref[0], g_ref[...]).astype(BF16)
    z = jnp.dot(h, w_ref[:, :LORA_DIM], preferred_element_type=F32)
    z = shift_lerp(z, slice(0, LORA_DIM))
    lane = lax.broadcasted_iota(jnp.int32, z.shape, 1)
    act = jnp.where(lane < LORA_W, jnp.tanh(z),
                    jnp.where(lane < LORA_W + LORA_A, z, jax.nn.sigmoid(z)))
    lo = jnp.dot(act.astype(BF16), wl_ref[...], preferred_element_type=F32) + bias_ref[...]
    rkv = jnp.dot(h, w_ref[:, LORA_DIM:SHIFT_DIM], preferred_element_type=F32)
    att_ref[0] = jnp.dot(h, w_ref[:, SHIFT_DIM:], preferred_element_type=F32)
    rkv = shift_lerp(rkv, slice(LORA_DIM, SHIFT_DIM))
    r_ref[0] = rkv[:, :c1]
    k_ref[0] = rkv[:, c1:c2]
    v_ref[0] = rkv[:, c2:c3]
    wpre = lo[:, :c1]
    softplus = jnp.maximum(-wpre, 0.0) + jnp.log(1.0 + jnp.exp(-jnp.abs(wpre)))
    lw_ref[0] = -jnp.exp(-softplus - 0.5)
    a_ref[0] = jax.nn.sigmoid(lo[:, c1:c2])
    gate_ref[0] = lo[:, c2:]


def _in_proj(x3, g, w, mu, w_lora, bias, tm=512):
    b, s, _ = x3.shape
    rspec = pl.BlockSpec((1, tm, RWKV_DIM), lambda bi, i: (bi, i, 0))
    rshape = jax.ShapeDtypeStruct((b, s, RWKV_DIM), F32)
    const = lambda shape: pl.BlockSpec(shape, lambda bi, i: (0, 0))
    return pl.pallas_call(
        _in_proj_kernel,
        grid=(b, s // tm),
        in_specs=[
            pl.BlockSpec((1, tm, D_MODEL), lambda bi, i: (bi, i, 0)),
            const((1, D_MODEL)), const(w.shape), const((1, SHIFT_DIM)),
            const(w_lora.shape), const((1, 3 * RWKV_DIM)),
        ],
        out_specs=[rspec] * 6 + [pl.BlockSpec((1, tm, ATT_COLS), lambda bi, i: (bi, i, 0))],
        out_shape=[rshape] * 6 + [jax.ShapeDtypeStruct((b, s, ATT_COLS), F32)],
        scratch_shapes=[pltpu.VMEM((SUBLANES, SHIFT_DIM), F32)],
        compiler_params=_params("parallel", "arbitrary"),
        name="in_proj",
    )(x3, g, w, mu, w_lora, bias)


def _norm_glu_kernel(x_ref, g_ref, w_ref, b_ref, o_ref):
    h = _rms(x_ref[...], g_ref[...]).astype(BF16)
    acc = jnp.dot(h, w_ref[...], preferred_element_type=F32) + b_ref[...]
    o_ref[...] = acc[:, :D_MODEL] * jax.nn.sigmoid(acc[:, D_MODEL:])


def _norm_glu(x2, g, w, b, tm=512):
    t = x2.shape[0]
    return pl.pallas_call(
        _norm_glu_kernel,
        grid=(t // tm,),
        in_specs=[
            pl.BlockSpec((tm, D_MODEL), lambda i: (i, 0)),
            pl.BlockSpec((1, D_MODEL), lambda i: (0, 0)),
            pl.BlockSpec((D_MODEL, 2 * D_MODEL), lambda i: (0, 0)),
            pl.BlockSpec((1, 2 * D_MODEL), lambda i: (0, 0)),
        ],
        out_specs=pl.BlockSpec((tm, D_MODEL), lambda i: (i, 0)),
        out_shape=jax.ShapeDtypeStruct((t, D_MODEL), F32),
        compiler_params=_params("parallel"),
        name="norm_glu",
    )(x2, g, w, b)


def _split3(x):
    hi = x.astype(BF16)
    r1 = x - hi.astype(F32)
    mid = r1.astype(BF16)
    lo = (r1 - mid.astype(F32)).astype(BF16)
    return hi, mid, lo


def _rwkv_scan_kernel(r_ref, k_ref, v_ref, lw_ref, a_ref, g_ref,
                      kk_ref, ka_ref, rk_ref, gg_ref, gb_ref, o_ref, state_ref,
                      atrt_ref, btkt_ref, bhkh_ref, vbf_ref, decay_ref, bonus_ref):
    c_len = CHUNK
    n2 = 2 * c_len
    nb, ts = r_ref.shape[0], r_ref.shape[1]
    n_pairs = RWKV_DIM // LANES
    lane = lax.broadcasted_iota(jnp.int32, (c_len, LANES), 1)
    lo = lane < HEAD_DIM
    lo2 = lax.broadcasted_iota(jnp.int32, (n2, LANES), 1) < HEAD_DIM
    top = lax.broadcasted_iota(jnp.int32, (n2, LANES), 0) < c_len
    own = lo2 == top

    def stack(x):
        return jnp.concatenate([jnp.where(lo, x, 0.0), jnp.where(lo, 0.0, x)], axis=0)

    ri = lax.broadcasted_iota(jnp.int32, (n2, n2), 0)
    ci = lax.broadcasted_iota(jnp.int32, (n2, n2), 1)
    strict = ri > ci
    incl = ri >= ci
    eye = (ri == ci).astype(F32)
    blk = [(ri >> l) == (ci >> l) for l in range(7)]
    tri = (lax.broadcasted_iota(jnp.int32, (c_len, c_len), 0)
           >= lax.broadcasted_iota(jnp.int32, (c_len, c_len), 1)).astype(BF16)

    @pl.when(pl.program_id(1) == 0)
    def _():
        state_ref[...] = jnp.zeros_like(state_ref)

    units = [(bb, j) for bb in range(nb) for j in range(n_pairs)]
    pairs = range(len(units))

    def mm(a, b, dims=NN):
        return lax.dot_general(a.astype(BF16), b.astype(BF16), dims, preferred_element_type=F32)

    def prep(c, slot):
        sl = pl.ds(pl.multiple_of(c * c_len, c_len), c_len)
        lw_all, cum_all = [], []
        for bb in range(nb):
            lw_b = lw_ref[bb, sl, :]
            cum3 = jnp.dot(tri, jnp.concatenate(_split3(lw_b), axis=1), preferred_element_type=F32)
            lw_all.append(lw_b)
            cum_all.append(cum3[:, :RWKV_DIM] + cum3[:, RWKV_DIM:2 * RWKV_DIM] + cum3[:, 2 * RWKV_DIM:])

        for idx, (bb, j) in enumerate(units):
            cs = slice(j * LANES, (j + 1) * LANES)
            cum, lw = cum_all[bb][:, cs], lw_all[bb][:, cs]
            cum_last = cum[c_len - 1:c_len, :]
            r_s, k_s = stack(r_ref[bb, sl, cs]), stack(k_ref[bb, sl, cs])
            v_s, a_s = stack(v_ref[bb, sl, cs]), stack(a_ref[bb, sl, cs])
            cum_s = jnp.concatenate([cum, cum], axis=0)
            lw_s = jnp.concatenate([lw, lw], axis=0)
            kk = k_s * kk_ref[:, cs]
            kk = kk / jnp.maximum(jnp.sqrt(jnp.sum(kk * kk, axis=-1, keepdims=True)), 1e-12)
            kmod = k_s * (1.0 + (a_s - 1.0) * ka_ref[:, cs])
            bvec = kk * a_s
            einv = jnp.exp(-cum_s)
            edec = jnp.exp(cum_last - cum_s)
            atrt_ref[slot, idx] = jnp.concatenate(
                [-kk * jnp.exp(cum_s - lw_s), r_s * jnp.exp(cum_s)], axis=0).astype(BF16)
            btkt_ref[slot, idx] = jnp.concatenate([bvec * einv, kmod * einv], axis=0).astype(BF16)
            bhkh_ref[slot, idx] = jnp.concatenate([bvec * edec, kmod * edec], axis=0).astype(BF16)
            vbf_ref[slot, idx] = v_s.astype(BF16)
            decay_ref[slot, idx] = jnp.exp(cum_last)
            bonus_ref[slot, idx] = jnp.sum(r_s * kmod * rk_ref[:, cs], axis=-1, keepdims=True) * v_s

    def inverse_part(slots):
        chains = [(slot, j) for slot in slots for j in pairs]
        gram = [mm(atrt_ref[slot, j], btkt_ref[slot, j], NT) for slot, j in chains]
        a_ab = [jnp.where(strict, g[:n2, :n2], 0.0).astype(BF16) for g in gram]
        a_ak = [jnp.where(strict, g[:n2, n2:], 0.0).astype(BF16) for g in gram]
        r_b = [jnp.where(incl, g[n2:, :n2], 0.0).astype(BF16) for g in gram]
        r_k = [jnp.where(incl, g[n2:, n2:], 0.0).astype(BF16) for g in gram]
        tinv = [(eye + jnp.where(strict & blk[1], g[:n2, :n2], 0.0)).astype(BF16) for g in gram]
        for lvl in range(2, 7):
            off = blk[lvl] & ~blk[lvl - 1]
            w = [jnp.where(off, mm(a, t), 0.0).astype(BF16) for a, t in zip(a_ab, tinv)]
            tinv = [t + mm(t, wi).astype(BF16) for t, wi in zip(tinv, w)]
        akv = [mm(a_ak[i], vbf_ref[slot, j]) for i, (slot, j) in enumerate(chains)]
        n = len(pairs)
        return {slot: (tinv[k * n:(k + 1) * n], akv[k * n:(k + 1) * n], r_b[k * n:(k + 1) * n],
                       r_k[k * n:(k + 1) * n]) for k, slot in enumerate(slots)}

    def state_part(c, slot, tinv, akv, r_b, r_k):
        sl = pl.ds(pl.multiple_of(c * c_len, c_len), c_len)
        at_rt = [atrt_ref[slot, j] for j in pairs]
        v_bf = [vbf_ref[slot, j] for j in pairs]
        state = [state_ref[j] for j in pairs]
        s_t = [mm(at_rt[j], state[j], NT) for j in pairs]
        u = [mm(tinv[j], s_t[j][:n2] + akv[j]) for j in pairs]
        u_bf = [u[j].astype(BF16) for j in pairs]
        for j in pairs:
            state_ref[j] = state[j] * decay_ref[slot, j] + mm(
                jnp.concatenate([u_bf[j], v_bf[j]], axis=0), bhkh_ref[slot, j], TN)
        y = [s_t[j][n2:] + mm(r_b[j], u_bf[j]) + mm(r_k[j], v_bf[j]) for j in pairs]

        inv_n = 1.0 / HEAD_DIM
        for idx, (bb, j) in enumerate(units):
            cs = slice(j * LANES, (j + 1) * LANES)
            mean = jnp.sum(y[idx], axis=-1, keepdims=True) * inv_n
            yc = jnp.where(own, y[idx] - mean, 0.0)
            var = jnp.sum(yc * yc, axis=-1, keepdims=True) * inv_n
            out = yc * lax.rsqrt(var + GN_EPS) * gg_ref[:, cs] + gb_ref[:, cs] + bonus_ref[slot, idx]
            out = jnp.where(lo, out[:c_len], out[c_len:])
            o_ref[bb, sl, cs] = (out * g_ref[bb, sl, cs]).astype(o_ref.dtype)

    n_chunks = ts // c_len
    prep(0, 0)
    prep(1, 1)

    def two_chunks(c, slots, c_next):
        for k, slot in enumerate(slots):
            prep(jnp.minimum(c_next + k, n_chunks - 1), (slot + 2) % SCAN_SLOTS)
        inv = inverse_part(slots)
        for k, slot in enumerate(slots):
            state_part(c + k, slot, *inv[slot])

    def body(i, carry):
        c0 = 4 * i
        two_chunks(c0, (0, 1), c0 + 2)
        two_chunks(c0 + 2, (2, 3), c0 + 4)
        return carry

    lax.fori_loop(0, n_chunks // 4, body, 0)


def _rwkv_scan(r, k, v, lw, a, g, kkp, kap, rkp, ggp, gbp, ts=512, nb=2):
    b, s, _ = r.shape
    n_units = nb * RWKV_DIM // LANES
    xspec = pl.BlockSpec((nb, ts, RWKV_DIM), lambda bi, i: (bi, i, 0))
    pspec = pl.BlockSpec((1, RWKV_DIM), lambda bi, i: (0, 0))
    return pl.pallas_call(
        _rwkv_scan_kernel,
        grid=(b // nb, s // ts),
        in_specs=[xspec] * 6 + [pspec] * 5,
        out_specs=xspec,
        out_shape=jax.ShapeDtypeStruct((b, s, RWKV_DIM), BF16),
        scratch_shapes=[
            pltpu.VMEM((n_units, LANES, LANES), F32),
            pltpu.VMEM((SCAN_SLOTS, n_units, 4 * CHUNK, LANES), BF16),
            pltpu.VMEM((SCAN_SLOTS, n_units, 4 * CHUNK, LANES), BF16),
            pltpu.VMEM((SCAN_SLOTS, n_units, 4 * CHUNK, LANES), BF16),
            pltpu.VMEM((SCAN_SLOTS, n_units, 2 * CHUNK, LANES), BF16),
            pltpu.VMEM((SCAN_SLOTS, n_units, 1, LANES), F32),
            pltpu.VMEM((SCAN_SLOTS, n_units, 2 * CHUNK, LANES), F32),
        ],
        compiler_params=_params("parallel", "arbitrary"),
        name="rwkv_scan",
    )(r, k, v, lw, a, g, kkp, kap, rkp, ggp, gbp)


def _swa_kernel(sink_ref, q_ref, kvc_ref, kvp_ref, o_ref):
    n = pl.program_id(1)
    blk = WINDOW
    n_sub = q_ref.shape[1] // blk
    kv_all = jnp.concatenate([kvp_ref[0], kvc_ref[0]], axis=0)
    kmat, vmat = kv_all[:, :KV_DIM], kv_all[:, KV_DIM:]
    lo_kv = lax.broadcasted_iota(jnp.int32, kmat.shape, 1) < HEAD_DIM
    k_sw = pltpu.roll(kmat, HEAD_DIM, axis=1)
    v_sw = pltpu.roll(vmat, HEAD_DIM, axis=1)
    k_dup = [jnp.where(lo_kv, kmat, k_sw).astype(BF16), jnp.where(lo_kv, k_sw, kmat).astype(BF16)]
    v_dup = [jnp.where(lo_kv, vmat, v_sw).astype(BF16), jnp.where(lo_kv, v_sw, vmat).astype(BF16)]

    lo_q = lax.broadcasted_iota(jnp.int32, (blk, LANES), 1) < HEAD_DIM
    rows = lax.broadcasted_iota(jnp.int32, (2 * blk, 2 * blk), 0)
    cols = lax.broadcasted_iota(jnp.int32, (2 * blk, 2 * blk), 1)
    rel = (rows & (blk - 1)) + blk - cols
    band = (rel >= 0) & (rel < WINDOW)
    has_prev = (cols >= blk) | (n > 0)
    relf = rel.astype(F32)
    first = lax.broadcasted_iota(jnp.int32, (2 * blk, 1), 0) < blk

    pairs = range(ATT_Q_HEADS // 2)
    subs = range(n_sub)
    scores = {}
    for sb in subs:
        for j in pairs:
            qp = q_ref[0, sb * blk:(sb + 1) * blk, j * LANES:(j + 1) * LANES] * (HEAD_DIM ** -0.5)
            qs = jnp.concatenate([jnp.where(lo_q, qp, 0.0), jnp.where(lo_q, 0.0, qp)], axis=0)
            keys = k_dup[j // 2][sb * blk:(sb + 2) * blk]
            scores[sb, j] = lax.dot_general(qs.astype(BF16), keys, NT, preferred_element_type=F32)
    probs = {}
    for j in pairs:
        slope = jnp.where(first, 2.0 ** -(2 * j + 1), 2.0 ** -(2 * j + 2))
        sink = jnp.where(first, sink_ref[2 * j], sink_ref[2 * j + 1])
        bias = jnp.where(band, -slope * relf, -jnp.inf)
        for sb in subs:
            s = scores[sb, j] + bias
            if sb == 0:
                s = jnp.where(has_prev, s, -jnp.inf)
            m = jnp.maximum(jnp.max(s, axis=-1, keepdims=True), sink)
            e = jnp.exp(s - m)
            p = e / (jnp.sum(e, axis=-1, keepdims=True) + jnp.exp(sink - m))
            probs[sb, j] = p.astype(BF16)
    for sb in subs:
        for j in pairs:
            o = jnp.dot(probs[sb, j], v_dup[j // 2][sb * blk:(sb + 2) * blk], preferred_element_type=F32)
            o_ref[0, sb * blk:(sb + 1) * blk, j * LANES:(j + 1) * LANES] = (
                jnp.where(lo_q, o[:blk], o[blk:]).astype(o_ref.dtype))


def _swa(p_att3, sinks, n_sub=2):
    b, s, _ = p_att3.shape
    blk = WINDOW
    tq = n_sub * blk
    kv_col = ATT_DIM // (2 * KV_DIM)
    return pl.pallas_call(
        _swa_kernel,
        grid=(b, s // tq),
        in_specs=[
            pl.BlockSpec(memory_space=pltpu.SMEM),
            pl.BlockSpec((1, tq, ATT_DIM), lambda bi, n: (bi, n, 0)),
            pl.BlockSpec((1, tq, 2 * KV_DIM), lambda bi, n: (bi, n, kv_col)),
            pl.BlockSpec((1, blk, 2 * KV_DIM),
                         lambda bi, n: (bi, jnp.maximum(n * n_sub - 1, 0), kv_col)),
        ],
        out_specs=pl.BlockSpec((1, tq, ATT_DIM), lambda bi, n: (bi, n, 0)),
        out_shape=jax.ShapeDtypeStruct((b, s, ATT_DIM), BF16),
        compiler_params=_params("parallel", "parallel"),
        name="swa_attention",
    )(sinks, p_att3, p_att3, p_att3)


def _conv_kernel(u_ref, halo_ref, w_ref, b_ref, lg_ref, lb_ref, o_ref, sh_ref, acc_ref):
    i = pl.program_id(1)
    ts = u_ref.shape[1]
    rb = 128
    sh_ref[0, 0:CONV_HALO, :] = jnp.where(i > 0, halo_ref[0], 0.0)
    sh_ref[0, CONV_HALO:, :] = u_ref[0]
    n_sh = ts + CONV_HALO - SUBLANES
    for q in range(1, SUBLANES):
        for cb in range(D_MODEL // LANES):
            cs = slice(cb * LANES, (cb + 1) * LANES)
            sh_ref[q, 0:n_sh, cs] = sh_ref[0, q:q + n_sh, cs]
    off = CONV_HALO - (CONV_WIDTH - 1)
    for cb in range(D_MODEL // LANES):
        cs = slice(cb * LANES, (cb + 1) * LANES)

        def rows(rblk, carry, cs=cs, qs=(), init=True):
            r0 = pl.multiple_of(rblk * rb, rb)
            acc = (jnp.zeros((rb, LANES), F32) + b_ref[:, cs]) if init else acc_ref[pl.ds(r0, rb), cs]
            for q in qs:
                taps = [j for j in range(CONV_WIDTH) if (off + j) % SUBLANES == q]
                a_max = (off + taps[-1]) // SUBLANES
                x = sh_ref[q, pl.ds(r0, rb + a_max * SUBLANES), cs]
                part = None
                for j in taps:
                    a = (off + j) // SUBLANES
                    term = w_ref[j:j + 1, cs] * x[a * SUBLANES:a * SUBLANES + rb]
                    part = term if part is None else part + term
                acc = acc + part
            acc_ref[pl.ds(r0, rb), cs] = acc
            return carry

        half = SUBLANES // 2
        lax.fori_loop(0, ts // rb, functools.partial(rows, qs=range(half), init=True), 0)
        lax.fori_loop(0, ts // rb, functools.partial(rows, qs=range(half, SUBLANES), init=False), 0)
    y = acc_ref[...]
    mean = jnp.mean(y, axis=-1, keepdims=True)
    yc = y - mean
    var = jnp.mean(yc * yc, axis=-1, keepdims=True)
    yn = yc * lax.rsqrt(var + LN_EPS) * lg_ref[...] + lb_ref[...]
    o_ref[0] = (yn * jax.nn.sigmoid(yn)).astype(o_ref.dtype)


def _conv_ln_silu(u3, dw_w, dw_b, ln_g, ln_b, ts=512):
    b, s, _ = u3.shape
    vec = pl.BlockSpec((1, D_MODEL), lambda bi, i: (0, 0))
    return pl.pallas_call(
        _conv_kernel,
        grid=(b, s // ts),
        in_specs=[
            pl.BlockSpec((1, ts, D_MODEL), lambda bi, i: (bi, i, 0)),
            pl.BlockSpec((1, CONV_HALO, D_MODEL),
                         lambda bi, i: (bi, jnp.maximum(i * (ts // CONV_HALO) - 1, 0), 0)),
            pl.BlockSpec((CONV_WIDTH, D_MODEL), lambda bi, i: (0, 0)),
            vec, vec, vec,
        ],
        out_specs=pl.BlockSpec((1, ts, D_MODEL), lambda bi, i: (bi, i, 0)),
        out_shape=jax.ShapeDtypeStruct((b, s, D_MODEL), BF16),
        scratch_shapes=[pltpu.VMEM((SUBLANES, ts + CONV_HALO, D_MODEL), F32),
                        pltpu.VMEM((ts, D_MODEL), F32)],
        compiler_params=_params("parallel", "parallel"),
        name="conv_ln_silu",
    )(u3, u3, dw_w, dw_b, ln_g, ln_b)


def _proj_mlp_kernel(*refs, n_proj, final_norm):
    x_ref = refs[0]
    y_refs = refs[1:1 + n_proj]
    wp_refs = refs[1 + n_proj:1 + 2 * n_proj]
    pb_ref, g_ref, w1_ref, w2_ref, gf_ref, o_ref, x1_ref, h_ref, acc_ref = refs[1 + 2 * n_proj:]
    j = pl.program_id(1)

    @pl.when(j == 0)
    def _():
        x1 = x_ref[...] + pb_ref[...]
        for y_ref, wp_ref in zip(y_refs, wp_refs):
            x1 = x1 + jnp.dot(y_ref[...], wp_ref[...], preferred_element_type=F32)
        x1_ref[...] = x1
        h_ref[...] = _rms(x1, g_ref[...]).astype(BF16)
        acc_ref[...] = jnp.zeros_like(acc_ref)

    hid = jnp.dot(h_ref[...], w1_ref[...], preferred_element_type=F32)
    hid = jnp.square(jnp.maximum(hid, 0.0)).astype(BF16)
    acc_ref[...] += jnp.dot(hid, w2_ref[...], preferred_element_type=F32)

    @pl.when(j == pl.num_programs(1) - 1)
    def _():
        out = x1_ref[...] + acc_ref[...]
        if final_norm:
            out = _rms(out, gf_ref[...])
        o_ref[...] = out


def _proj_mlp(x2, ys, wps, pb, g, w1, w2, gf, final_norm, tm=1024, tf=1024):
    t = x2.shape[0]
    n_proj = len(ys)
    vec = pl.BlockSpec((1, D_MODEL), lambda i, j: (0, 0))
    in_specs = [pl.BlockSpec((tm, D_MODEL), lambda i, j: (i, 0))]
    in_specs += [pl.BlockSpec((tm, y.shape[1]), lambda i, j: (i, 0)) for y in ys]
    in_specs += [pl.BlockSpec(w.shape, lambda i, j: (0, 0)) for w in wps]
    in_specs += [vec, vec,
                 pl.BlockSpec((D_MODEL, tf), lambda i, j: (0, j)),
                 pl.BlockSpec((tf, D_MODEL), lambda i, j: (j, 0)),
                 vec]
    return pl.pallas_call(
        functools.partial(_proj_mlp_kernel, n_proj=n_proj, final_norm=final_norm),
        grid=(t // tm, D_FF // tf),
        in_specs=in_specs,
        out_specs=pl.BlockSpec((tm, D_MODEL), lambda i, j: (i, 0)),
        out_shape=jax.ShapeDtypeStruct((t, D_MODEL), F32),
        scratch_shapes=[pltpu.VMEM((tm, D_MODEL), F32),
                        pltpu.VMEM((tm, D_MODEL), BF16),
                        pltpu.VMEM((tm, D_MODEL), F32)],
        compiler_params=_params("parallel", "arbitrary"),
        name="proj_mlp",
    )(x2, *ys, *wps, pb, g, w1, w2, gf)


def _row(v):
    return v.reshape(1, -1).astype(F32)


def _hybrid_front(x3, g_mix, w_in, mu, w0, w_up, a0, a_up, g_up):
    c3 = 3 * RWKV_DIM
    w = jnp.concatenate([w_in[:, c3:SHIFT_DIM], w_in[:, :c3], w_in[:, SHIFT_DIM:]], axis=1).astype(BF16)
    mu_p = jnp.concatenate([mu[c3:], mu[:c3]]).reshape(1, -1).astype(F32)
    w_lora = jnp.zeros((LORA_DIM, c3), F32)
    w_lora = w_lora.at[:LORA_W, :RWKV_DIM].set(w_up)
    w_lora = w_lora.at[LORA_W:LORA_W + LORA_A, RWKV_DIM:2 * RWKV_DIM].set(a_up)
    w_lora = w_lora.at[LORA_W + LORA_A:, 2 * RWKV_DIM:].set(g_up)
    bias = jnp.concatenate([w0, a0, jnp.zeros((RWKV_DIM,), F32)]).reshape(1, -1)
    return _in_proj(x3, g_mix, w, mu_p, w_lora.astype(BF16), bias)


def kernel(x, norm_mix_g, norm_ffn_g, final_norm_g, hy_w_in, hy_mu, hy_w0, hy_w_up, hy_a0, hy_a_up, hy_g_up, hy_k_k, hy_k_a, hy_r_k, hy_gn_g, hy_gn_b, hy_sinks, hy_w_out, cv_pw1_w, cv_pw1_b, cv_dw_w, cv_dw_b, cv_ln_g, cv_ln_b, cv_pw2_w, cv_pw2_b, mlp_w1, mlp_w2):
    bsz, seq, d = x.shape
    depth = norm_mix_g.shape[0]
    t = bsz * seq
    x2 = x.reshape(t, d)
    zero_row = jnp.zeros((1, D_MODEL), F32)
    gf = _row(final_norm_g)

    for layer in range(depth):
        i = layer // 2
        g_mix = _row(norm_mix_g[layer])
        if layer % 2 == 0:
            r, k, v, lw, a, gate, p_att = _hybrid_front(
                x2.reshape(bsz, seq, d), g_mix, hy_w_in[i], hy_mu[i], hy_w0[i], hy_w_up[i],
                hy_a0[i], hy_a_up[i], hy_g_up[i])
            y_rwkv = _rwkv_scan(r, k, v, lw, a, gate, _row(hy_k_k[i]), _row(hy_k_a[i]),
                                _row(hy_r_k[i]), _row(hy_gn_g[i]), _row(hy_gn_b[i]))
            y_att = _swa(p_att, hy_sinks[i].astype(F32))
            w_out = hy_w_out[i].astype(BF16)
            ys = [y_rwkv.reshape(t, RWKV_DIM), y_att.reshape(t, ATT_DIM)]
            wps = [w_out[:RWKV_DIM], w_out[RWKV_DIM:]]
            pb = zero_row
        else:
            u = _norm_glu(x2, g_mix, cv_pw1_w[i].astype(BF16), _row(cv_pw1_b[i]))
            u = _conv_ln_silu(u.reshape(bsz, seq, D_MODEL), cv_dw_w[i].astype(F32),
                              _row(cv_dw_b[i]), _row(cv_ln_g[i]), _row(cv_ln_b[i]))
            ys = [u.reshape(t, D_MODEL)]
            wps = [cv_pw2_w[i].astype(BF16)]
            pb = _row(cv_pw2_b[i])
        x2 = _proj_mlp(x2, ys, wps, pb, _row(norm_ffn_g[layer]),
                       mlp_w1[layer].astype(BF16), mlp_w2[layer].astype(BF16), gf,
                       final_norm=(layer == depth - 1))
    return x2.reshape(bsz, seq, d)
```

```python
import functools

import jax
import jax.numpy as jnp
from jax import lax
from jax.experimental import pallas as pl
from jax.experimental.pallas import tpu as pltpu

D_MODEL = 1024
HEAD_DIM = 64
RWKV_DIM = 512
ATT_Q_HEADS = 8
ATT_DIM = 512
KV_DIM = 128
LORA_W = 64
LORA_A = 64
LORA_G = 128
SHIFT_DIM = 3 * RWKV_DIM + LORA_W + LORA_A + LORA_G
ATT_COLS = ATT_DIM + 2 * KV_DIM
WINDOW = 128
CONV_WIDTH = 31
CONV_HALO = 32
D_FF = 4 * D_MODEL
RMS_EPS = 1e-6
LN_EPS = 1e-5
GN_EPS = 64e-5

LANES = 128
SUBLANES = 8
CHUNK = 64
SCAN_SLOTS = 4
VMEM_LIMIT = 56 * 1024 * 1024

F32 = jnp.float32
BF16 = jnp.bfloat16
NN = (((1,), (0,)), ((), ()))
NT = (((1,), (1,)), ((), ()))
TN = (((0,), (0,)), ((), ()))


def _params(*sem):
    return pltpu.CompilerParams(dimension_semantics=sem, vmem_limit_bytes=VMEM_LIMIT)


def _rms(x, g):
    return x * lax.rsqrt(jnp.mean(x * x, axis=-1, keepdims=True) + RMS_EPS) * g


LORA_DIM = LORA_W + LORA_A + LORA_G


def _in_proj_kernel(x_ref, g_ref, w_ref, mu_ref, wl_ref, bias_ref,
                    r_ref, k_ref, v_ref, lw_ref, a_ref, gate_ref, att_ref, prev_ref):
    tm = x_ref.shape[1]
    c1, c2, c3 = RWKV_DIM, 2 * RWKV_DIM, 3 * RWKV_DIM

    @pl.when(pl.program_id(1) == 0)
    def _():
        prev_ref[...] = jnp.zeros_like(prev_ref)

    def shift_lerp(p, cs):
        first = lax.broadcasted_iota(jnp.int32, p.shape, 0) == 0
        shifted = jnp.where(first, prev_ref[SUBLANES - 1:SUBLANES, cs], pltpu.roll(p, 1, axis=0))
        prev_ref[:, cs] = p[tm - SUBLANES:, :]
        return p + (shifted - p) * mu_ref[:, cs]

    h = _rms(x_ref[0], g_ref[...]).astype(BF16)
    z = jnp.dot(h, w_ref[:, :LORA_DIM], preferred_element_type=F32)
    z = shift_lerp(z, slice(0, LORA_DIM))
    lane = lax.broadcasted_iota(jnp.int32, z.shape, 1)
    act = jnp.where(lane < LORA_W, jnp.tanh(z),
                    jnp.where(lane < LORA_W + LORA_A, z, jax.nn.sigmoid(z)))
    lo = jnp.dot(act.astype(BF16), wl_ref[...], preferred_element_type=F32) + bias_ref[...]
    rkv = jnp.dot(h, w_ref[:, LORA_DIM:SHIFT_DIM], preferred_element_type=F32)
    att_ref[0] = jnp.dot(h, w_ref[:, SHIFT_DIM:], preferred_element_type=F32)
    rkv = shift_lerp(rkv, slice(LORA_DIM, SHIFT_DIM))
    r_ref[0] = rkv[:, :c1]
    k_ref[0] = rkv[:, c1:c2]
    v_ref[0] = rkv[:, c2:c3]
    wpre = lo[:, :c1]
    softplus = jnp.maximum(-wpre, 0.0) + jnp.log(1.0 + jnp.exp(-jnp.abs(wpre)))
    lw_ref[0] = -jnp.exp(-softplus - 0.5)
    a_ref[0] = jax.nn.sigmoid(lo[:, c1:c2])
    gate_ref[0] = lo[:, c2:]


def _in_proj(x3, g, w, mu, w_lora, bias, tm=512):
    b, s, _ = x3.shape
    rspec = pl.BlockSpec((1, tm, RWKV_DIM), lambda bi, i: (bi, i, 0))
    rshape = jax.ShapeDtypeStruct((b, s, RWKV_DIM), F32)
    const = lambda shape: pl.BlockSpec(shape, lambda bi, i: (0, 0))
    return pl.pallas_call(
        _in_proj_kernel,
        grid=(b, s // tm),
        in_specs=[
            pl.BlockSpec((1, tm, D_MODEL), lambda bi, i: (bi, i, 0)),
            const((1, D_MODEL)), const(w.shape), const((1, SHIFT_DIM)),
            const(w_lora.shape), const((1, 3 * RWKV_DIM)),
        ],
        out_specs=[rspec] * 6 + [pl.BlockSpec((1, tm, ATT_COLS), lambda bi, i: (bi, i, 0))],
        out_shape=[rshape] * 6 + [jax.ShapeDtypeStruct((b, s, ATT_COLS), F32)],
        scratch_shapes=[pltpu.VMEM((SUBLANES, SHIFT_DIM), F32)],
        compiler_params=_params("parallel", "arbitrary"),
        name="in_proj",
    )(x3, g, w, mu, w_lora, bias)


def _norm_glu_kernel(x_ref, g_ref, w_ref, b_ref, o_ref):
    h = _rms(x_ref[...], g_ref[...]).astype(BF16)
    acc = jnp.dot(h, w_ref[...], preferred_element_type=F32) + b_ref[...]
    o_ref[...] = acc[:, :D_MODEL] * jax.nn.sigmoid(acc[:, D_MODEL:])


def _norm_glu(x2, g, w, b, tm=1024):
    t = x2.shape[0]
    return pl.pallas_call(
        _norm_glu_kernel,
        grid=(t // tm,),
        in_specs=[
            pl.BlockSpec((tm, D_MODEL), lambda i: (i, 0)),
            pl.BlockSpec((1, D_MODEL), lambda i: (0, 0)),
            pl.BlockSpec((D_MODEL, 2 * D_MODEL), lambda i: (0, 0)),
            pl.BlockSpec((1, 2 * D_MODEL), lambda i: (0, 0)),
        ],
        out_specs=pl.BlockSpec((tm, D_MODEL), lambda i: (i, 0)),
        out_shape=jax.ShapeDtypeStruct((t, D_MODEL), F32),
        compiler_params=_params("parallel"),
        name="norm_glu",
    )(x2, g, w, b)


def _split3(x):
    hi = x.astype(BF16)
    r1 = x - hi.astype(F32)
    mid = r1.astype(BF16)
    lo = (r1 - mid.astype(F32)).astype(BF16)
    return hi, mid, lo


def _rwkv_scan_kernel(r_ref, k_ref, v_ref, lw_ref, a_ref, g_ref,
                      kk_ref, ka_ref, rk_ref, gg_ref, gb_ref, o_ref, state_ref,
                      atrt_ref, btkt_ref, bhkh_ref, vbf_ref, decay_ref, bonus_ref):
    c_len = CHUNK
    n2 = 2 * c_len
    nb, ts = r_ref.shape[0], r_ref.shape[1]
    n_pairs = RWKV_DIM // LANES
    lane = lax.broadcasted_iota(jnp.int32, (c_len, LANES), 1)
    lo = lane < HEAD_DIM
    lo2 = lax.broadcasted_iota(jnp.int32, (n2, LANES), 1) < HEAD_DIM
    top = lax.broadcasted_iota(jnp.int32, (n2, LANES), 0) < c_len
    own = lo2 == top

    def stack(x):
        return jnp.concatenate([jnp.where(lo, x, 0.0), jnp.where(lo, 0.0, x)], axis=0)

    ri = lax.broadcasted_iota(jnp.int32, (n2, n2), 0)
    ci = lax.broadcasted_iota(jnp.int32, (n2, n2), 1)
    strict = ri > ci
    incl = ri >= ci
    eye = (ri == ci).astype(F32)
    blk = [(ri >> l) == (ci >> l) for l in range(7)]
    tri = (lax.broadcasted_iota(jnp.int32, (c_len, c_len), 0)
           >= lax.broadcasted_iota(jnp.int32, (c_len, c_len), 1)).astype(BF16)

    @pl.when(pl.program_id(1) == 0)
    def _():
        state_ref[...] = jnp.zeros_like(state_ref)

    units = [(bb, j) for bb in range(nb) for j in range(n_pairs)]
    pairs = range(len(units))

    def mm(a, b, dims=NN):
        return lax.dot_general(a.astype(BF16), b.astype(BF16), dims, preferred_element_type=F32)

    def exact_zero(vregs):
        while len(vregs) > 1:
            vregs = [a + b for a, b in zip(vregs[::2], vregs[1::2])] + vregs[len(vregs) & ~1:]
        return (jnp.clip(vregs[0], -1.0, 1.0) * 0.0)[0:1]

    def prep(c, slot):
        sl = pl.ds(pl.multiple_of(c * c_len, c_len), c_len)
        lw_all, cum_all = [], []
        for bb in range(nb):
            lw_b = lw_ref[bb, sl, :]
            cum3 = jnp.dot(tri, jnp.concatenate(_split3(lw_b), axis=1), preferred_element_type=F32)
            lw_all.append(lw_b)
            cum_all.append(cum3[:, :RWKV_DIM] + cum3[:, RWKV_DIM:2 * RWKV_DIM] + cum3[:, 2 * RWKV_DIM:])

        folded = []
        for idx, (bb, j) in enumerate(units):
            cs = slice(j * LANES, (j + 1) * LANES)
            cum, lw = cum_all[bb][:, cs], lw_all[bb][:, cs]
            cum_last = cum[c_len - 1:c_len, :]
            r_s, k_s = stack(r_ref[bb, sl, cs]), stack(k_ref[bb, sl, cs])
            v_s, a_s = stack(v_ref[bb, sl, cs]), stack(a_ref[bb, sl, cs])
            kk = k_s * kk_ref[:, cs]
            kk = kk * lax.rsqrt(jnp.maximum(jnp.sum(kk * kk, axis=-1, keepdims=True), 1e-24))
            kmod = k_s * (1.0 + (a_s - 1.0) * ka_ref[:, cs])
            bvec = kk * a_s

            def both(x):
                return jnp.concatenate([x, x], axis=0)

            einv = both(jnp.exp(-cum))
            edec = both(jnp.exp(cum_last - cum))
            at_rt = jnp.concatenate([-kk * both(jnp.exp(cum - lw)), r_s * both(jnp.exp(cum))], axis=0)
            bh_kh = jnp.concatenate([bvec * edec, kmod * edec], axis=0)
            bonus = jnp.sum(r_s * kmod * rk_ref[:, cs], axis=-1, keepdims=True) * v_s
            atrt_ref[slot, idx] = at_rt.astype(BF16)
            btkt_ref[slot, idx] = jnp.concatenate([bvec * einv, kmod * einv], axis=0).astype(BF16)
            bhkh_ref[slot, idx] = bh_kh.astype(BF16)
            vbf_ref[slot, idx] = v_s.astype(BF16)
            decay_ref[slot, idx] = jnp.exp(cum_last)
            bonus_ref[slot, idx] = bonus
            for x in (at_rt, bh_kh, bonus):
                folded.extend(x[r:r + SUBLANES] for r in range(0, x.shape[0], SUBLANES))
        return exact_zero(folded)

    def inverse_part(slots):
        chains = [(slot, j) for slot in slots for j in pairs]
        gram = [mm(atrt_ref[slot, j], btkt_ref[slot, j], NT) for slot, j in chains]
        a_ab = [jnp.where(strict, g[:n2, :n2], 0.0).astype(BF16) for g in gram]
        a_ak = [jnp.where(strict, g[:n2, n2:], 0.0).astype(BF16) for g in gram]
        r_b = [jnp.where(incl, g[n2:, :n2], 0.0).astype(BF16) for g in gram]
        r_k = [jnp.where(incl, g[n2:, n2:], 0.0).astype(BF16) for g in gram]
        tinv = [(eye + jnp.where(strict & blk[1], g[:n2, :n2], 0.0)).astype(BF16) for g in gram]
        for lvl in range(2, 7):
            off = blk[lvl] & ~blk[lvl - 1]
            w = [jnp.where(off, mm(a, t), 0.0).astype(BF16) for a, t in zip(a_ab, tinv)]
            tinv = [t + mm(t, wi).astype(BF16) for t, wi in zip(tinv, w)]
        akv = [mm(a_ak[i], vbf_ref[slot, j]) for i, (slot, j) in enumerate(chains)]
        n = len(pairs)
        return {slot: (tinv[k * n:(k + 1) * n], akv[k * n:(k + 1) * n], r_b[k * n:(k + 1) * n],
                       r_k[k * n:(k + 1) * n]) for k, slot in enumerate(slots)}

    def state_part(c, slot, tinv, akv, r_b, r_k, zero=None):
        sl = pl.ds(pl.multiple_of(c * c_len, c_len), c_len)
        at_rt = [atrt_ref[slot, j] for j in pairs]
        v_bf = [vbf_ref[slot, j] for j in pairs]
        state = [state_ref[j] for j in pairs]
        if zero is not None:
            state[0] = state[0] + zero
        s_t = [mm(at_rt[j], state[j], NT) for j in pairs]
        u = [mm(tinv[j], s_t[j][:n2] + akv[j]) for j in pairs]
        u_bf = [u[j].astype(BF16) for j in pairs]
        for j in pairs:
            state_ref[j] = state[j] * decay_ref[slot, j] + mm(
                jnp.concatenate([u_bf[j], v_bf[j]], axis=0), bhkh_ref[slot, j], TN)
        y = [s_t[j][n2:] + mm(r_b[j], u_bf[j]) + mm(r_k[j], v_bf[j]) for j in pairs]

        inv_n = 1.0 / HEAD_DIM
        for idx, (bb, j) in enumerate(units):
            cs = slice(j * LANES, (j + 1) * LANES)
            mean = jnp.sum(y[idx], axis=-1, keepdims=True) * inv_n
            yc = jnp.where(own, y[idx] - mean, 0.0)
            var = jnp.sum(yc * yc, axis=-1, keepdims=True) * inv_n
            out = yc * lax.rsqrt(var + GN_EPS) * gg_ref[:, cs] + gb_ref[:, cs] + bonus_ref[slot, idx]
            out = jnp.where(lo, out[:c_len], out[c_len:])
            o_ref[bb, sl, cs] = (out * g_ref[bb, sl, cs]).astype(o_ref.dtype)

    n_chunks = ts // c_len
    prep(0, 0)
    prep(1, 1)

    def two_chunks(c, slots, c_next):
        zs = [prep(jnp.minimum(c_next + k, n_chunks - 1), (slot + 2) % SCAN_SLOTS)
              for k, slot in enumerate(slots)]
        inv = inverse_part(slots)
        for k, slot in enumerate(slots):
            state_part(c + k, slot, *inv[slot], zero=zs[0] + zs[1] if k == 0 else None)

    def body(i, carry):
        c0 = 4 * i
        two_chunks(c0, (0, 1), c0 + 2)
        two_chunks(c0 + 2, (2, 3), c0 + 4)
        return carry

    lax.fori_loop(0, n_chunks // 4, body, 0)


def _rwkv_scan(r, k, v, lw, a, g, kkp, kap, rkp, ggp, gbp, ts=512, nb=2):
    b, s, _ = r.shape
    n_units = nb * RWKV_DIM // LANES
    xspec = pl.BlockSpec((nb, ts, RWKV_DIM), lambda bi, i: (bi, i, 0))
    pspec = pl.BlockSpec((1, RWKV_DIM), lambda bi, i: (0, 0))
    return pl.pallas_call(
        _rwkv_scan_kernel,
        grid=(b // nb, s // ts),
        in_specs=[xspec] * 6 + [pspec] * 5,
        out_specs=xspec,
        out_shape=jax.ShapeDtypeStruct((b, s, RWKV_DIM), BF16),
        scratch_shapes=[
            pltpu.VMEM((n_units, LANES, LANES), F32),
            pltpu.VMEM((SCAN_SLOTS, n_units, 4 * CHUNK, LANES), BF16),
            pltpu.VMEM((SCAN_SLOTS, n_units, 4 * CHUNK, LANES), BF16),
            pltpu.VMEM((SCAN_SLOTS, n_units, 4 * CHUNK, LANES), BF16),
            pltpu.VMEM((SCAN_SLOTS, n_units, 2 * CHUNK, LANES), BF16),
            pltpu.VMEM((SCAN_SLOTS, n_units, 1, LANES), F32),
            pltpu.VMEM((SCAN_SLOTS, n_units, 2 * CHUNK, LANES), F32),
        ],
        compiler_params=_params("parallel", "arbitrary"),
        name="rwkv_scan",
    )(r, k, v, lw, a, g, kkp, kap, rkp, ggp, gbp)


def _swa_kernel(sink_ref, q_ref, kvc_ref, kvp_ref, o_ref):
    n = pl.program_id(1)
    blk = WINDOW
    n_sub = q_ref.shape[1] // blk
    kv_all = jnp.concatenate([kvp_ref[0], kvc_ref[0]], axis=0)
    kmat, vmat = kv_all[:, :KV_DIM], kv_all[:, KV_DIM:]
    lo_kv = lax.broadcasted_iota(jnp.int32, kmat.shape, 1) < HEAD_DIM
    k_sw = pltpu.roll(kmat, HEAD_DIM, axis=1)
    v_sw = pltpu.roll(vmat, HEAD_DIM, axis=1)
    k_dup = [jnp.where(lo_kv, kmat, k_sw).astype(BF16), jnp.where(lo_kv, k_sw, kmat).astype(BF16)]
    v_dup = [jnp.where(lo_kv, vmat, v_sw).astype(BF16), jnp.where(lo_kv, v_sw, vmat).astype(BF16)]

    lo_q = lax.broadcasted_iota(jnp.int32, (blk, LANES), 1) < HEAD_DIM
    rows = lax.broadcasted_iota(jnp.int32, (2 * blk, 2 * blk), 0)
    cols = lax.broadcasted_iota(jnp.int32, (2 * blk, 2 * blk), 1)
    rel = (rows & (blk - 1)) + blk - cols
    band = (rel >= 0) & (rel < WINDOW)
    has_prev = (cols >= blk) | (n > 0)
    relf = rel.astype(F32)
    first = lax.broadcasted_iota(jnp.int32, (2 * blk, 1), 0) < blk

    pairs = range(ATT_Q_HEADS // 2)
    subs = range(n_sub)
    scores = {}
    for sb in subs:
        for j in pairs:
            qp = q_ref[0, sb * blk:(sb + 1) * blk, j * LANES:(j + 1) * LANES] * (HEAD_DIM ** -0.5)
            qs = jnp.concatenate([jnp.where(lo_q, qp, 0.0), jnp.where(lo_q, 0.0, qp)], axis=0)
            keys = k_dup[j // 2][sb * blk:(sb + 2) * blk]
            scores[sb, j] = lax.dot_general(qs.astype(BF16), keys, NT, preferred_element_type=F32)
    probs = {}
    for j in pairs:
        slope = jnp.where(first, 2.0 ** -(2 * j + 1), 2.0 ** -(2 * j + 2))
        sink = jnp.where(first, sink_ref[2 * j], sink_ref[2 * j + 1])
        bias = jnp.where(band, -slope * relf, -jnp.inf)
        for sb in subs:
            s = scores[sb, j] + bias
            if sb == 0:
                s = jnp.where(has_prev, s, -jnp.inf)
            m = jnp.maximum(jnp.max(s, axis=-1, keepdims=True), sink)
            e = jnp.exp(s - m)
            inv = 1.0 / (jnp.sum(e, axis=-1, keepdims=True) + jnp.exp(sink - m))
            probs[sb, j] = (e * inv).astype(BF16)
    for sb in subs:
        for j in pairs:
            o = jnp.dot(probs[sb, j], v_dup[j // 2][sb * blk:(sb + 2) * blk], preferred_element_type=F32)
            o_ref[0, sb * blk:(sb + 1) * blk, j * LANES:(j + 1) * LANES] = (
                jnp.where(lo_q, o[:blk], o[blk:]).astype(o_ref.dtype))


def _swa(p_att3, sinks, n_sub=2):
    b, s, _ = p_att3.shape
    blk = WINDOW
    tq = n_sub * blk
    kv_col = ATT_DIM // (2 * KV_DIM)
    return pl.pallas_call(
        _swa_kernel,
        grid=(b, s // tq),
        in_specs=[
            pl.BlockSpec(memory_space=pltpu.SMEM),
            pl.BlockSpec((1, tq, ATT_DIM), lambda bi, n: (bi, n, 0)),
            pl.BlockSpec((1, tq, 2 * KV_DIM), lambda bi, n: (bi, n, kv_col)),
            pl.BlockSpec((1, blk, 2 * KV_DIM),
                         lambda bi, n: (bi, jnp.maximum(n * n_sub - 1, 0), kv_col)),
        ],
        out_specs=pl.BlockSpec((1, tq, ATT_DIM), lambda bi, n: (bi, n, 0)),
        out_shape=jax.ShapeDtypeStruct((b, s, ATT_DIM), BF16),
        compiler_params=_params("parallel", "parallel"),
        name="swa_attention",
    )(sinks, p_att3, p_att3, p_att3)


def _conv_kernel(u_ref, halo_ref, w_ref, b_ref, lg_ref, lb_ref, o_ref, sh_ref, acc_ref):
    i = pl.program_id(1)
    ts = u_ref.shape[1]
    rb = 128
    sh_ref[0, 0:CONV_HALO, :] = jnp.where(i > 0, halo_ref[0], 0.0)
    sh_ref[0, CONV_HALO:, :] = u_ref[0]
    n_sh = ts + CONV_HALO - SUBLANES
    for q in range(1, SUBLANES):
        for cb in range(D_MODEL // LANES):
            cs = slice(cb * LANES, (cb + 1) * LANES)
            sh_ref[q, 0:n_sh, cs] = sh_ref[0, q:q + n_sh, cs]
    off = CONV_HALO - (CONV_WIDTH - 1)
    for cb in range(D_MODEL // LANES):
        cs = slice(cb * LANES, (cb + 1) * LANES)

        def rows(rblk, carry, cs=cs, qs=(), init=True):
            r0 = pl.multiple_of(rblk * rb, rb)
            acc = (jnp.zeros((rb, LANES), F32) + b_ref[:, cs]) if init else acc_ref[pl.ds(r0, rb), cs]
            for q in qs:
                taps = [j for j in range(CONV_WIDTH) if (off + j) % SUBLANES == q]
                a_max = (off + taps[-1]) // SUBLANES
                x = sh_ref[q, pl.ds(r0, rb + a_max * SUBLANES), cs]
                part = None
                for j in taps:
                    a = (off + j) // SUBLANES
                    term = w_ref[j:j + 1, cs] * x[a * SUBLANES:a * SUBLANES + rb]
                    part = term if part is None else part + term
                acc = acc + part
            acc_ref[pl.ds(r0, rb), cs] = acc
            return carry

        half = SUBLANES // 2
        lax.fori_loop(0, ts // rb, functools.partial(rows, qs=range(half), init=True), 0)
        lax.fori_loop(0, ts // rb, functools.partial(rows, qs=range(half, SUBLANES), init=False), 0)
    y = acc_ref[...]
    mean = jnp.mean(y, axis=-1, keepdims=True)
    yc = y - mean
    var = jnp.mean(yc * yc, axis=-1, keepdims=True)
    yn = yc * lax.rsqrt(var + LN_EPS) * lg_ref[...] + lb_ref[...]
    o_ref[0] = (yn * jax.nn.sigmoid(yn)).astype(o_ref.dtype)


def _conv_ln_silu(u3, dw_w, dw_b, ln_g, ln_b, ts=512):
    b, s, _ = u3.shape
    vec = pl.BlockSpec((1, D_MODEL), lambda bi, i: (0, 0))
    return pl.pallas_call(
        _conv_kernel,
        grid=(b, s // ts),
        in_specs=[
            pl.BlockSpec((1, ts, D_MODEL), lambda bi, i: (bi, i, 0)),
            pl.BlockSpec((1, CONV_HALO, D_MODEL),
                         lambda bi, i: (bi, jnp.maximum(i * (ts // CONV_HALO) - 1, 0), 0)),
            pl.BlockSpec((CONV_WIDTH, D_MODEL), lambda bi, i: (0, 0)),
            vec, vec, vec,
        ],
        out_specs=pl.BlockSpec((1, ts, D_MODEL), lambda bi, i: (bi, i, 0)),
        out_shape=jax.ShapeDtypeStruct((b, s, D_MODEL), BF16),
        scratch_shapes=[pltpu.VMEM((SUBLANES, ts + CONV_HALO, D_MODEL), F32),
                        pltpu.VMEM((ts, D_MODEL), F32)],
        compiler_params=_params("parallel", "parallel"),
        name="conv_ln_silu",
    )(u3, u3, dw_w, dw_b, ln_g, ln_b)


def _proj_mlp_kernel(*refs, n_proj, final_norm):
    x_ref = refs[0]
    y_refs = refs[1:1 + n_proj]
    wp_refs = refs[1 + n_proj:1 + 2 * n_proj]
    pb_ref, g_ref, w1_ref, w2_ref, gf_ref, o_ref, x1_ref, h_ref, acc_ref = refs[1 + 2 * n_proj:]
    j = pl.program_id(1)

    @pl.when(j == 0)
    def _():
        x1 = x_ref[...] + pb_ref[...]
        for y_ref, wp_ref in zip(y_refs, wp_refs):
            x1 = x1 + jnp.dot(y_ref[...], wp_ref[...], preferred_element_type=F32)
        x1_ref[...] = x1
        h_ref[...] = _rms(x1, g_ref[...]).astype(BF16)
        acc_ref[...] = jnp.zeros_like(acc_ref)

    hid = jnp.dot(h_ref[...], w1_ref[...], preferred_element_type=F32)
    hid = jnp.square(jnp.maximum(hid, 0.0)).astype(BF16)
    acc_ref[...] += jnp.dot(hid, w2_ref[...], preferred_element_type=F32)

    @pl.when(j == pl.num_programs(1) - 1)
    def _():
        out = x1_ref[...] + acc_ref[...]
        if final_norm:
            out = _rms(out, gf_ref[...])
        o_ref[...] = out


def _proj_mlp(x2, ys, wps, pb, g, w1, w2, gf, final_norm, tm=1024, tf=1024):
    t = x2.shape[0]
    n_proj = len(ys)
    vec = pl.BlockSpec((1, D_MODEL), lambda i, j: (0, 0))
    in_specs = [pl.BlockSpec((tm, D_MODEL), lambda i, j: (i, 0))]
    in_specs += [pl.BlockSpec((tm, y.shape[1]), lambda i, j: (i, 0)) for y in ys]
    in_specs += [pl.BlockSpec(w.shape, lambda i, j: (0, 0)) for w in wps]
    in_specs += [vec, vec,
                 pl.BlockSpec((D_MODEL, tf), lambda i, j: (0, j)),
                 pl.BlockSpec((tf, D_MODEL), lambda i, j: (j, 0)),
                 vec]
    return pl.pallas_call(
        functools.partial(_proj_mlp_kernel, n_proj=n_proj, final_norm=final_norm),
        grid=(t // tm, D_FF // tf),
        in_specs=in_specs,
        out_specs=pl.BlockSpec((tm, D_MODEL), lambda i, j: (i, 0)),
        out_shape=jax.ShapeDtypeStruct((t, D_MODEL), F32),
        scratch_shapes=[pltpu.VMEM((tm, D_MODEL), F32),
                        pltpu.VMEM((tm, D_MODEL), BF16),
                        pltpu.VMEM((tm, D_MODEL), F32)],
        compiler_params=_params("parallel", "arbitrary"),
        name="proj_mlp",
    )(x2, *ys, *wps, pb, g, w1, w2, gf)


def _row(v):
    return v.reshape(1, -1).astype(F32)


def _hybrid_front(x3, g_mix, w_in, mu, w0, w_up, a0, a_up, g_up):
    c3 = 3 * RWKV_DIM
    w = jnp.concatenate([w_in[:, c3:SHIFT_DIM], w_in[:, :c3], w_in[:, SHIFT_DIM:]], axis=1).astype(BF16)
    mu_p = jnp.concatenate([mu[c3:], mu[:c3]]).reshape(1, -1).astype(F32)
    w_lora = jnp.zeros((LORA_DIM, c3), F32)
    w_lora = w_lora.at[:LORA_W, :RWKV_DIM].set(w_up)
    w_lora = w_lora.at[LORA_W:LORA_W + LORA_A, RWKV_DIM:2 * RWKV_DIM].set(a_up)
    w_lora = w_lora.at[LORA_W + LORA_A:, 2 * RWKV_DIM:].set(g_up)
    bias = jnp.concatenate([w0, a0, jnp.zeros((RWKV_DIM,), F32)]).reshape(1, -1)
    return _in_proj(x3, g_mix, w, mu_p, w_lora.astype(BF16), bias)


def kernel(x, norm_mix_g, norm_ffn_g, final_norm_g, hy_w_in, hy_mu, hy_w0, hy_w_up, hy_a0, hy_a_up, hy_g_up, hy_k_k, hy_k_a, hy_r_k, hy_gn_g, hy_gn_b, hy_sinks, hy_w_out, cv_pw1_w, cv_pw1_b, cv_dw_w, cv_dw_b, cv_ln_g, cv_ln_b, cv_pw2_w, cv_pw2_b, mlp_w1, mlp_w2):
    bsz, seq, d = x.shape
    depth = norm_mix_g.shape[0]
    t = bsz * seq
    x2 = x.reshape(t, d)
    zero_row = jnp.zeros((1, D_MODEL), F32)
    gf = _row(final_norm_g)

    for layer in range(depth):
        i = layer // 2
        g_mix = _row(norm_mix_g[layer])
        if layer % 2 == 0:
            r, k, v, lw, a, gate, p_att = _hybrid_front(
                x2.reshape(bsz, seq, d), g_mix, hy_w_in[i], hy_mu[i], hy_w0[i], hy_w_up[i],
                hy_a0[i], hy_a_up[i], hy_g_up[i])
            y_rwkv = _rwkv_scan(r, k, v, lw, a, gate, _row(hy_k_k[i]), _row(hy_k_a[i]),
                                _row(hy_r_k[i]), _row(hy_gn_g[i]), _row(hy_gn_b[i]))
            y_att = _swa(p_att, hy_sinks[i].astype(F32))
            w_out = hy_w_out[i].astype(BF16)
            ys = [y_rwkv.reshape(t, RWKV_DIM), y_att.reshape(t, ATT_DIM)]
            wps = [w_out[:RWKV_DIM], w_out[RWKV_DIM:]]
            pb = zero_row
        else:
            u = _norm_glu(x2, g_mix, cv_pw1_w[i].astype(BF16), _row(cv_pw1_b[i]))
            u = _conv_ln_silu(u.reshape(bsz, seq, D_MODEL), cv_dw_w[i].astype(F32),
                              _row(cv_dw_b[i]), _row(cv_ln_g[i]), _row(cv_ln_b[i]))
            ys = [u.reshape(t, D_MODEL)]
            wps = [cv_pw2_w[i].astype(BF16)]
            pb = _row(cv_pw2_b[i])
        x2 = _proj_mlp(x2, ys, wps, pb, _row(norm_ffn_g[layer]),
                       mlp_w1[layer].astype(BF16), mlp_w2[layer].astype(BF16), gf,
                       final_norm=(layer == depth - 1))
    return x2.reshape(bsz, seq, d)
```

```python
import functools

import jax
import jax.numpy as jnp
from jax import lax
from jax.experimental import pallas as pl
from jax.experimental.pallas import tpu as pltpu

D_MODEL = 1024
HEAD_DIM = 64
RWKV_DIM = 512
ATT_Q_HEADS = 8
ATT_DIM = 512
KV_DIM = 128
LORA_W = 64
LORA_A = 64
LORA_G = 128
SHIFT_DIM = 3 * RWKV_DIM + LORA_W + LORA_A + LORA_G
ATT_COLS = ATT_DIM + 2 * KV_DIM
WINDOW = 128
CONV_WIDTH = 31
CONV_HALO = 32
D_FF = 4 * D_MODEL
RMS_EPS = 1e-6
LN_EPS = 1e-5
GN_EPS = 64e-5

LANES = 128
SUBLANES = 8
CHUNK = 64
SCAN_SLOTS = 4
VMEM_LIMIT = 56 * 1024 * 1024

F32 = jnp.float32
BF16 = jnp.bfloat16
NN = (((1,), (0,)), ((), ()))
NT = (((1,), (1,)), ((), ()))
TN = (((0,), (0,)), ((), ()))


def _params(*sem):
    return pltpu.CompilerParams(dimension_semantics=sem, vmem_limit_bytes=VMEM_LIMIT)


def _rms(x, g):
    return x * lax.rsqrt(jnp.mean(x * x, axis=-1, keepdims=True) + RMS_EPS) * g


LORA_DIM = LORA_W + LORA_A + LORA_G


def _in_proj_kernel(x_ref, g_ref, w_ref, mu_ref, wl_ref, bias_ref,
                    r_ref, k_ref, v_ref, lw_ref, a_ref, gate_ref, att_ref, prev_ref):
    tm = x_ref.shape[1]
    c1, c2, c3 = RWKV_DIM, 2 * RWKV_DIM, 3 * RWKV_DIM

    @pl.when(pl.program_id(1) == 0)
    def _():
        prev_ref[...] = jnp.zeros_like(prev_ref)

    def shift_lerp(p, cs):
        first = lax.broadcasted_iota(jnp.int32, p.shape, 0) == 0
        shifted = jnp.where(first, prev_ref[SUBLANES - 1:SUBLANES, cs], pltpu.roll(p, 1, axis=0))
        prev_ref[:, cs] = p[tm - SUBLANES:, :]
        return p + (shifted - p) * mu_ref[:, cs]

    h = _rms(x_ref[0], g_ref[...]).astype(BF16)
    z = jnp.dot(h, w_ref[:, :LORA_DIM], preferred_element_type=F32)
    z = shift_lerp(z, slice(0, LORA_DIM))
    lane = lax.broadcasted_iota(jnp.int32, z.shape, 1)
    act = jnp.where(lane < LORA_W, jnp.tanh(z),
                    jnp.where(lane < LORA_W + LORA_A, z, jax.nn.sigmoid(z)))
    lo = jnp.dot(act.astype(BF16), wl_ref[...], preferred_element_type=F32) + bias_ref[...]
    rkv = jnp.dot(h, w_ref[:, LORA_DIM:SHIFT_DIM], preferred_element_type=F32)
    att_ref[0] = jnp.dot(h, w_ref[:, SHIFT_DIM:], preferred_element_type=F32)
    rkv = shift_lerp(rkv, slice(LORA_DIM, SHIFT_DIM))
    r_ref[0] = rkv[:, :c1]
    k_ref[0] = rkv[:, c1:c2]
    v_ref[0] = rkv[:, c2:c3]
    wpre = lo[:, :c1]
    softplus = jnp.maximum(-wpre, 0.0) + jnp.log(1.0 + jnp.exp(-jnp.abs(wpre)))
    lw_ref[0] = -jnp.exp(-softplus - 0.5)
    a_ref[0] = jax.nn.sigmoid(lo[:, c1:c2])
    gate_ref[0] = lo[:, c2:]


def _in_proj(x3, g, w, mu, w_lora, bias, tm=512):
    b, s, _ = x3.shape
    rspec = pl.BlockSpec((1, tm, RWKV_DIM), lambda bi, i: (bi, i, 0))
    rshape = jax.ShapeDtypeStruct((b, s, RWKV_DIM), F32)
    const = lambda shape: pl.BlockSpec(shape, lambda bi, i: (0, 0))
    return pl.pallas_call(
        _in_proj_kernel,
        grid=(b, s // tm),
        in_specs=[
            pl.BlockSpec((1, tm, D_MODEL), lambda bi, i: (bi, i, 0)),
            const((1, D_MODEL)), const(w.shape), const((1, SHIFT_DIM)),
            const(w_lora.shape), const((1, 3 * RWKV_DIM)),
        ],
        out_specs=[rspec] * 6 + [pl.BlockSpec((1, tm, ATT_COLS), lambda bi, i: (bi, i, 0))],
        out_shape=[rshape] * 6 + [jax.ShapeDtypeStruct((b, s, ATT_COLS), F32)],
        scratch_shapes=[pltpu.VMEM((SUBLANES, SHIFT_DIM), F32)],
        compiler_params=_params("parallel", "arbitrary"),
        name="in_proj",
    )(x3, g, w, mu, w_lora, bias)


def _norm_glu_kernel(x_ref, g_ref, w_ref, b_ref, o_ref):
    h = _rms(x_ref[...], g_ref[...]).astype(BF16)
    acc = jnp.dot(h, w_ref[...], preferred_element_type=F32) + b_ref[...]
    o_ref[...] = acc[:, :D_MODEL] * jax.nn.sigmoid(acc[:, D_MODEL:])


def _norm_glu(x2, g, w, b, tm=1024):
    t = x2.shape[0]
    return pl.pallas_call(
        _norm_glu_kernel,
        grid=(t // tm,),
        in_specs=[
            pl.BlockSpec((tm, D_MODEL), lambda i: (i, 0)),
            pl.BlockSpec((1, D_MODEL), lambda i: (0, 0)),
            pl.BlockSpec((D_MODEL, 2 * D_MODEL), lambda i: (0, 0)),
            pl.BlockSpec((1, 2 * D_MODEL), lambda i: (0, 0)),
        ],
        out_specs=pl.BlockSpec((tm, D_MODEL), lambda i: (i, 0)),
        out_shape=jax.ShapeDtypeStruct((t, D_MODEL), F32),
        compiler_params=_params("parallel"),
        name="norm_glu",
    )(x2, g, w, b)


def _split3(x):
    hi = x.astype(BF16)
    r1 = x - hi.astype(F32)
    mid = r1.astype(BF16)
    lo = (r1 - mid.astype(F32)).astype(BF16)
    return hi, mid, lo


def _rwkv_scan_kernel(r_ref, k_ref, v_ref, lw_ref, a_ref, g_ref,
                      kk_ref, ka_ref, rk_ref, gg_ref, gb_ref, o_ref, state_ref,
                      atrt_ref, btkt_ref, bhkh_ref, vbf_ref, decay_ref, bonus_ref):
    c_len = CHUNK
    n2 = 2 * c_len
    nb, ts = r_ref.shape[0], r_ref.shape[1]
    n_pairs = RWKV_DIM // LANES
    lane = lax.broadcasted_iota(jnp.int32, (c_len, LANES), 1)
    lo = lane < HEAD_DIM
    lo2 = lax.broadcasted_iota(jnp.int32, (n2, LANES), 1) < HEAD_DIM
    top = lax.broadcasted_iota(jnp.int32, (n2, LANES), 0) < c_len
    own = lo2 == top

    def stack(x):
        return jnp.concatenate([jnp.where(lo, x, 0.0), jnp.where(lo, 0.0, x)], axis=0)

    ri = lax.broadcasted_iota(jnp.int32, (n2, n2), 0)
    ci = lax.broadcasted_iota(jnp.int32, (n2, n2), 1)
    strict = ri > ci
    incl = ri >= ci
    eye = (ri == ci).astype(F32)
    blk = [(ri >> l) == (ci >> l) for l in range(7)]
    tri = (lax.broadcasted_iota(jnp.int32, (c_len, c_len), 0)
           >= lax.broadcasted_iota(jnp.int32, (c_len, c_len), 1)).astype(BF16)

    @pl.when(pl.program_id(1) == 0)
    def _():
        state_ref[...] = jnp.zeros_like(state_ref)

    units = [(bb, j) for bb in range(nb) for j in range(n_pairs)]
    pairs = range(len(units))

    def mm(a, b, dims=NN):
        return lax.dot_general(a.astype(BF16), b.astype(BF16), dims, preferred_element_type=F32)

    def exact_zero(vregs):
        while len(vregs) > 1:
            vregs = [a + b for a, b in zip(vregs[::2], vregs[1::2])] + vregs[len(vregs) & ~1:]
        return (jnp.clip(vregs[0], -1.0, 1.0) * 0.0)[0:1]

    def prep(c, slot):
        sl = pl.ds(pl.multiple_of(c * c_len, c_len), c_len)
        lw_all, cum_all = [], []
        for bb in range(nb):
            lw_b = lw_ref[bb, sl, :]
            cum3 = jnp.dot(tri, jnp.concatenate(_split3(lw_b), axis=1), preferred_element_type=F32)
            lw_all.append(lw_b)
            cum_all.append(cum3[:, :RWKV_DIM] + cum3[:, RWKV_DIM:2 * RWKV_DIM] + cum3[:, 2 * RWKV_DIM:])

        folded = []
        for idx, (bb, j) in enumerate(units):
            cs = slice(j * LANES, (j + 1) * LANES)
            cum, lw = cum_all[bb][:, cs], lw_all[bb][:, cs]
            cum_last = cum[c_len - 1:c_len, :]
            r_s, k_s = stack(r_ref[bb, sl, cs]), stack(k_ref[bb, sl, cs])
            v_s, a_s = stack(v_ref[bb, sl, cs]), stack(a_ref[bb, sl, cs])
            kk = k_s * kk_ref[:, cs]
            kk = kk * lax.rsqrt(jnp.maximum(jnp.sum(kk * kk, axis=-1, keepdims=True), 1e-24))
            kmod = k_s * (1.0 + (a_s - 1.0) * ka_ref[:, cs])
            bvec = kk * a_s

            def both(x):
                return jnp.concatenate([x, x], axis=0)

            einv = both(jnp.exp(-cum))
            edec = both(jnp.exp(cum_last - cum))
            at_rt = jnp.concatenate([-kk * both(jnp.exp(cum - lw)), r_s * both(jnp.exp(cum))], axis=0)
            bh_kh = jnp.concatenate([bvec * edec, kmod * edec], axis=0)
            bonus = jnp.sum(r_s * kmod * rk_ref[:, cs], axis=-1, keepdims=True) * v_s
            atrt_ref[slot, idx] = at_rt.astype(BF16)
            btkt_ref[slot, idx] = jnp.concatenate([bvec * einv, kmod * einv], axis=0).astype(BF16)
            bhkh_ref[slot, idx] = bh_kh.astype(BF16)
            vbf_ref[slot, idx] = v_s.astype(BF16)
            decay_ref[slot, idx] = jnp.exp(cum_last)
            bonus_ref[slot, idx] = bonus
            for x in (at_rt, bh_kh, bonus):
                folded.extend(x[r:r + SUBLANES] for r in range(0, x.shape[0], SUBLANES))
        return exact_zero(folded)

    def inverse_part(slots):
        chains = [(slot, j) for slot in slots for j in pairs]
        gram = [mm(atrt_ref[slot, j], btkt_ref[slot, j], NT) for slot, j in chains]
        a_ab = [jnp.where(strict, g[:n2, :n2], 0.0).astype(BF16) for g in gram]
        a_ak = [jnp.where(strict, g[:n2, n2:], 0.0).astype(BF16) for g in gram]
        r_b = [jnp.where(incl, g[n2:, :n2], 0.0).astype(BF16) for g in gram]
        r_k = [jnp.where(incl, g[n2:, n2:], 0.0).astype(BF16) for g in gram]
        tinv = [(eye + jnp.where(strict & blk[1], g[:n2, :n2], 0.0)).astype(BF16) for g in gram]
        for lvl in range(2, 7):
            off = blk[lvl] & ~blk[lvl - 1]
            w = [jnp.where(off, mm(a, t), 0.0).astype(BF16) for a, t in zip(a_ab, tinv)]
            tinv = [t + mm(t, wi).astype(BF16) for t, wi in zip(tinv, w)]
        akv = [mm(a_ak[i], vbf_ref[slot, j]) for i, (slot, j) in enumerate(chains)]
        n = len(pairs)
        return {slot: (tinv[k * n:(k + 1) * n], akv[k * n:(k + 1) * n], r_b[k * n:(k + 1) * n],
                       r_k[k * n:(k + 1) * n]) for k, slot in enumerate(slots)}

    def state_part(c, slot, tinv, akv, r_b, r_k, zero=None):
        sl = pl.ds(pl.multiple_of(c * c_len, c_len), c_len)
        at_rt = [atrt_ref[slot, j] for j in pairs]
        v_bf = [vbf_ref[slot, j] for j in pairs]
        state = [state_ref[j] for j in pairs]
        if zero is not None:
            state[0] = state[0] + zero
        s_t = [mm(at_rt[j], state[j], NT) for j in pairs]
        u = [mm(tinv[j], s_t[j][:n2] + akv[j]) for j in pairs]
        u_bf = [u[j].astype(BF16) for j in pairs]
        for j in pairs:
            state_ref[j] = state[j] * decay_ref[slot, j] + mm(
                jnp.concatenate([u_bf[j], v_bf[j]], axis=0), bhkh_ref[slot, j], TN)
        y = [s_t[j][n2:] + mm(r_b[j], u_bf[j]) + mm(r_k[j], v_bf[j]) for j in pairs]

        inv_n = 1.0 / HEAD_DIM
        for idx, (bb, j) in enumerate(units):
            cs = slice(j * LANES, (j + 1) * LANES)
            mean = jnp.sum(y[idx], axis=-1, keepdims=True) * inv_n
            yc = jnp.where(own, y[idx] - mean, 0.0)
            var = jnp.sum(yc * yc, axis=-1, keepdims=True) * inv_n
            out = yc * lax.rsqrt(var + GN_EPS) * gg_ref[:, cs] + gb_ref[:, cs] + bonus_ref[slot, idx]
            out = jnp.where(lo, out[:c_len], out[c_len:])
            o_ref[bb, sl, cs] = (out * g_ref[bb, sl, cs]).astype(o_ref.dtype)

    n_chunks = ts // c_len
    prep(0, 0)
    prep(1, 1)

    def two_chunks(c, slots, c_next):
        zs = [prep(jnp.minimum(c_next + k, n_chunks - 1), (slot + 2) % SCAN_SLOTS)
              for k, slot in enumerate(slots)]
        inv = inverse_part(slots)
        for k, slot in enumerate(slots):
            state_part(c + k, slot, *inv[slot], zero=zs[0] + zs[1] if k == 0 else None)

    def body(i, carry):
        c0 = 4 * i
        two_chunks(c0, (0, 1), c0 + 2)
        two_chunks(c0 + 2, (2, 3), c0 + 4)
        return carry

    lax.fori_loop(0, n_chunks // 4, body, 0)


def _rwkv_scan(r, k, v, lw, a, g, kkp, kap, rkp, ggp, gbp, ts=512, nb=2):
    b, s, _ = r.shape
    n_units = nb * RWKV_DIM // LANES
    xspec = pl.BlockSpec((nb, ts, RWKV_DIM), lambda bi, i: (bi, i, 0))
    pspec = pl.BlockSpec((1, RWKV_DIM), lambda bi, i: (0, 0))
    return pl.pallas_call(
        _rwkv_scan_kernel,
        grid=(b // nb, s // ts),
        in_specs=[xspec] * 6 + [pspec] * 5,
        out_specs=xspec,
        out_shape=jax.ShapeDtypeStruct((b, s, RWKV_DIM), BF16),
        scratch_shapes=[
            pltpu.VMEM((n_units, LANES, LANES), F32),
            pltpu.VMEM((SCAN_SLOTS, n_units, 4 * CHUNK, LANES), BF16),
            pltpu.VMEM((SCAN_SLOTS, n_units, 4 * CHUNK, LANES), BF16),
            pltpu.VMEM((SCAN_SLOTS, n_units, 4 * CHUNK, LANES), BF16),
            pltpu.VMEM((SCAN_SLOTS, n_units, 2 * CHUNK, LANES), BF16),
            pltpu.VMEM((SCAN_SLOTS, n_units, 1, LANES), F32),
            pltpu.VMEM((SCAN_SLOTS, n_units, 2 * CHUNK, LANES), F32),
        ],
        compiler_params=_params("parallel", "arbitrary"),
        name="rwkv_scan",
    )(r, k, v, lw, a, g, kkp, kap, rkp, ggp, gbp)


def _swa_kernel(sink_ref, q_ref, kvc_ref, kvp_ref, o_ref):
    n = pl.program_id(1)
    blk = WINDOW
    n_sub = q_ref.shape[1] // blk
    kv_all = jnp.concatenate([kvp_ref[0], kvc_ref[0]], axis=0)
    kmat, vmat = kv_all[:, :KV_DIM], kv_all[:, KV_DIM:]
    lo_kv = lax.broadcasted_iota(jnp.int32, kmat.shape, 1) < HEAD_DIM
    k_sw = pltpu.roll(kmat, HEAD_DIM, axis=1)
    v_sw = pltpu.roll(vmat, HEAD_DIM, axis=1)
    k_dup = [jnp.where(lo_kv, kmat, k_sw).astype(BF16), jnp.where(lo_kv, k_sw, kmat).astype(BF16)]
    v_dup = [jnp.where(lo_kv, vmat, v_sw).astype(BF16), jnp.where(lo_kv, v_sw, vmat).astype(BF16)]

    lo_q = lax.broadcasted_iota(jnp.int32, (blk, LANES), 1) < HEAD_DIM
    rows = lax.broadcasted_iota(jnp.int32, (2 * blk, 2 * blk), 0)
    cols = lax.broadcasted_iota(jnp.int32, (2 * blk, 2 * blk), 1)
    rel = (rows & (blk - 1)) + blk - cols
    band = (rel >= 0) & (rel < WINDOW)
    has_prev = (cols >= blk) | (n > 0)
    relf = rel.astype(F32)
    first = lax.broadcasted_iota(jnp.int32, (2 * blk, 1), 0) < blk

    pairs = range(ATT_Q_HEADS // 2)
    subs = range(n_sub)
    scores = {}
    for sb in subs:
        for j in pairs:
            qp = q_ref[0, sb * blk:(sb + 1) * blk, j * LANES:(j + 1) * LANES] * (HEAD_DIM ** -0.5)
            qs = jnp.concatenate([jnp.where(lo_q, qp, 0.0), jnp.where(lo_q, 0.0, qp)], axis=0)
            keys = k_dup[j // 2][sb * blk:(sb + 2) * blk]
            scores[sb, j] = lax.dot_general(qs.astype(BF16), keys, NT, preferred_element_type=F32)
    probs = {}
    for j in pairs:
        slope = jnp.where(first, 2.0 ** -(2 * j + 1), 2.0 ** -(2 * j + 2))
        sink = jnp.where(first, sink_ref[2 * j], sink_ref[2 * j + 1])
        bias = jnp.where(band, -slope * relf, -jnp.inf)
        for sb in subs:
            s = scores[sb, j] + bias
            if sb == 0:
                s = jnp.where(has_prev, s, -jnp.inf)
            m = jnp.maximum(jnp.max(s, axis=-1, keepdims=True), sink)
            e = jnp.exp(s - m)
            inv = 1.0 / (jnp.sum(e, axis=-1, keepdims=True) + jnp.exp(sink - m))
            probs[sb, j] = (e * inv).astype(BF16)
    for sb in subs:
        for j in pairs:
            o = jnp.dot(probs[sb, j], v_dup[j // 2][sb * blk:(sb + 2) * blk], preferred_element_type=F32)
            o_ref[0, sb * blk:(sb + 1) * blk, j * LANES:(j + 1) * LANES] = (
                jnp.where(lo_q, o[:blk], o[blk:]).astype(o_ref.dtype))


def _swa(p_att3, sinks, n_sub=4):
    b, s, _ = p_att3.shape
    blk = WINDOW
    tq = n_sub * blk
    kv_col = ATT_DIM // (2 * KV_DIM)
    return pl.pallas_call(
        _swa_kernel,
        grid=(b, s // tq),
        in_specs=[
            pl.BlockSpec(memory_space=pltpu.SMEM),
            pl.BlockSpec((1, tq, ATT_DIM), lambda bi, n: (bi, n, 0)),
            pl.BlockSpec((1, tq, 2 * KV_DIM), lambda bi, n: (bi, n, kv_col)),
            pl.BlockSpec((1, blk, 2 * KV_DIM),
                         lambda bi, n: (bi, jnp.maximum(n * n_sub - 1, 0), kv_col)),
        ],
        out_specs=pl.BlockSpec((1, tq, ATT_DIM), lambda bi, n: (bi, n, 0)),
        out_shape=jax.ShapeDtypeStruct((b, s, ATT_DIM), BF16),
        compiler_params=_params("parallel", "parallel"),
        name="swa_attention",
    )(sinks, p_att3, p_att3, p_att3)


def _conv_kernel(u_ref, halo_ref, w_ref, b_ref, lg_ref, lb_ref, o_ref, sh_ref, acc_ref):
    i = pl.program_id(1)
    ts = u_ref.shape[1]
    rb = 128
    sh_ref[0, 0:CONV_HALO, :] = jnp.where(i > 0, halo_ref[0], 0.0)
    sh_ref[0, CONV_HALO:, :] = u_ref[0]
    n_sh = ts + CONV_HALO - SUBLANES
    for q in range(1, SUBLANES):
        for cb in range(D_MODEL // LANES):
            cs = slice(cb * LANES, (cb + 1) * LANES)
            sh_ref[q, 0:n_sh, cs] = sh_ref[0, q:q + n_sh, cs]
    off = CONV_HALO - (CONV_WIDTH - 1)
    for cb in range(D_MODEL // LANES):
        cs = slice(cb * LANES, (cb + 1) * LANES)

        def rows(rblk, carry, cs=cs, qs=(), init=True):
            r0 = pl.multiple_of(rblk * rb, rb)
            acc = (jnp.zeros((rb, LANES), F32) + b_ref[:, cs]) if init else acc_ref[pl.ds(r0, rb), cs]
            for q in qs:
                taps = [j for j in range(CONV_WIDTH) if (off + j) % SUBLANES == q]
                a_max = (off + taps[-1]) // SUBLANES
                x = sh_ref[q, pl.ds(r0, rb + a_max * SUBLANES), cs]
                part = None
                for j in taps:
                    a = (off + j) // SUBLANES
                    term = w_ref[j:j + 1, cs] * x[a * SUBLANES:a * SUBLANES + rb]
                    part = term if part is None else part + term
                acc = acc + part
            acc_ref[pl.ds(r0, rb), cs] = acc
            return carry

        half = SUBLANES // 2
        lax.fori_loop(0, ts // rb, functools.partial(rows, qs=range(half), init=True), 0)
        lax.fori_loop(0, ts // rb, functools.partial(rows, qs=range(half, SUBLANES), init=False), 0)
    y = acc_ref[...]
    mean = jnp.mean(y, axis=-1, keepdims=True)
    yc = y - mean
    var = jnp.mean(yc * yc, axis=-1, keepdims=True)
    yn = yc * lax.rsqrt(var + LN_EPS) * lg_ref[...] + lb_ref[...]
    o_ref[0] = (yn * jax.nn.sigmoid(yn)).astype(o_ref.dtype)


def _conv_ln_silu(u3, dw_w, dw_b, ln_g, ln_b, ts=512):
    b, s, _ = u3.shape
    vec = pl.BlockSpec((1, D_MODEL), lambda bi, i: (0, 0))
    return pl.pallas_call(
        _conv_kernel,
        grid=(b, s // ts),
        in_specs=[
            pl.BlockSpec((1, ts, D_MODEL), lambda bi, i: (bi, i, 0)),
            pl.BlockSpec((1, CONV_HALO, D_MODEL),
                         lambda bi, i: (bi, jnp.maximum(i * (ts // CONV_HALO) - 1, 0), 0)),
            pl.BlockSpec((CONV_WIDTH, D_MODEL), lambda bi, i: (0, 0)),
            vec, vec, vec,
        ],
        out_specs=pl.BlockSpec((1, ts, D_MODEL), lambda bi, i: (bi, i, 0)),
        out_shape=jax.ShapeDtypeStruct((b, s, D_MODEL), BF16),
        scratch_shapes=[pltpu.VMEM((SUBLANES, ts + CONV_HALO, D_MODEL), F32),
                        pltpu.VMEM((ts, D_MODEL), F32)],
        compiler_params=_params("parallel", "parallel"),
        name="conv_ln_silu",
    )(u3, u3, dw_w, dw_b, ln_g, ln_b)


def _proj_mlp_kernel(*refs, n_proj, final_norm):
    x_ref = refs[0]
    y_refs = refs[1:1 + n_proj]
    wp_refs = refs[1 + n_proj:1 + 2 * n_proj]
    pb_ref, g_ref, w1_ref, w2_ref, gf_ref, o_ref, x1_ref, h_ref, acc_ref = refs[1 + 2 * n_proj:]
    j = pl.program_id(1)

    @pl.when(j == 0)
    def _():
        x1 = x_ref[...] + pb_ref[...]
        for y_ref, wp_ref in zip(y_refs, wp_refs):
            x1 = x1 + jnp.dot(y_ref[...], wp_ref[...], preferred_element_type=F32)
        x1_ref[...] = x1
        h_ref[...] = _rms(x1, g_ref[...]).astype(BF16)
        acc_ref[...] = jnp.zeros_like(acc_ref)

    hid = jnp.dot(h_ref[...], w1_ref[...], preferred_element_type=F32)
    hid = jnp.square(jnp.maximum(hid, 0.0)).astype(BF16)
    acc_ref[...] += jnp.dot(hid, w2_ref[...], preferred_element_type=F32)

    @pl.when(j == pl.num_programs(1) - 1)
    def _():
        out = x1_ref[...] + acc_ref[...]
        if final_norm:
            out = _rms(out, gf_ref[...])
        o_ref[...] = out


def _proj_mlp(x2, ys, wps, pb, g, w1_all, w2_all, layer, gf, final_norm, tm=1024, tf=1024):
    t = x2.shape[0]
    n_proj = len(ys)
    vec = pl.BlockSpec((1, D_MODEL), lambda i, j: (0, 0))
    in_specs = [pl.BlockSpec((tm, D_MODEL), lambda i, j: (i, 0))]
    in_specs += [pl.BlockSpec((tm, y.shape[1]), lambda i, j: (i, 0)) for y in ys]
    in_specs += [pl.BlockSpec(w.shape, lambda i, j: (0, 0)) for w in wps]
    in_specs += [vec, vec,
                 pl.BlockSpec((None, D_MODEL, tf), lambda i, j: (layer, 0, j)),
                 pl.BlockSpec((None, tf, D_MODEL), lambda i, j: (layer, j, 0)),
                 vec]
    return pl.pallas_call(
        functools.partial(_proj_mlp_kernel, n_proj=n_proj, final_norm=final_norm),
        grid=(t // tm, D_FF // tf),
        in_specs=in_specs,
        out_specs=pl.BlockSpec((tm, D_MODEL), lambda i, j: (i, 0)),
        out_shape=jax.ShapeDtypeStruct((t, D_MODEL), F32),
        scratch_shapes=[pltpu.VMEM((tm, D_MODEL), F32),
                        pltpu.VMEM((tm, D_MODEL), BF16),
                        pltpu.VMEM((tm, D_MODEL), F32)],
        compiler_params=_params("parallel", "arbitrary"),
        name="proj_mlp",
    )(x2, *ys, *wps, pb, g, w1_all, w2_all, gf)


def _row(v):
    return v.reshape(1, -1).astype(F32)


def _hybrid_front(x3, g_mix, w_in, mu, w0, w_up, a0, a_up, g_up):
    c3 = 3 * RWKV_DIM
    w = jnp.concatenate([w_in[:, c3:SHIFT_DIM], w_in[:, :c3], w_in[:, SHIFT_DIM:]], axis=1).astype(BF16)
    mu_p = jnp.concatenate([mu[c3:], mu[:c3]]).reshape(1, -1).astype(F32)
    w_lora = jnp.zeros((LORA_DIM, c3), F32)
    w_lora = w_lora.at[:LORA_W, :RWKV_DIM].set(w_up)
    w_lora = w_lora.at[LORA_W:LORA_W + LORA_A, RWKV_DIM:2 * RWKV_DIM].set(a_up)
    w_lora = w_lora.at[LORA_W + LORA_A:, 2 * RWKV_DIM:].set(g_up)
    bias = jnp.concatenate([w0, a0, jnp.zeros((RWKV_DIM,), F32)]).reshape(1, -1)
    return _in_proj(x3, g_mix, w, mu_p, w_lora.astype(BF16), bias)


def kernel(x, norm_mix_g, norm_ffn_g, final_norm_g, hy_w_in, hy_mu, hy_w0, hy_w_up, hy_a0, hy_a_up, hy_g_up, hy_k_k, hy_k_a, hy_r_k, hy_gn_g, hy_gn_b, hy_sinks, hy_w_out, cv_pw1_w, cv_pw1_b, cv_dw_w, cv_dw_b, cv_ln_g, cv_ln_b, cv_pw2_w, cv_pw2_b, mlp_w1, mlp_w2):
    bsz, seq, d = x.shape
    depth = norm_mix_g.shape[0]
    t = bsz * seq
    x2 = x.reshape(t, d)
    zero_row = jnp.zeros((1, D_MODEL), F32)
    gf = _row(final_norm_g)
    w1_all, w2_all = mlp_w1.astype(BF16), mlp_w2.astype(BF16)

    for layer in range(depth):
        i = layer // 2
        g_mix = _row(norm_mix_g[layer])
        if layer % 2 == 0:
            r, k, v, lw, a, gate, p_att = _hybrid_front(
                x2.reshape(bsz, seq, d), g_mix, hy_w_in[i], hy_mu[i], hy_w0[i], hy_w_up[i],
                hy_a0[i], hy_a_up[i], hy_g_up[i])
            y_rwkv = _rwkv_scan(r, k, v, lw, a, gate, _row(hy_k_k[i]), _row(hy_k_a[i]),
                                _row(hy_r_k[i]), _row(hy_gn_g[i]), _row(hy_gn_b[i]))
            y_att = _swa(p_att, hy_sinks[i].astype(F32))
            w_out = hy_w_out[i].astype(BF16)
            ys = [y_rwkv.reshape(t, RWKV_DIM), y_att.reshape(t, ATT_DIM)]
            wps = [w_out[:RWKV_DIM], w_out[RWKV_DIM:]]
            pb = zero_row
        else:
            u = _norm_glu(x2, g_mix, cv_pw1_w[i].astype(BF16), _row(cv_pw1_b[i]))
            u = _conv_ln_silu(u.reshape(bsz, seq, D_MODEL), cv_dw_w[i].astype(F32),
                              _row(cv_dw_b[i]), _row(cv_ln_g[i]), _row(cv_ln_b[i]))
            ys = [u.reshape(t, D_MODEL)]
            wps = [cv_pw2_w[i].astype(BF16)]
            pb = _row(cv_pw2_b[i])
        x2 = _proj_mlp(x2, ys, wps, pb, _row(norm_ffn_g[layer]), w1_all, w2_all, layer, gf,
                       final_norm=(layer == depth - 1))
    return x2.reshape(bsz, seq, d)
```

```python
import functools

import jax
import jax.numpy as jnp
from jax import lax
from jax.experimental import pallas as pl
from jax.experimental.pallas import tpu as pltpu

D_MODEL = 1024
HEAD_DIM = 64
RWKV_DIM = 512
ATT_Q_HEADS = 8
ATT_DIM = 512
KV_DIM = 128
LORA_W = 64
LORA_A = 64
LORA_G = 128
SHIFT_DIM = 3 * RWKV_DIM + LORA_W + LORA_A + LORA_G
ATT_COLS = ATT_DIM + 2 * KV_DIM
WINDOW = 128
CONV_WIDTH = 31
CONV_HALO = 32
D_FF = 4 * D_MODEL
RMS_EPS = 1e-6
LN_EPS = 1e-5
GN_EPS = 64e-5

LANES = 128
SUBLANES = 8
CHUNK = 64
SCAN_SLOTS = 4
VMEM_LIMIT = 56 * 1024 * 1024

F32 = jnp.float32
BF16 = jnp.bfloat16
NN = (((1,), (0,)), ((), ()))
NT = (((1,), (1,)), ((), ()))
TN = (((0,), (0,)), ((), ()))


def _params(*sem):
    return pltpu.CompilerParams(dimension_semantics=sem, vmem_limit_bytes=VMEM_LIMIT)


def _rms(x, g):
    return x * lax.rsqrt(jnp.mean(x * x, axis=-1, keepdims=True) + RMS_EPS) * g


LORA_DIM = LORA_W + LORA_A + LORA_G


def _in_proj_kernel(x_ref, g_ref, w_ref, mu_ref, wl_ref, bias_ref,
                    r_ref, k_ref, v_ref, lw_ref, a_ref, gate_ref, att_ref, prev_ref):
    tm = x_ref.shape[1]
    c1, c2, c3 = RWKV_DIM, 2 * RWKV_DIM, 3 * RWKV_DIM

    @pl.when(pl.program_id(1) == 0)
    def _():
        prev_ref[...] = jnp.zeros_like(prev_ref)

    def shift_lerp(p, cs):
        first = lax.broadcasted_iota(jnp.int32, p.shape, 0) == 0
        shifted = jnp.where(first, prev_ref[SUBLANES - 1:SUBLANES, cs], pltpu.roll(p, 1, axis=0))
        prev_ref[:, cs] = p[tm - SUBLANES:, :]
        return p + (shifted - p) * mu_ref[:, cs]

    h = _rms(x_ref[0], g_ref[...]).astype(BF16)
    z = jnp.dot(h, w_ref[:, :LORA_DIM], preferred_element_type=F32)
    z = shift_lerp(z, slice(0, LORA_DIM))
    lane = lax.broadcasted_iota(jnp.int32, z.shape, 1)
    act = jnp.where(lane < LORA_W, jnp.tanh(z),
                    jnp.where(lane < LORA_W + LORA_A, z, jax.nn.sigmoid(z)))
    lo = jnp.dot(act.astype(BF16), wl_ref[...], preferred_element_type=F32) + bias_ref[...]
    rkv = jnp.dot(h, w_ref[:, LORA_DIM:SHIFT_DIM], preferred_element_type=F32)
    att_ref[0] = jnp.dot(h, w_ref[:, SHIFT_DIM:], preferred_element_type=F32)
    rkv = shift_lerp(rkv, slice(LORA_DIM, SHIFT_DIM))
    r_ref[0] = rkv[:, :c1]
    k_ref[0] = rkv[:, c1:c2]
    v_ref[0] = rkv[:, c2:c3]
    wpre = lo[:, :c1]
    softplus = jnp.maximum(-wpre, 0.0) + jnp.log(1.0 + jnp.exp(-jnp.abs(wpre)))
    lw_ref[0] = -jnp.exp(-softplus - 0.5)
    a_ref[0] = jax.nn.sigmoid(lo[:, c1:c2])
    gate_ref[0] = lo[:, c2:]


def _in_proj(x3, g, w, mu, w_lora, bias, tm=512):
    b, s, _ = x3.shape
    rspec = pl.BlockSpec((1, tm, RWKV_DIM), lambda bi, i: (bi, i, 0))
    rshape = jax.ShapeDtypeStruct((b, s, RWKV_DIM), F32)
    const = lambda shape: pl.BlockSpec(shape, lambda bi, i: (0, 0))
    return pl.pallas_call(
        _in_proj_kernel,
        grid=(b, s // tm),
        in_specs=[
            pl.BlockSpec((1, tm, D_MODEL), lambda bi, i: (bi, i, 0)),
            const((1, D_MODEL)), const(w.shape), const((1, SHIFT_DIM)),
            const(w_lora.shape), const((1, 3 * RWKV_DIM)),
        ],
        out_specs=[rspec] * 6 + [pl.BlockSpec((1, tm, ATT_COLS), lambda bi, i: (bi, i, 0))],
        out_shape=[rshape] * 6 + [jax.ShapeDtypeStruct((b, s, ATT_COLS), F32)],
        scratch_shapes=[pltpu.VMEM((SUBLANES, SHIFT_DIM), F32)],
        compiler_params=_params("parallel", "arbitrary"),
        name="in_proj",
    )(x3, g, w, mu, w_lora, bias)


def _norm_glu_kernel(x_ref, g_ref, w_ref, b_ref, o_ref):
    h = _rms(x_ref[...], g_ref[...]).astype(BF16)
    acc = jnp.dot(h, w_ref[...], preferred_element_type=F32) + b_ref[...]
    o_ref[...] = acc[:, :D_MODEL] * jax.nn.sigmoid(acc[:, D_MODEL:])


def _norm_glu(x2, g, w, b, tm=1024):
    t = x2.shape[0]
    return pl.pallas_call(
        _norm_glu_kernel,
        grid=(t // tm,),
        in_specs=[
            pl.BlockSpec((tm, D_MODEL), lambda i: (i, 0)),
            pl.BlockSpec((1, D_MODEL), lambda i: (0, 0)),
            pl.BlockSpec((D_MODEL, 2 * D_MODEL), lambda i: (0, 0)),
            pl.BlockSpec((1, 2 * D_MODEL), lambda i: (0, 0)),
        ],
        out_specs=pl.BlockSpec((tm, D_MODEL), lambda i: (i, 0)),
        out_shape=jax.ShapeDtypeStruct((t, D_MODEL), F32),
        compiler_params=_params("parallel"),
        name="norm_glu",
    )(x2, g, w, b)


def _split3(x):
    hi = x.astype(BF16)
    r1 = x - hi.astype(F32)
    mid = r1.astype(BF16)
    lo = (r1 - mid.astype(F32)).astype(BF16)
    return hi, mid, lo


def _rwkv_scan_kernel(r_ref, k_ref, v_ref, lw_ref, a_ref, g_ref,
                      kk_ref, ka_ref, rk_ref, gg_ref, gb_ref, o_ref, state_ref,
                      atrt_ref, btkt_ref, bhkh_ref, vbf_ref, decay_ref, bonus_ref):
    c_len = CHUNK
    n2 = 2 * c_len
    nb, ts = r_ref.shape[0], r_ref.shape[1]
    n_pairs = RWKV_DIM // LANES
    lane = lax.broadcasted_iota(jnp.int32, (c_len, LANES), 1)
    lo = lane < HEAD_DIM
    lo2 = lax.broadcasted_iota(jnp.int32, (n2, LANES), 1) < HEAD_DIM
    top = lax.broadcasted_iota(jnp.int32, (n2, LANES), 0) < c_len
    own = lo2 == top

    def stack(x):
        return jnp.concatenate([jnp.where(lo, x, 0.0), jnp.where(lo, 0.0, x)], axis=0)

    ri = lax.broadcasted_iota(jnp.int32, (n2, n2), 0)
    ci = lax.broadcasted_iota(jnp.int32, (n2, n2), 1)
    strict = ri > ci
    incl = ri >= ci
    eye = (ri == ci).astype(F32)
    blk = [(ri >> l) == (ci >> l) for l in range(7)]
    tri = (lax.broadcasted_iota(jnp.int32, (c_len, c_len), 0)
           >= lax.broadcasted_iota(jnp.int32, (c_len, c_len), 1)).astype(BF16)

    @pl.when(pl.program_id(1) == 0)
    def _():
        state_ref[...] = jnp.zeros_like(state_ref)

    units = [(bb, j) for bb in range(nb) for j in range(n_pairs)]
    pairs = range(len(units))

    def mm(a, b, dims=NN):
        return lax.dot_general(a.astype(BF16), b.astype(BF16), dims, preferred_element_type=F32)

    def exact_zero(vregs):
        while len(vregs) > 1:
            vregs = [a + b for a, b in zip(vregs[::2], vregs[1::2])] + vregs[len(vregs) & ~1:]
        return (jnp.clip(vregs[0], -1.0, 1.0) * 0.0)[0:1]

    def prep(c, slot):
        sl = pl.ds(pl.multiple_of(c * c_len, c_len), c_len)
        lw_all, cum_all = [], []
        for bb in range(nb):
            lw_b = lw_ref[bb, sl, :]
            cum3 = jnp.dot(tri, jnp.concatenate(_split3(lw_b), axis=1), preferred_element_type=F32)
            lw_all.append(lw_b)
            cum_all.append(cum3[:, :RWKV_DIM] + cum3[:, RWKV_DIM:2 * RWKV_DIM] + cum3[:, 2 * RWKV_DIM:])

        folded = []
        for idx, (bb, j) in enumerate(units):
            cs = slice(j * LANES, (j + 1) * LANES)
            cum, lw = cum_all[bb][:, cs], lw_all[bb][:, cs]
            cum_last = cum[c_len - 1:c_len, :]
            r_s, k_s = stack(r_ref[bb, sl, cs]), stack(k_ref[bb, sl, cs])
            v_s, a_s = stack(v_ref[bb, sl, cs]), stack(a_ref[bb, sl, cs])
            kk = k_s * kk_ref[:, cs]
            kk = kk * lax.rsqrt(jnp.maximum(jnp.sum(kk * kk, axis=-1, keepdims=True), 1e-24))
            kmod = k_s * (1.0 + (a_s - 1.0) * ka_ref[:, cs])
            bvec = kk * a_s

            def both(x):
                return jnp.concatenate([x, x], axis=0)

            einv = both(jnp.exp(-cum))
            edec = both(jnp.exp(cum_last - cum))
            at_rt = jnp.concatenate([-kk * both(jnp.exp(cum - lw)), r_s * both(jnp.exp(cum))], axis=0)
            bh_kh = jnp.concatenate([bvec * edec, kmod * edec], axis=0)
            bonus = jnp.sum(r_s * kmod * rk_ref[:, cs], axis=-1, keepdims=True) * v_s
            atrt_ref[slot, idx] = at_rt.astype(BF16)
            btkt_ref[slot, idx] = jnp.concatenate([bvec * einv, kmod * einv], axis=0).astype(BF16)
            bhkh_ref[slot, idx] = bh_kh.astype(BF16)
            vbf_ref[slot, idx] = v_s.astype(BF16)
            decay_ref[slot, idx] = jnp.exp(cum_last)
            bonus_ref[slot, idx] = bonus
            for x in (at_rt, bh_kh, bonus):
                folded.extend(x[r:r + SUBLANES] for r in range(0, x.shape[0], SUBLANES))
        return exact_zero(folded)

    def inverse_part(slots, between=None, zero=None):
        def tick():
            if between is not None:
                next(between, None)

        chains = [(slot, j) for slot in slots for j in pairs]
        lhs = [atrt_ref[slot, j] for slot, j in chains]
        if zero is not None:
            lhs[0] = lhs[0] + zero.astype(BF16)
        gram = [mm(lhs[i], btkt_ref[slot, j], NT) for i, (slot, j) in enumerate(chains)]
        tick()
        a_ab =[jnp.where(strict, g[:n2, :n2], 0.0).astype(BF16) for g in gram]
        a_ak = [jnp.where(strict, g[:n2, n2:], 0.0).astype(BF16) for g in gram]
        r_b = [jnp.where(incl, g[n2:, :n2], 0.0).astype(BF16) for g in gram]
        r_k = [jnp.where(incl, g[n2:, n2:], 0.0).astype(BF16) for g in gram]
        tinv = [(eye + jnp.where(strict & blk[1], g[:n2, :n2], 0.0)).astype(BF16) for g in gram]
        for lvl in range(2, 7):
            off = blk[lvl] & ~blk[lvl - 1]
            w = [jnp.where(off, mm(a, t), 0.0).astype(BF16) for a, t in zip(a_ab, tinv)]
            tinv = [t + mm(t, wi).astype(BF16) for t, wi in zip(tinv, w)]
            tick()
        akv = [mm(a_ak[i], vbf_ref[slot, j]) for i, (slot, j) in enumerate(chains)]
        if between is not None:
            for _ in between:
                pass
        n = len(pairs)
        return {slot: (tinv[k * n:(k + 1) * n], akv[k * n:(k + 1) * n], r_b[k * n:(k + 1) * n],
                       r_k[k * n:(k + 1) * n]) for k, slot in enumerate(slots)}

    def state_stages(c, slot, tinv, akv, r_b, r_k, zero=None):
        sl = pl.ds(pl.multiple_of(c * c_len, c_len), c_len)
        at_rt = [atrt_ref[slot, j] for j in pairs]
        v_bf = [vbf_ref[slot, j] for j in pairs]
        state = [state_ref[j] for j in pairs]
        if zero is not None:
            state[0] = state[0] + zero
        s_t = [mm(at_rt[j], state[j], NT) for j in pairs]
        yield
        u = [mm(tinv[j], s_t[j][:n2] + akv[j]) for j in pairs]
        u_bf = [u[j].astype(BF16) for j in pairs]
        yield
        for j in pairs:
            state_ref[j] = state[j] * decay_ref[slot, j] + mm(
                jnp.concatenate([u_bf[j], v_bf[j]], axis=0), bhkh_ref[slot, j], TN)
        y = [s_t[j][n2:] + mm(r_b[j], u_bf[j]) + mm(r_k[j], v_bf[j]) for j in pairs]
        yield
        inv_n = 1.0 / HEAD_DIM
        for idx, (bb, j) in enumerate(units):
            cs = slice(j * LANES, (j + 1) * LANES)
            mean = jnp.sum(y[idx], axis=-1, keepdims=True) * inv_n
            yc = jnp.where(own, y[idx] - mean, 0.0)
            var = jnp.sum(yc * yc, axis=-1, keepdims=True) * inv_n
            out = yc * lax.rsqrt(var + GN_EPS) * gg_ref[:, cs] + gb_ref[:, cs] + bonus_ref[slot, idx]
            out = jnp.where(lo, out[:c_len], out[c_len:])
            o_ref[bb, sl, cs] = (out * g_ref[bb, sl, cs]).astype(o_ref.dtype)

    def state_of(c, slots, inv, zero=None):
        for k, slot in enumerate(slots):
            yield from state_stages(c + k, slot, *inv[slot], zero=zero if k == 0 else None)

    n_chunks = ts // c_len
    prep(0, 0)
    prep(1, 1)

    def body(i, carry):
        c0 = 4 * i
        zs = [prep(c0 + 2 + k, 2 + k) for k in range(2)]
        inv_x = inverse_part((0, 1))
        inv_y = inverse_part((2, 3), between=state_of(c0, (0, 1), inv_x))
        zs = [prep(jnp.minimum(c0 + 4 + k, n_chunks - 1), k) for k in range(2)]
        for _ in state_of(c0 + 2, (2, 3), inv_y, zero=zs[0] + zs[1]):
            pass
        return carry

    lax.fori_loop(0, n_chunks // 4, body, 0)


def _rwkv_scan(r, k, v, lw, a, g, kkp, kap, rkp, ggp, gbp, ts=512, nb=2):
    b, s, _ = r.shape
    n_units = nb * RWKV_DIM // LANES
    xspec = pl.BlockSpec((nb, ts, RWKV_DIM), lambda bi, i: (bi, i, 0))
    pspec = pl.BlockSpec((1, RWKV_DIM), lambda bi, i: (0, 0))
    return pl.pallas_call(
        _rwkv_scan_kernel,
        grid=(b // nb, s // ts),
        in_specs=[xspec] * 6 + [pspec] * 5,
        out_specs=xspec,
        out_shape=jax.ShapeDtypeStruct((b, s, RWKV_DIM), BF16),
        scratch_shapes=[
            pltpu.VMEM((n_units, LANES, LANES), F32),
            pltpu.VMEM((SCAN_SLOTS, n_units, 4 * CHUNK, LANES), BF16),
            pltpu.VMEM((SCAN_SLOTS, n_units, 4 * CHUNK, LANES), BF16),
            pltpu.VMEM((SCAN_SLOTS, n_units, 4 * CHUNK, LANES), BF16),
            pltpu.VMEM((SCAN_SLOTS, n_units, 2 * CHUNK, LANES), BF16),
            pltpu.VMEM((SCAN_SLOTS, n_units, 1, LANES), F32),
            pltpu.VMEM((SCAN_SLOTS, n_units, 2 * CHUNK, LANES), F32),
        ],
        compiler_params=_params("parallel", "arbitrary"),
        name="rwkv_scan",
    )(r, k, v, lw, a, g, kkp, kap, rkp, ggp, gbp)


def _swa_kernel(sink_ref, q_ref, kvc_ref, kvp_ref, o_ref):
    n = pl.program_id(1)
    blk = WINDOW
    n_sub = q_ref.shape[1] // blk
    kv_all = jnp.concatenate([kvp_ref[0], kvc_ref[0]], axis=0)
    kmat, vmat = kv_all[:, :KV_DIM], kv_all[:, KV_DIM:]
    lo_kv = lax.broadcasted_iota(jnp.int32, kmat.shape, 1) < HEAD_DIM
    k_sw = pltpu.roll(kmat, HEAD_DIM, axis=1)
    v_sw = pltpu.roll(vmat, HEAD_DIM, axis=1)
    k_dup = [jnp.where(lo_kv, kmat, k_sw).astype(BF16), jnp.where(lo_kv, k_sw, kmat).astype(BF16)]
    v_dup = [jnp.where(lo_kv, vmat, v_sw).astype(BF16), jnp.where(lo_kv, v_sw, vmat).astype(BF16)]

    lo_q = lax.broadcasted_iota(jnp.int32, (blk, LANES), 1) < HEAD_DIM
    rows = lax.broadcasted_iota(jnp.int32, (2 * blk, 2 * blk), 0)
    cols = lax.broadcasted_iota(jnp.int32, (2 * blk, 2 * blk), 1)
    rel = (rows & (blk - 1)) + blk - cols
    band = (rel >= 0) & (rel < WINDOW)
    has_prev = (cols >= blk) | (n > 0)
    relf = rel.astype(F32)
    first = lax.broadcasted_iota(jnp.int32, (2 * blk, 1), 0) < blk

    pairs = range(ATT_Q_HEADS // 2)
    subs = range(n_sub)
    scores = {}
    for sb in subs:
        for j in pairs:
            qp = q_ref[0, sb * blk:(sb + 1) * blk, j * LANES:(j + 1) * LANES] * (HEAD_DIM ** -0.5)
            qs = jnp.concatenate([jnp.where(lo_q, qp, 0.0), jnp.where(lo_q, 0.0, qp)], axis=0)
            keys = k_dup[j // 2][sb * blk:(sb + 2) * blk]
            scores[sb, j] = lax.dot_general(qs.astype(BF16), keys, NT, preferred_element_type=F32)
    probs = {}
    for j in pairs:
        slope = jnp.where(first, 2.0 ** -(2 * j + 1), 2.0 ** -(2 * j + 2))
        sink = jnp.where(first, sink_ref[2 * j], sink_ref[2 * j + 1])
        bias = jnp.where(band, -slope * relf, -jnp.inf)
        for sb in subs:
            s = scores[sb, j] + bias
            if sb == 0:
                s = jnp.where(has_prev, s, -jnp.inf)
            m = jnp.maximum(jnp.max(s, axis=-1, keepdims=True), sink)
            e = jnp.exp(s - m)
            inv = 1.0 / (jnp.sum(e, axis=-1, keepdims=True) + jnp.exp(sink - m))
            probs[sb, j] = (e * inv).astype(BF16)
    for sb in subs:
        for j in pairs:
            o = jnp.dot(probs[sb, j], v_dup[j // 2][sb * blk:(sb + 2) * blk], preferred_element_type=F32)
            o_ref[0, sb * blk:(sb + 1) * blk, j * LANES:(j + 1) * LANES] = (
                jnp.where(lo_q, o[:blk], o[blk:]).astype(o_ref.dtype))


def _swa(p_att3, sinks, n_sub=4):
    b, s, _ = p_att3.shape
    blk = WINDOW
    tq = n_sub * blk
    kv_col = ATT_DIM // (2 * KV_DIM)
    return pl.pallas_call(
        _swa_kernel,
        grid=(b, s // tq),
        in_specs=[
            pl.BlockSpec(memory_space=pltpu.SMEM),
            pl.BlockSpec((1, tq, ATT_DIM), lambda bi, n: (bi, n, 0)),
            pl.BlockSpec((1, tq, 2 * KV_DIM), lambda bi, n: (bi, n, kv_col)),
            pl.BlockSpec((1, blk, 2 * KV_DIM),
                         lambda bi, n: (bi, jnp.maximum(n * n_sub - 1, 0), kv_col)),
        ],
        out_specs=pl.BlockSpec((1, tq, ATT_DIM), lambda bi, n: (bi, n, 0)),
        out_shape=jax.ShapeDtypeStruct((b, s, ATT_DIM), BF16),
        compiler_params=_params("parallel", "parallel"),
        name="swa_attention",
    )(sinks, p_att3, p_att3, p_att3)


def _conv_kernel(u_ref, halo_ref, w_ref, b_ref, lg_ref, lb_ref, o_ref, sh_ref, acc_ref):
    i = pl.program_id(1)
    ts = u_ref.shape[1]
    rb = 128
    sh_ref[0, 0:CONV_HALO, :] = jnp.where(i > 0, halo_ref[0], 0.0)
    sh_ref[0, CONV_HALO:, :] = u_ref[0]
    n_sh = ts + CONV_HALO - SUBLANES
    for q in range(1, SUBLANES):
        for cb in range(D_MODEL // LANES):
            cs = slice(cb * LANES, (cb + 1) * LANES)
            sh_ref[q, 0:n_sh, cs] = sh_ref[0, q:q + n_sh, cs]
    off = CONV_HALO - (CONV_WIDTH - 1)
    for cb in range(D_MODEL // LANES):
        cs = slice(cb * LANES, (cb + 1) * LANES)

        def rows(rblk, carry, cs=cs, qs=(), init=True):
            r0 = pl.multiple_of(rblk * rb, rb)
            acc = (jnp.zeros((rb, LANES), F32) + b_ref[:, cs]) if init else acc_ref[pl.ds(r0, rb), cs]
            for q in qs:
                taps = [j for j in range(CONV_WIDTH) if (off + j) % SUBLANES == q]
                a_max = (off + taps[-1]) // SUBLANES
                x = sh_ref[q, pl.ds(r0, rb + a_max * SUBLANES), cs]
                part = None
                for j in taps:
                    a = (off + j) // SUBLANES
                    term = w_ref[j:j + 1, cs] * x[a * SUBLANES:a * SUBLANES + rb]
                    part = term if part is None else part + term
                acc = acc + part
            acc_ref[pl.ds(r0, rb), cs] = acc
            return carry

        half = SUBLANES // 2
        lax.fori_loop(0, ts // rb, functools.partial(rows, qs=range(half), init=True), 0)
        lax.fori_loop(0, ts // rb, functools.partial(rows, qs=range(half, SUBLANES), init=False), 0)
    y = acc_ref[...]
    mean = jnp.mean(y, axis=-1, keepdims=True)
    yc = y - mean
    var = jnp.mean(yc * yc, axis=-1, keepdims=True)
    yn = yc * lax.rsqrt(var + LN_EPS) * lg_ref[...] + lb_ref[...]
    o_ref[0] = (yn * jax.nn.sigmoid(yn)).astype(o_ref.dtype)


def _conv_ln_silu(u3, dw_w, dw_b, ln_g, ln_b, ts=512):
    b, s, _ = u3.shape
    vec = pl.BlockSpec((1, D_MODEL), lambda bi, i: (0, 0))
    return pl.pallas_call(
        _conv_kernel,
        grid=(b, s // ts),
        in_specs=[
            pl.BlockSpec((1, ts, D_MODEL), lambda bi, i: (bi, i, 0)),
            pl.BlockSpec((1, CONV_HALO, D_MODEL),
                         lambda bi, i: (bi, jnp.maximum(i * (ts // CONV_HALO) - 1, 0), 0)),
            pl.BlockSpec((CONV_WIDTH, D_MODEL), lambda bi, i: (0, 0)),
            vec, vec, vec,
        ],
        out_specs=pl.BlockSpec((1, ts, D_MODEL), lambda bi, i: (bi, i, 0)),
        out_shape=jax.ShapeDtypeStruct((b, s, D_MODEL), BF16),
        scratch_shapes=[pltpu.VMEM((SUBLANES, ts + CONV_HALO, D_MODEL), F32),
                        pltpu.VMEM((ts, D_MODEL), F32)],
        compiler_params=_params("parallel", "parallel"),
        name="conv_ln_silu",
    )(u3, u3, dw_w, dw_b, ln_g, ln_b)


def _proj_mlp_kernel(*refs, n_proj, final_norm):
    x_ref = refs[0]
    y_refs = refs[1:1 + n_proj]
    wp_refs = refs[1 + n_proj:1 + 2 * n_proj]
    pb_ref, g_ref, w1_ref, w2_ref, gf_ref, o_ref, x1_ref, h_ref, acc_ref = refs[1 + 2 * n_proj:]
    j = pl.program_id(1)

    @pl.when(j == 0)
    def _():
        x1 = x_ref[...] + pb_ref[...]
        for y_ref, wp_ref in zip(y_refs, wp_refs):
            x1 = x1 + jnp.dot(y_ref[...], wp_ref[...], preferred_element_type=F32)
        x1_ref[...] = x1
        h_ref[...] = _rms(x1, g_ref[...]).astype(BF16)
        acc_ref[...] = jnp.zeros_like(acc_ref)

    hid = jnp.dot(h_ref[...], w1_ref[...].astype(BF16), preferred_element_type=F32)
    hid = jnp.square(jnp.maximum(hid, 0.0)).astype(BF16)
    acc_ref[...] += jnp.dot(hid, w2_ref[...].astype(BF16), preferred_element_type=F32)

    @pl.when(j == pl.num_programs(1) - 1)
    def _():
        out = x1_ref[...] + acc_ref[...]
        if final_norm:
            out = _rms(out, gf_ref[...])
        o_ref[...] = out


def _proj_mlp(x2, ys, wps, pb, g, w1_all, w2_all, layer, gf, final_norm, tm=1024, tf=1024):
    t = x2.shape[0]
    n_proj = len(ys)
    vec = pl.BlockSpec((1, D_MODEL), lambda i, j: (0, 0))
    in_specs = [pl.BlockSpec((tm, D_MODEL), lambda i, j: (i, 0))]
    in_specs += [pl.BlockSpec((tm, y.shape[1]), lambda i, j: (i, 0)) for y in ys]
    in_specs += [pl.BlockSpec(w.shape, lambda i, j: (0, 0)) for w in wps]
    in_specs += [vec, vec,
                 pl.BlockSpec((None, D_MODEL, tf), lambda i, j: (layer, 0, j)),
                 pl.BlockSpec((None, tf, D_MODEL), lambda i, j: (layer, j, 0)),
                 vec]
    return pl.pallas_call(
        functools.partial(_proj_mlp_kernel, n_proj=n_proj, final_norm=final_norm),
        grid=(t // tm, D_FF // tf),
        in_specs=in_specs,
        out_specs=pl.BlockSpec((tm, D_MODEL), lambda i, j: (i, 0)),
        out_shape=jax.ShapeDtypeStruct((t, D_MODEL), F32),
        scratch_shapes=[pltpu.VMEM((tm, D_MODEL), F32),
                        pltpu.VMEM((tm, D_MODEL), BF16),
                        pltpu.VMEM((tm, D_MODEL), F32)],
        compiler_params=_params("parallel", "arbitrary"),
        name="proj_mlp",
    )(x2, *ys, *wps, pb, g, w1_all, w2_all, gf)


def _row(v):
    return v.reshape(1, -1).astype(F32)


def _hybrid_front(x3, g_mix, w_in, mu, w0, w_up, a0, a_up, g_up):
    c3 = 3 * RWKV_DIM
    w = jnp.concatenate([w_in[:, c3:SHIFT_DIM], w_in[:, :c3], w_in[:, SHIFT_DIM:]], axis=1).astype(BF16)
    mu_p = jnp.concatenate([mu[c3:], mu[:c3]]).reshape(1, -1).astype(F32)
    w_lora = jnp.zeros((LORA_DIM, c3), F32)
    w_lora = w_lora.at[:LORA_W, :RWKV_DIM].set(w_up)
    w_lora = w_lora.at[LORA_W:LORA_W + LORA_A, RWKV_DIM:2 * RWKV_DIM].set(a_up)
    w_lora = w_lora.at[LORA_W + LORA_A:, 2 * RWKV_DIM:].set(g_up)
    bias = jnp.concatenate([w0, a0, jnp.zeros((RWKV_DIM,), F32)]).reshape(1, -1)
    return _in_proj(x3, g_mix, w, mu_p, w_lora.astype(BF16), bias)


def kernel(x, norm_mix_g, norm_ffn_g, final_norm_g, hy_w_in, hy_mu, hy_w0, hy_w_up, hy_a0, hy_a_up, hy_g_up, hy_k_k, hy_k_a, hy_r_k, hy_gn_g, hy_gn_b, hy_sinks, hy_w_out, cv_pw1_w, cv_pw1_b, cv_dw_w, cv_dw_b, cv_ln_g, cv_ln_b, cv_pw2_w, cv_pw2_b, mlp_w1, mlp_w2):
    bsz, seq, d = x.shape
    depth = norm_mix_g.shape[0]
    t = bsz * seq
    x2 = x.reshape(t, d)
    zero_row = jnp.zeros((1, D_MODEL), F32)
    gf = _row(final_norm_g)
    w1_all, w2_all = mlp_w1, mlp_w2

    for layer in range(depth):
        i = layer // 2
        g_mix = _row(norm_mix_g[layer])
        if layer % 2 == 0:
            r, k, v, lw, a, gate, p_att = _hybrid_front(
                x2.reshape(bsz, seq, d), g_mix, hy_w_in[i], hy_mu[i], hy_w0[i], hy_w_up[i],
                hy_a0[i], hy_a_up[i], hy_g_up[i])
            y_rwkv = _rwkv_scan(r, k, v, lw, a, gate, _row(hy_k_k[i]), _row(hy_k_a[i]),
                                _row(hy_r_k[i]), _row(hy_gn_g[i]), _row(hy_gn_b[i]))
            y_att = _swa(p_att, hy_sinks[i].astype(F32))
            w_out = hy_w_out[i].astype(BF16)
            ys = [y_rwkv.reshape(t, RWKV_DIM), y_att.reshape(t, ATT_DIM)]
            wps = [w_out[:RWKV_DIM], w_out[RWKV_DIM:]]
            pb = zero_row
        else:
            u = _norm_glu(x2, g_mix, cv_pw1_w[i].astype(BF16), _row(cv_pw1_b[i]))
            u = _conv_ln_silu(u.reshape(bsz, seq, D_MODEL), cv_dw_w[i].astype(F32),
                              _row(cv_dw_b[i]), _row(cv_ln_g[i]), _row(cv_ln_b[i]))
            ys = [u.reshape(t, D_MODEL)]
            wps = [cv_pw2_w[i].astype(BF16)]
            pb = _row(cv_pw2_b[i])
        x2 = _proj_mlp(x2, ys, wps, pb, _row(norm_ffn_g[layer]), w1_all, w2_all, layer, gf,
                       final_norm=(layer == depth - 1))
    return x2.reshape(bsz, seq, d)
```

```python
import functools

import jax
import jax.numpy as jnp
from jax import lax
from jax.experimental import pallas as pl
from jax.experimental.pallas import tpu as pltpu

D_MODEL = 1024
HEAD_DIM = 64
RWKV_DIM = 512
ATT_Q_HEADS = 8
ATT_DIM = 512
KV_DIM = 128
LORA_W = 64
LORA_A = 64
LORA_G = 128
SHIFT_DIM = 3 * RWKV_DIM + LORA_W + LORA_A + LORA_G
ATT_COLS = ATT_DIM + 2 * KV_DIM
WINDOW = 128
CONV_WIDTH = 31
CONV_HALO = 32
D_FF = 4 * D_MODEL
RMS_EPS = 1e-6
LN_EPS = 1e-5
GN_EPS = 64e-5

LANES = 128
SUBLANES = 8
CHUNK = 64
SCAN_SLOTS = 4
VMEM_LIMIT = 56 * 1024 * 1024

F32 = jnp.float32
BF16 = jnp.bfloat16
NN = (((1,), (0,)), ((), ()))
NT = (((1,), (1,)), ((), ()))
TN = (((0,), (0,)), ((), ()))


def _params(*sem):
    return pltpu.CompilerParams(dimension_semantics=sem, vmem_limit_bytes=VMEM_LIMIT)


def _rms(x, g):
    return x * lax.rsqrt(jnp.mean(x * x, axis=-1, keepdims=True) + RMS_EPS) * g


LORA_DIM = LORA_W + LORA_A + LORA_G


def _in_proj_kernel(x_ref, g_ref, w_ref, mu_ref, wl_ref, bias_ref,
                    r_ref, k_ref, v_ref, lw_ref, a_ref, gate_ref, att_ref, prev_ref):
    tm = x_ref.shape[1]
    c1, c2, c3 = RWKV_DIM, 2 * RWKV_DIM, 3 * RWKV_DIM

    @pl.when(pl.program_id(1) == 0)
    def _():
        prev_ref[...] = jnp.zeros_like(prev_ref)

    def shift_lerp(p, cs):
        first = lax.broadcasted_iota(jnp.int32, p.shape, 0) == 0
        shifted = jnp.where(first, prev_ref[SUBLANES - 1:SUBLANES, cs], pltpu.roll(p, 1, axis=0))
        prev_ref[:, cs] = p[tm - SUBLANES:, :]
        return p + (shifted - p) * mu_ref[:, cs]

    h = _rms(x_ref[0], g_ref[...]).astype(BF16)
    z = jnp.dot(h, w_ref[:, :LORA_DIM], preferred_element_type=F32)
    z = shift_lerp(z, slice(0, LORA_DIM))
    lane = lax.broadcasted_iota(jnp.int32, z.shape, 1)
    act = jnp.where(lane < LORA_W, jnp.tanh(z),
                    jnp.where(lane < LORA_W + LORA_A, z, jax.nn.sigmoid(z)))
    lo = jnp.dot(act.astype(BF16), wl_ref[...], preferred_element_type=F32) + bias_ref[...]
    rkv = jnp.dot(h, w_ref[:, LORA_DIM:SHIFT_DIM], preferred_element_type=F32)
    att_ref[0] = jnp.dot(h, w_ref[:, SHIFT_DIM:], preferred_element_type=F32)
    rkv = shift_lerp(rkv, slice(LORA_DIM, SHIFT_DIM))
    r_ref[0] = rkv[:, :c1]
    k_ref[0] = rkv[:, c1:c2]
    v_ref[0] = rkv[:, c2:c3]
    wpre = lo[:, :c1]
    softplus = jnp.maximum(-wpre, 0.0) + jnp.log(1.0 + jnp.exp(-jnp.abs(wpre)))
    lw_ref[0] = -jnp.exp(-softplus - 0.5)
    a_ref[0] = jax.nn.sigmoid(lo[:, c1:c2])
    gate_ref[0] = lo[:, c2:]


def _in_proj(x3, g, w, mu, w_lora, bias, tm=512):
    b, s, _ = x3.shape
    rspec = pl.BlockSpec((1, tm, RWKV_DIM), lambda bi, i: (bi, i, 0))
    rshape = jax.ShapeDtypeStruct((b, s, RWKV_DIM), F32)
    const = lambda shape: pl.BlockSpec(shape, lambda bi, i: (0, 0))
    return pl.pallas_call(
        _in_proj_kernel,
        grid=(b, s // tm),
        in_specs=[
            pl.BlockSpec((1, tm, D_MODEL), lambda bi, i: (bi, i, 0)),
            const((1, D_MODEL)), const(w.shape), const((1, SHIFT_DIM)),
            const(w_lora.shape), const((1, 3 * RWKV_DIM)),
        ],
        out_specs=[rspec] * 6 + [pl.BlockSpec((1, tm, ATT_COLS), lambda bi, i: (bi, i, 0))],
        out_shape=[rshape] * 6 + [jax.ShapeDtypeStruct((b, s, ATT_COLS), F32)],
        scratch_shapes=[pltpu.VMEM((SUBLANES, SHIFT_DIM), F32)],
        compiler_params=_params("parallel", "arbitrary"),
        name="in_proj",
    )(x3, g, w, mu, w_lora, bias)


def _norm_glu_kernel(x_ref, g_ref, w_ref, b_ref, o_ref):
    h = _rms(x_ref[...], g_ref[...]).astype(BF16)
    acc = jnp.dot(h, w_ref[...], preferred_element_type=F32) + b_ref[...]
    o_ref[...] = acc[:, :D_MODEL] * jax.nn.sigmoid(acc[:, D_MODEL:])


def _norm_glu(x2, g, w, b, tm=1024):
    t = x2.shape[0]
    return pl.pallas_call(
        _norm_glu_kernel,
        grid=(t // tm,),
        in_specs=[
            pl.BlockSpec((tm, D_MODEL), lambda i: (i, 0)),
            pl.BlockSpec((1, D_MODEL), lambda i: (0, 0)),
            pl.BlockSpec((D_MODEL, 2 * D_MODEL), lambda i: (0, 0)),
            pl.BlockSpec((1, 2 * D_MODEL), lambda i: (0, 0)),
        ],
        out_specs=pl.BlockSpec((tm, D_MODEL), lambda i: (i, 0)),
        out_shape=jax.ShapeDtypeStruct((t, D_MODEL), F32),
        compiler_params=_params("parallel"),
        name="norm_glu",
    )(x2, g, w, b)


def _split3(x):
    hi = x.astype(BF16)
    r1 = x - hi.astype(F32)
    mid = r1.astype(BF16)
    lo = (r1 - mid.astype(F32)).astype(BF16)
    return hi, mid, lo


def _rwkv_scan_kernel(r_ref, k_ref, v_ref, lw_ref, a_ref, g_ref,
                      kk_ref, ka_ref, rk_ref, gg_ref, gb_ref, o_ref, state_ref,
                      atrt_ref, btkt_ref, bhkh_ref, vbf_ref, decay_ref, bonus_ref):
    c_len = CHUNK
    n2 = 2 * c_len
    nb, ts = r_ref.shape[0], r_ref.shape[1]
    n_pairs = RWKV_DIM // LANES
    lane = lax.broadcasted_iota(jnp.int32, (c_len, LANES), 1)
    lo = lane < HEAD_DIM
    lo2 = lax.broadcasted_iota(jnp.int32, (n2, LANES), 1) < HEAD_DIM
    top = lax.broadcasted_iota(jnp.int32, (n2, LANES), 0) < c_len
    own = lo2 == top

    def stack(x):
        return jnp.concatenate([jnp.where(lo, x, 0.0), jnp.where(lo, 0.0, x)], axis=0)

    ri = lax.broadcasted_iota(jnp.int32, (n2, n2), 0)
    ci = lax.broadcasted_iota(jnp.int32, (n2, n2), 1)
    strict = ri > ci
    incl = ri >= ci
    eye = (ri == ci).astype(F32)
    blk = [(ri >> l) == (ci >> l) for l in range(7)]
    tri = (lax.broadcasted_iota(jnp.int32, (c_len, c_len), 0)
           >= lax.broadcasted_iota(jnp.int32, (c_len, c_len), 1)).astype(BF16)

    @pl.when(pl.program_id(1) == 0)
    def _():
        state_ref[...] = jnp.zeros_like(state_ref)

    units = [(bb, j) for bb in range(nb) for j in range(n_pairs)]
    pairs = range(len(units))

    def mm(a, b, dims=NN):
        return lax.dot_general(a.astype(BF16), b.astype(BF16), dims, preferred_element_type=F32)

    def exact_zero(vregs):
        while len(vregs) > 1:
            vregs = [a + b for a, b in zip(vregs[::2], vregs[1::2])] + vregs[len(vregs) & ~1:]
        return (jnp.clip(vregs[0], -1.0, 1.0) * 0.0)[0:1]

    def prep(c, slot):
        sl = pl.ds(pl.multiple_of(c * c_len, c_len), c_len)
        lw_all, cum_all = [], []
        for bb in range(nb):
            lw_b = lw_ref[bb, sl, :]
            cum3 = jnp.dot(tri, jnp.concatenate(_split3(lw_b), axis=1), preferred_element_type=F32)
            lw_all.append(lw_b)
            cum_all.append(cum3[:, :RWKV_DIM] + cum3[:, RWKV_DIM:2 * RWKV_DIM] + cum3[:, 2 * RWKV_DIM:])

        folded = []
        for idx, (bb, j) in enumerate(units):
            cs = slice(j * LANES, (j + 1) * LANES)
            cum, lw = cum_all[bb][:, cs], lw_all[bb][:, cs]
            cum_last = cum[c_len - 1:c_len, :]
            r_s, k_s = stack(r_ref[bb, sl, cs]), stack(k_ref[bb, sl, cs])
            v_s, a_s = stack(v_ref[bb, sl, cs]), stack(a_ref[bb, sl, cs])
            kk = k_s * kk_ref[:, cs]
            kk = kk * lax.rsqrt(jnp.maximum(jnp.sum(kk * kk, axis=-1, keepdims=True), 1e-24))
            kmod = k_s * (1.0 + (a_s - 1.0) * ka_ref[:, cs])
            bvec = kk * a_s

            def both(x):
                return jnp.concatenate([x, x], axis=0)

            einv = both(jnp.exp(-cum))
            edec = both(jnp.exp(cum_last - cum))
            at_rt = jnp.concatenate([-kk * both(jnp.exp(cum - lw)), r_s * both(jnp.exp(cum))], axis=0)
            bh_kh = jnp.concatenate([bvec * edec, kmod * edec], axis=0)
            bonus = jnp.sum(r_s * kmod * rk_ref[:, cs], axis=-1, keepdims=True) * v_s
            atrt_ref[slot, idx] = at_rt.astype(BF16)
            btkt_ref[slot, idx] = jnp.concatenate([bvec * einv, kmod * einv], axis=0).astype(BF16)
            bhkh_ref[slot, idx] = bh_kh.astype(BF16)
            vbf_ref[slot, idx] = v_s.astype(BF16)
            decay_ref[slot, idx] = jnp.exp(cum_last)
            bonus_ref[slot, idx] = bonus
            for x in (at_rt, bh_kh, bonus):
                folded.extend(x[r:r + SUBLANES] for r in range(0, x.shape[0], SUBLANES))
        return exact_zero(folded)

    def inverse_part(slots, between=None, zero=None):
        def tick():
            if between is not None:
                next(between, None)

        chains = [(slot, j) for slot in slots for j in pairs]
        lhs = [atrt_ref[slot, j] for slot, j in chains]
        if zero is not None:
            lhs[0] = lhs[0] + zero.astype(BF16)
        gram = [mm(lhs[i], btkt_ref[slot, j], NT) for i, (slot, j) in enumerate(chains)]
        tick()
        a_ab =[jnp.where(strict, g[:n2, :n2], 0.0).astype(BF16) for g in gram]
        a_ak = [jnp.where(strict, g[:n2, n2:], 0.0).astype(BF16) for g in gram]
        r_b = [jnp.where(incl, g[n2:, :n2], 0.0).astype(BF16) for g in gram]
        r_k = [jnp.where(incl, g[n2:, n2:], 0.0).astype(BF16) for g in gram]
        tinv = [(eye + jnp.where(strict & blk[1], g[:n2, :n2], 0.0)).astype(BF16) for g in gram]
        for lvl in range(2, 7):
            off = blk[lvl] & ~blk[lvl - 1]
            w = [jnp.where(off, mm(a, t), 0.0).astype(BF16) for a, t in zip(a_ab, tinv)]
            tinv = [t + mm(t, wi).astype(BF16) for t, wi in zip(tinv, w)]
            tick()
        akv = [mm(a_ak[i], vbf_ref[slot, j]) for i, (slot, j) in enumerate(chains)]
        if between is not None:
            for _ in between:
                pass
        n = len(pairs)
        return {slot: (tinv[k * n:(k + 1) * n], akv[k * n:(k + 1) * n], r_b[k * n:(k + 1) * n],
                       r_k[k * n:(k + 1) * n]) for k, slot in enumerate(slots)}

    def state_stages(c, slot, tinv, akv, r_b, r_k, zero=None):
        sl = pl.ds(pl.multiple_of(c * c_len, c_len), c_len)
        at_rt = [atrt_ref[slot, j] for j in pairs]
        v_bf = [vbf_ref[slot, j] for j in pairs]
        state = [state_ref[j] for j in pairs]
        if zero is not None:
            state[0] = state[0] + zero
        s_t = [mm(at_rt[j], state[j], NT) for j in pairs]
        yield
        u = [mm(tinv[j], s_t[j][:n2] + akv[j]) for j in pairs]
        u_bf = [u[j].astype(BF16) for j in pairs]
        yield
        for j in pairs:
            state_ref[j] = state[j] * decay_ref[slot, j] + mm(
                jnp.concatenate([u_bf[j], v_bf[j]], axis=0), bhkh_ref[slot, j], TN)
        y = [s_t[j][n2:] + mm(r_b[j], u_bf[j]) + mm(r_k[j], v_bf[j]) for j in pairs]
        yield
        inv_n = 1.0 / HEAD_DIM
        for idx, (bb, j) in enumerate(units):
            cs = slice(j * LANES, (j + 1) * LANES)
            mean = jnp.sum(y[idx], axis=-1, keepdims=True) * inv_n
            yc = jnp.where(own, y[idx] - mean, 0.0)
            var = jnp.sum(yc * yc, axis=-1, keepdims=True) * inv_n
            out = yc * lax.rsqrt(var + GN_EPS) * gg_ref[:, cs] + gb_ref[:, cs] + bonus_ref[slot, idx]
            out = jnp.where(lo, out[:c_len], out[c_len:])
            o_ref[bb, sl, cs] = (out * g_ref[bb, sl, cs]).astype(o_ref.dtype)

    def state_of(c, slots, inv, zero=None):
        for k, slot in enumerate(slots):
            yield from state_stages(c + k, slot, *inv[slot], zero=zero if k == 0 else None)

    n_chunks = ts // c_len
    prep(0, 0)
    prep(1, 1)

    def body(i, carry):
        c0 = 4 * i
        zs = [prep(c0 + 2 + k, 2 + k) for k in range(2)]
        inv_x = inverse_part((0, 1))
        inv_y = inverse_part((2, 3), between=state_of(c0, (0, 1), inv_x))
        zs = [prep(jnp.minimum(c0 + 4 + k, n_chunks - 1), k) for k in range(2)]
        for _ in state_of(c0 + 2, (2, 3), inv_y, zero=zs[0] + zs[1]):
            pass
        return carry

    lax.fori_loop(0, n_chunks // 4, body, 0)


def _rwkv_scan(r, k, v, lw, a, g, kkp, kap, rkp, ggp, gbp, ts=512, nb=2):
    b, s, _ = r.shape
    n_units = nb * RWKV_DIM // LANES
    xspec = pl.BlockSpec((nb, ts, RWKV_DIM), lambda bi, i: (bi, i, 0))
    pspec = pl.BlockSpec((1, RWKV_DIM), lambda bi, i: (0, 0))
    return pl.pallas_call(
        _rwkv_scan_kernel,
        grid=(b // nb, s // ts),
        in_specs=[xspec] * 6 + [pspec] * 5,
        out_specs=xspec,
        out_shape=jax.ShapeDtypeStruct((b, s, RWKV_DIM), BF16),
        scratch_shapes=[
            pltpu.VMEM((n_units, LANES, LANES), F32),
            pltpu.VMEM((SCAN_SLOTS, n_units, 4 * CHUNK, LANES), BF16),
            pltpu.VMEM((SCAN_SLOTS, n_units, 4 * CHUNK, LANES), BF16),
            pltpu.VMEM((SCAN_SLOTS, n_units, 4 * CHUNK, LANES), BF16),
            pltpu.VMEM((SCAN_SLOTS, n_units, 2 * CHUNK, LANES), BF16),
            pltpu.VMEM((SCAN_SLOTS, n_units, 1, LANES), F32),
            pltpu.VMEM((SCAN_SLOTS, n_units, 2 * CHUNK, LANES), F32),
        ],
        compiler_params=_params("parallel", "arbitrary"),
        name="rwkv_scan",
    )(r, k, v, lw, a, g, kkp, kap, rkp, ggp, gbp)


def _swa_kernel(sink_ref, q_ref, kvc_ref, kvp_ref, o_ref):
    n = pl.program_id(1)
    blk = WINDOW
    n_sub = q_ref.shape[1] // blk
    kv_all = jnp.concatenate([kvp_ref[0], kvc_ref[0]], axis=0)
    kmat, vmat = kv_all[:, :KV_DIM], kv_all[:, KV_DIM:]
    lo_kv = lax.broadcasted_iota(jnp.int32, kmat.shape, 1) < HEAD_DIM
    k_sw = pltpu.roll(kmat, HEAD_DIM, axis=1)
    v_sw = pltpu.roll(vmat, HEAD_DIM, axis=1)
    k_dup = [jnp.where(lo_kv, kmat, k_sw).astype(BF16), jnp.where(lo_kv, k_sw, kmat).astype(BF16)]
    v_dup = [jnp.where(lo_kv, vmat, v_sw).astype(BF16), jnp.where(lo_kv, v_sw, vmat).astype(BF16)]

    lo_q = lax.broadcasted_iota(jnp.int32, (blk, LANES), 1) < HEAD_DIM
    rows = lax.broadcasted_iota(jnp.int32, (2 * blk, 2 * blk), 0)
    cols = lax.broadcasted_iota(jnp.int32, (2 * blk, 2 * blk), 1)
    rel = (rows & (blk - 1)) + blk - cols
    band = (rel >= 0) & (rel < WINDOW)
    has_prev = (cols >= blk) | (n > 0)
    relf = rel.astype(F32)
    first = lax.broadcasted_iota(jnp.int32, (2 * blk, 1), 0) < blk

    pairs = range(ATT_Q_HEADS // 2)
    subs = range(n_sub)
    scores = {}
    for sb in subs:
        for j in pairs:
            qp = q_ref[0, sb * blk:(sb + 1) * blk, j * LANES:(j + 1) * LANES] * (HEAD_DIM ** -0.5)
            qs = jnp.concatenate([jnp.where(lo_q, qp, 0.0), jnp.where(lo_q, 0.0, qp)], axis=0)
            keys = k_dup[j // 2][sb * blk:(sb + 2) * blk]
            scores[sb, j] = lax.dot_general(qs.astype(BF16), keys, NT, preferred_element_type=F32)
    probs = {}
    for j in pairs:
        slope = jnp.where(first, 2.0 ** -(2 * j + 1), 2.0 ** -(2 * j + 2))
        sink = jnp.where(first, sink_ref[2 * j], sink_ref[2 * j + 1])
        bias = jnp.where(band, -slope * relf, -jnp.inf)
        for sb in subs:
            s = scores[sb, j] + bias
            if sb == 0:
                s = jnp.where(has_prev, s, -jnp.inf)
            m = jnp.maximum(jnp.max(s, axis=-1, keepdims=True), sink)
            e = jnp.exp(s - m)
            inv = 1.0 / (jnp.sum(e, axis=-1, keepdims=True) + jnp.exp(sink - m))
            probs[sb, j] = (e * inv).astype(BF16)
    for sb in subs:
        for j in pairs:
            o = jnp.dot(probs[sb, j], v_dup[j // 2][sb * blk:(sb + 2) * blk], preferred_element_type=F32)
            o_ref[0, sb * blk:(sb + 1) * blk, j * LANES:(j + 1) * LANES] = (
                jnp.where(lo_q, o[:blk], o[blk:]).astype(o_ref.dtype))


def _swa(p_att3, sinks, n_sub=4):
    b, s, _ = p_att3.shape
    blk = WINDOW
    tq = n_sub * blk
    kv_col = ATT_DIM // (2 * KV_DIM)
    return pl.pallas_call(
        _swa_kernel,
        grid=(b, s // tq),
        in_specs=[
            pl.BlockSpec(memory_space=pltpu.SMEM),
            pl.BlockSpec((1, tq, ATT_DIM), lambda bi, n: (bi, n, 0)),
            pl.BlockSpec((1, tq, 2 * KV_DIM), lambda bi, n: (bi, n, kv_col)),
            pl.BlockSpec((1, blk, 2 * KV_DIM),
                         lambda bi, n: (bi, jnp.maximum(n * n_sub - 1, 0), kv_col)),
        ],
        out_specs=pl.BlockSpec((1, tq, ATT_DIM), lambda bi, n: (bi, n, 0)),
        out_shape=jax.ShapeDtypeStruct((b, s, ATT_DIM), BF16),
        compiler_params=_params("parallel", "parallel"),
        name="swa_attention",
    )(sinks, p_att3, p_att3, p_att3)


def _conv_kernel(u_ref, halo_ref, w_ref, b_ref, lg_ref, lb_ref, o_ref, sh_ref, acc_ref):
    i = pl.program_id(1)
    ts = u_ref.shape[1]
    rb = 128
    sh_ref[0, 0:CONV_HALO, :] = jnp.where(i > 0, halo_ref[0], 0.0)
    sh_ref[0, CONV_HALO:, :] = u_ref[0]
    n_sh = ts + CONV_HALO - SUBLANES
    for q in range(1, SUBLANES):
        for cb in range(D_MODEL // LANES):
            cs = slice(cb * LANES, (cb + 1) * LANES)
            sh_ref[q, 0:n_sh, cs] = sh_ref[0, q:q + n_sh, cs]
    off = CONV_HALO - (CONV_WIDTH - 1)
    for cb in range(D_MODEL // LANES):
        cs = slice(cb * LANES, (cb + 1) * LANES)

        def rows(rblk, carry, cs=cs, qs=(), init=True):
            r0 = pl.multiple_of(rblk * rb, rb)
            acc = (jnp.zeros((rb, LANES), F32) + b_ref[:, cs]) if init else acc_ref[pl.ds(r0, rb), cs]
            for q in qs:
                taps = [j for j in range(CONV_WIDTH) if (off + j) % SUBLANES == q]
                a_max = (off + taps[-1]) // SUBLANES
                x = sh_ref[q, pl.ds(r0, rb + a_max * SUBLANES), cs]
                part = None
                for j in taps:
                    a = (off + j) // SUBLANES
                    term = w_ref[j:j + 1, cs] * x[a * SUBLANES:a * SUBLANES + rb]
                    part = term if part is None else part + term
                acc = acc + part
            acc_ref[pl.ds(r0, rb), cs] = acc
            return carry

        half = SUBLANES // 2
        lax.fori_loop(0, ts // rb, functools.partial(rows, qs=range(half), init=True), 0)
        lax.fori_loop(0, ts // rb, functools.partial(rows, qs=range(half, SUBLANES), init=False), 0)
    y = acc_ref[...]
    mean = jnp.mean(y, axis=-1, keepdims=True)
    yc = y - mean
    var = jnp.mean(yc * yc, axis=-1, keepdims=True)
    yn = yc * lax.rsqrt(var + LN_EPS) * lg_ref[...] + lb_ref[...]
    o_ref[0] = (yn * jax.nn.sigmoid(yn)).astype(o_ref.dtype)


def _conv_ln_silu(u3, dw_w, dw_b, ln_g, ln_b, ts=512):
    b, s, _ = u3.shape
    vec = pl.BlockSpec((1, D_MODEL), lambda bi, i: (0, 0))
    return pl.pallas_call(
        _conv_kernel,
        grid=(b, s // ts),
        in_specs=[
            pl.BlockSpec((1, ts, D_MODEL), lambda bi, i: (bi, i, 0)),
            pl.BlockSpec((1, CONV_HALO, D_MODEL),
                         lambda bi, i: (bi, jnp.maximum(i * (ts // CONV_HALO) - 1, 0), 0)),
            pl.BlockSpec((CONV_WIDTH, D_MODEL), lambda bi, i: (0, 0)),
            vec, vec, vec,
        ],
        out_specs=pl.BlockSpec((1, ts, D_MODEL), lambda bi, i: (bi, i, 0)),
        out_shape=jax.ShapeDtypeStruct((b, s, D_MODEL), BF16),
        scratch_shapes=[pltpu.VMEM((SUBLANES, ts + CONV_HALO, D_MODEL), F32),
                        pltpu.VMEM((ts, D_MODEL), F32)],
        compiler_params=_params("parallel", "parallel"),
        name="conv_ln_silu",
    )(u3, u3, dw_w, dw_b, ln_g, ln_b)


def _proj_mlp_kernel(*refs, n_proj, final_norm):
    x_ref = refs[0]
    y_refs = refs[1:1 + n_proj]
    wp_refs = refs[1 + n_proj:1 + 2 * n_proj]
    pb_ref, g_ref, w1_ref, w2_ref, gf_ref, o_ref, x1_ref, h_ref, acc_ref = refs[1 + 2 * n_proj:]
    j = pl.program_id(1)
    last = pl.num_programs(1) - 1
    tm = x_ref.shape[0]
    all_rows = slice(0, tm)
    halves = (slice(0, tm // 2), slice(tm // 2, tm))

    def residual_in(rows):
        x1 = x_ref[rows, :] + pb_ref[...]
        for y_ref, wp_ref in zip(y_refs, wp_refs):
            x1 = x1 + jnp.dot(y_ref[rows, :], wp_ref[...], preferred_element_type=F32)
        x1_ref[rows, :] = x1
        h_ref[rows, :] = _rms(x1, g_ref[...]).astype(BF16)

    def hidden(rows, w1):
        hid = jnp.dot(h_ref[rows, :], w1, preferred_element_type=F32)
        return jnp.square(jnp.maximum(hid, 0.0)).astype(BF16)

    def residual_out(rows, mlp):
        out = x1_ref[rows, :] + mlp
        if final_norm:
            out = _rms(out, gf_ref[...])
        o_ref[rows, :] = out

    @pl.when(j == 0)
    def _():
        w1, w2 = w1_ref[...].astype(BF16), w2_ref[...].astype(BF16)
        for rows in halves:
            residual_in(rows)
        for rows in halves:
            acc_ref[rows, :] = jnp.dot(hidden(rows, w1), w2, preferred_element_type=F32)

    @pl.when((j > 0) & (j < last))
    def _():
        hid = hidden(all_rows, w1_ref[...].astype(BF16))
        acc_ref[...] += jnp.dot(hid, w2_ref[...].astype(BF16), preferred_element_type=F32)

    @pl.when(j == last)
    def _():
        w2 = w2_ref[...].astype(BF16)
        hid = hidden(all_rows, w1_ref[...].astype(BF16))
        for rows in halves:
            residual_out(rows, acc_ref[rows, :] + jnp.dot(hid[rows], w2, preferred_element_type=F32))


def _proj_mlp(x2, ys, wps, pb, g, w1_all, w2_all, layer, gf, final_norm, tm=1024, tf=1024):
    t = x2.shape[0]
    n_proj = len(ys)
    vec = pl.BlockSpec((1, D_MODEL), lambda i, j: (0, 0))
    in_specs = [pl.BlockSpec((tm, D_MODEL), lambda i, j: (i, 0))]
    in_specs += [pl.BlockSpec((tm, y.shape[1]), lambda i, j: (i, 0)) for y in ys]
    in_specs += [pl.BlockSpec(w.shape, lambda i, j: (0, 0)) for w in wps]
    in_specs += [vec, vec,
                 pl.BlockSpec((None, D_MODEL, tf), lambda i, j: (layer, 0, j)),
                 pl.BlockSpec((None, tf, D_MODEL), lambda i, j: (layer, j, 0)),
                 vec]
    return pl.pallas_call(
        functools.partial(_proj_mlp_kernel, n_proj=n_proj, final_norm=final_norm),
        grid=(t // tm, D_FF // tf),
        in_specs=in_specs,
        out_specs=pl.BlockSpec((tm, D_MODEL), lambda i, j: (i, 0)),
        out_shape=jax.ShapeDtypeStruct((t, D_MODEL), F32),
        scratch_shapes=[pltpu.VMEM((tm, D_MODEL), F32),
                        pltpu.VMEM((tm, D_MODEL), BF16),
                        pltpu.VMEM((tm, D_MODEL), F32)],
        compiler_params=_params("parallel", "arbitrary"),
        name="proj_mlp",
    )(x2, *ys, *wps, pb, g, w1_all, w2_all, gf)


def _row(v):
    return v.reshape(1, -1).astype(F32)


def _hybrid_front(x3, g_mix, w_in, mu, w0, w_up, a0, a_up, g_up):
    c3 = 3 * RWKV_DIM
    w = jnp.concatenate([w_in[:, c3:SHIFT_DIM], w_in[:, :c3], w_in[:, SHIFT_DIM:]], axis=1).astype(BF16)
    mu_p = jnp.concatenate([mu[c3:], mu[:c3]]).reshape(1, -1).astype(F32)
    w_lora = jnp.zeros((LORA_DIM, c3), F32)
    w_lora = w_lora.at[:LORA_W, :RWKV_DIM].set(w_up)
    w_lora = w_lora.at[LORA_W:LORA_W + LORA_A, RWKV_DIM:2 * RWKV_DIM].set(a_up)
    w_lora = w_lora.at[LORA_W + LORA_A:, 2 * RWKV_DIM:].set(g_up)
    bias = jnp.concatenate([w0, a0, jnp.zeros((RWKV_DIM,), F32)]).reshape(1, -1)
    return _in_proj(x3, g_mix, w, mu_p, w_lora.astype(BF16), bias)


def kernel(x, norm_mix_g, norm_ffn_g, final_norm_g, hy_w_in, hy_mu, hy_w0, hy_w_up, hy_a0, hy_a_up, hy_g_up, hy_k_k, hy_k_a, hy_r_k, hy_gn_g, hy_gn_b, hy_sinks, hy_w_out, cv_pw1_w, cv_pw1_b, cv_dw_w, cv_dw_b, cv_ln_g, cv_ln_b, cv_pw2_w, cv_pw2_b, mlp_w1, mlp_w2):
    bsz, seq, d = x.shape
    depth = norm_mix_g.shape[0]
    t = bsz * seq
    x2 = x.reshape(t, d)
    zero_row = jnp.zeros((1, D_MODEL), F32)
    gf = _row(final_norm_g)
    w1_all, w2_all = mlp_w1, mlp_w2

    for layer in range(depth):
        i = layer // 2
        g_mix = _row(norm_mix_g[layer])
        if layer % 2 == 0:
            r, k, v, lw, a, gate, p_att = _hybrid_front(
                x2.reshape(bsz, seq, d), g_mix, hy_w_in[i], hy_mu[i], hy_w0[i], hy_w_up[i],
                hy_a0[i], hy_a_up[i], hy_g_up[i])
            y_rwkv = _rwkv_scan(r, k, v, lw, a, gate, _row(hy_k_k[i]), _row(hy_k_a[i]),
                                _row(hy_r_k[i]), _row(hy_gn_g[i]), _row(hy_gn_b[i]))
            y_att = _swa(p_att, hy_sinks[i].astype(F32))
            w_out = hy_w_out[i].astype(BF16)
            ys = [y_rwkv.reshape(t, RWKV_DIM), y_att.reshape(t, ATT_DIM)]
            wps = [w_out[:RWKV_DIM], w_out[RWKV_DIM:]]
            pb = zero_row
        else:
            u = _norm_glu(x2, g_mix, cv_pw1_w[i].astype(BF16), _row(cv_pw1_b[i]))
            u = _conv_ln_silu(u.reshape(bsz, seq, D_MODEL), cv_dw_w[i].astype(F32),
                              _row(cv_dw_b[i]), _row(cv_ln_g[i]), _row(cv_ln_b[i]))
            ys = [u.reshape(t, D_MODEL)]
            wps = [cv_pw2_w[i].astype(BF16)]
            pb = _row(cv_pw2_b[i])
        x2 = _proj_mlp(x2, ys, wps, pb, _row(norm_ffn_g[layer]), w1_all, w2_all, layer, gf,
                       final_norm=(layer == depth - 1))
    return x2.reshape(bsz, seq, d)
```

```python
import functools

import jax
import jax.numpy as jnp
from jax import lax
from jax.experimental import pallas as pl
from jax.experimental.pallas import tpu as pltpu

D_MODEL = 1024
HEAD_DIM = 64
RWKV_DIM = 512
ATT_Q_HEADS = 8
ATT_DIM = 512
KV_DIM = 128
LORA_W = 64
LORA_A = 64
LORA_G = 128
SHIFT_DIM = 3 * RWKV_DIM + LORA_W + LORA_A + LORA_G
ATT_COLS = ATT_DIM + 2 * KV_DIM
WINDOW = 128
CONV_WIDTH = 31
CONV_HALO = 32
D_FF = 4 * D_MODEL
RMS_EPS = 1e-6
LN_EPS = 1e-5
GN_EPS = 64e-5

LANES = 128
SUBLANES = 8
CHUNK = 64
SCAN_SLOTS = 4
VMEM_LIMIT = 56 * 1024 * 1024

F32 = jnp.float32
BF16 = jnp.bfloat16
NN = (((1,), (0,)), ((), ()))
NT = (((1,), (1,)), ((), ()))
TN = (((0,), (0,)), ((), ()))


def _params(*sem):
    return pltpu.CompilerParams(dimension_semantics=sem, vmem_limit_bytes=VMEM_LIMIT)


def _rms(x, g):
    return x * lax.rsqrt(jnp.mean(x * x, axis=-1, keepdims=True) + RMS_EPS) * g


LORA_DIM = LORA_W + LORA_A + LORA_G


def _in_proj_kernel(x_ref, g_ref, w_ref, mu_ref, wl_ref, bias_ref,
                    r_ref, k_ref, v_ref, lw_ref, a_ref, gate_ref, att_ref, prev_ref):
    tm = x_ref.shape[1]
    c1, c2, c3 = RWKV_DIM, 2 * RWKV_DIM, 3 * RWKV_DIM

    @pl.when(pl.program_id(1) == 0)
    def _():
        prev_ref[...] = jnp.zeros_like(prev_ref)

    def shift_lerp(p, cs):
        first = lax.broadcasted_iota(jnp.int32, p.shape, 0) == 0
        shifted = jnp.where(first, prev_ref[SUBLANES - 1:SUBLANES, cs], pltpu.roll(p, 1, axis=0))
        prev_ref[:, cs] = p[tm - SUBLANES:, :]
        return p + (shifted - p) * mu_ref[:, cs]

    h = _rms(x_ref[0], g_ref[...]).astype(BF16)
    z = jnp.dot(h, w_ref[:, :LORA_DIM], preferred_element_type=F32)
    z = shift_lerp(z, slice(0, LORA_DIM))
    lane = lax.broadcasted_iota(jnp.int32, z.shape, 1)
    act = jnp.where(lane < LORA_W, jnp.tanh(z),
                    jnp.where(lane < LORA_W + LORA_A, z, jax.nn.sigmoid(z)))
    lo = jnp.dot(act.astype(BF16), wl_ref[...], preferred_element_type=F32) + bias_ref[...]
    rkv = jnp.dot(h, w_ref[:, LORA_DIM:SHIFT_DIM], preferred_element_type=F32)
    att_ref[0] = jnp.dot(h, w_ref[:, SHIFT_DIM:], preferred_element_type=F32)
    rkv = shift_lerp(rkv, slice(LORA_DIM, SHIFT_DIM))
    r_ref[0] = rkv[:, :c1]
    k_ref[0] = rkv[:, c1:c2]
    v_ref[0] = rkv[:, c2:c3]
    wpre = lo[:, :c1]
    softplus = jnp.maximum(-wpre, 0.0) + jnp.log(1.0 + jnp.exp(-jnp.abs(wpre)))
    lw_ref[0] = -jnp.exp(-softplus - 0.5)
    a_ref[0] = jax.nn.sigmoid(lo[:, c1:c2])
    gate_ref[0] = lo[:, c2:]


def _in_proj(x3, g, w, mu, w_lora, bias, tm=512):
    b, s, _ = x3.shape
    rspec = pl.BlockSpec((1, tm, RWKV_DIM), lambda bi, i: (bi, i, 0))
    rshape = jax.ShapeDtypeStruct((b, s, RWKV_DIM), F32)
    const = lambda shape: pl.BlockSpec(shape, lambda bi, i: (0, 0))
    return pl.pallas_call(
        _in_proj_kernel,
        grid=(b, s // tm),
        in_specs=[
            pl.BlockSpec((1, tm, D_MODEL), lambda bi, i: (bi, i, 0)),
            const((1, D_MODEL)), const(w.shape), const((1, SHIFT_DIM)),
            const(w_lora.shape), const((1, 3 * RWKV_DIM)),
        ],
        out_specs=[rspec] * 6 + [pl.BlockSpec((1, tm, ATT_COLS), lambda bi, i: (bi, i, 0))],
        out_shape=[rshape] * 6 + [jax.ShapeDtypeStruct((b, s, ATT_COLS), F32)],
        scratch_shapes=[pltpu.VMEM((SUBLANES, SHIFT_DIM), F32)],
        compiler_params=_params("parallel", "arbitrary"),
        name="in_proj",
    )(x3, g, w, mu, w_lora, bias)


def _norm_glu_kernel(x_ref, g_ref, w_ref, b_ref, o_ref):
    h = _rms(x_ref[...], g_ref[...]).astype(BF16)
    acc = jnp.dot(h, w_ref[...], preferred_element_type=F32) + b_ref[...]
    o_ref[...] = acc[:, :D_MODEL] * jax.nn.sigmoid(acc[:, D_MODEL:])


def _norm_glu(x2, g, w, b, tm=1024):
    t = x2.shape[0]
    return pl.pallas_call(
        _norm_glu_kernel,
        grid=(t // tm,),
        in_specs=[
            pl.BlockSpec((tm, D_MODEL), lambda i: (i, 0)),
            pl.BlockSpec((1, D_MODEL), lambda i: (0, 0)),
            pl.BlockSpec((D_MODEL, 2 * D_MODEL), lambda i: (0, 0)),
            pl.BlockSpec((1, 2 * D_MODEL), lambda i: (0, 0)),
        ],
        out_specs=pl.BlockSpec((tm, D_MODEL), lambda i: (i, 0)),
        out_shape=jax.ShapeDtypeStruct((t, D_MODEL), F32),
        compiler_params=_params("parallel"),
        name="norm_glu",
    )(x2, g, w, b)


def _split3(x):
    hi = x.astype(BF16)
    r1 = x - hi.astype(F32)
    mid = r1.astype(BF16)
    lo = (r1 - mid.astype(F32)).astype(BF16)
    return hi, mid, lo


def _rwkv_scan_kernel(r_ref, k_ref, v_ref, lw_ref, a_ref, g_ref,
                      kk_ref, ka_ref, rk_ref, gg_ref, gb_ref, o_ref, state_ref,
                      atrt_ref, btkt_ref, bhkh_ref, vbf_ref, decay_ref, bonus_ref):
    c_len = CHUNK
    n2 = 2 * c_len
    nb, ts = r_ref.shape[0], r_ref.shape[1]
    n_pairs = RWKV_DIM // LANES
    lane = lax.broadcasted_iota(jnp.int32, (c_len, LANES), 1)
    lo = lane < HEAD_DIM
    lo2 = lax.broadcasted_iota(jnp.int32, (n2, LANES), 1) < HEAD_DIM
    top = lax.broadcasted_iota(jnp.int32, (n2, LANES), 0) < c_len
    own = lo2 == top

    def stack(x):
        return jnp.concatenate([jnp.where(lo, x, 0.0), jnp.where(lo, 0.0, x)], axis=0)

    ri = lax.broadcasted_iota(jnp.int32, (n2, n2), 0)
    ci = lax.broadcasted_iota(jnp.int32, (n2, n2), 1)
    strict = ri > ci
    incl = ri >= ci
    eye = (ri == ci).astype(F32)
    blk = [(ri >> l) == (ci >> l) for l in range(7)]
    tri = (lax.broadcasted_iota(jnp.int32, (c_len, c_len), 0)
           >= lax.broadcasted_iota(jnp.int32, (c_len, c_len), 1)).astype(BF16)

    @pl.when(pl.program_id(1) == 0)
    def _():
        state_ref[...] = jnp.zeros_like(state_ref)

    units = [(bb, j) for bb in range(nb) for j in range(n_pairs)]
    pairs = range(len(units))

    def mm(a, b, dims=NN):
        return lax.dot_general(a.astype(BF16), b.astype(BF16), dims, preferred_element_type=F32)

    def exact_zero(vregs):
        while len(vregs) > 1:
            vregs = [a + b for a, b in zip(vregs[::2], vregs[1::2])] + vregs[len(vregs) & ~1:]
        return (jnp.clip(vregs[0], -1.0, 1.0) * 0.0)[0:1]

    def prep(c, slot):
        sl = pl.ds(pl.multiple_of(c * c_len, c_len), c_len)
        lw_all, cum_all = [], []
        for bb in range(nb):
            lw_b = lw_ref[bb, sl, :]
            cum3 = jnp.dot(tri, jnp.concatenate(_split3(lw_b), axis=1), preferred_element_type=F32)
            lw_all.append(lw_b)
            cum_all.append(cum3[:, :RWKV_DIM] + cum3[:, RWKV_DIM:2 * RWKV_DIM] + cum3[:, 2 * RWKV_DIM:])

        folded = []
        for idx, (bb, j) in enumerate(units):
            cs = slice(j * LANES, (j + 1) * LANES)
            cum, lw = cum_all[bb][:, cs], lw_all[bb][:, cs]
            cum_last = cum[c_len - 1:c_len, :]
            r_s, k_s = stack(r_ref[bb, sl, cs]), stack(k_ref[bb, sl, cs])
            v_s, a_s = stack(v_ref[bb, sl, cs]), stack(a_ref[bb, sl, cs])
            kk = k_s * kk_ref[:, cs]
            kk = kk * lax.rsqrt(jnp.maximum(jnp.sum(kk * kk, axis=-1, keepdims=True), 1e-24))
            kmod = k_s * (1.0 + (a_s - 1.0) * ka_ref[:, cs])
            bvec = kk * a_s

            def both(x):
                return jnp.concatenate([x, x], axis=0)

            einv = both(jnp.exp(-cum))
            edec = both(jnp.exp(cum_last - cum))
            at_rt = jnp.concatenate([-kk * both(jnp.exp(cum - lw)), r_s * both(jnp.exp(cum))], axis=0)
            bh_kh = jnp.concatenate([bvec * edec, kmod * edec], axis=0)
            bonus = jnp.sum(r_s * kmod * rk_ref[:, cs], axis=-1, keepdims=True) * v_s
            atrt_ref[slot, idx] = at_rt.astype(BF16)
            btkt_ref[slot, idx] = jnp.concatenate([bvec * einv, kmod * einv], axis=0).astype(BF16)
            bhkh_ref[slot, idx] = bh_kh.astype(BF16)
            vbf_ref[slot, idx] = v_s.astype(BF16)
            decay_ref[slot, idx] = jnp.exp(cum_last)
            bonus_ref[slot, idx] = bonus
            for x in (at_rt, bh_kh, bonus):
                folded.extend(x[r:r + SUBLANES] for r in range(0, x.shape[0], SUBLANES))
        return exact_zero(folded)

    def inverse_part(slots, between=None, zero=None):
        def tick():
            if between is not None:
                next(between, None)

        chains = [(slot, j) for slot in slots for j in pairs]
        lhs = [atrt_ref[slot, j] for slot, j in chains]
        if zero is not None:
            lhs[0] = lhs[0] + zero.astype(BF16)
        gram = [mm(lhs[i], btkt_ref[slot, j], NT) for i, (slot, j) in enumerate(chains)]
        tick()
        a_ab =[jnp.where(strict, g[:n2, :n2], 0.0).astype(BF16) for g in gram]
        a_ak = [jnp.where(strict, g[:n2, n2:], 0.0).astype(BF16) for g in gram]
        r_b = [jnp.where(incl, g[n2:, :n2], 0.0).astype(BF16) for g in gram]
        r_k = [jnp.where(incl, g[n2:, n2:], 0.0).astype(BF16) for g in gram]
        tinv = [(eye + jnp.where(strict & blk[1], g[:n2, :n2], 0.0)).astype(BF16) for g in gram]
        for lvl in range(2, 7):
            off = blk[lvl] & ~blk[lvl - 1]
            w = [jnp.where(off, mm(a, t), 0.0).astype(BF16) for a, t in zip(a_ab, tinv)]
            tinv = [t + mm(t, wi).astype(BF16) for t, wi in zip(tinv, w)]
            tick()
        akv = [mm(a_ak[i], vbf_ref[slot, j]) for i, (slot, j) in enumerate(chains)]
        if between is not None:
            for _ in between:
                pass
        n = len(pairs)
        return {slot: (tinv[k * n:(k + 1) * n], akv[k * n:(k + 1) * n], r_b[k * n:(k + 1) * n],
                       r_k[k * n:(k + 1) * n]) for k, slot in enumerate(slots)}

    def state_stages(c, slot, tinv, akv, r_b, r_k, zero=None):
        sl = pl.ds(pl.multiple_of(c * c_len, c_len), c_len)
        at_rt = [atrt_ref[slot, j] for j in pairs]
        v_bf = [vbf_ref[slot, j] for j in pairs]
        state = [state_ref[j] for j in pairs]
        if zero is not None:
            state[0] = state[0] + zero
        s_t = [mm(at_rt[j], state[j], NT) for j in pairs]
        yield
        rhs = [s_t[j][:n2] + akv[j] for j in pairs]
        rhs_hi = [x.astype(BF16) for x in rhs]
        rhs_lo = [(x - h.astype(F32)).astype(BF16) for x, h in zip(rhs, rhs_hi)]
        u = [mm(tinv[j], rhs_hi[j]) + mm(tinv[j], rhs_lo[j]) for j in pairs]
        u_bf = [u[j].astype(BF16) for j in pairs]
        yield
        for j in pairs:
            state_ref[j] = state[j] * decay_ref[slot, j] + mm(
                jnp.concatenate([u_bf[j], v_bf[j]], axis=0), bhkh_ref[slot, j], TN)
        y = [s_t[j][n2:] + mm(r_b[j], u_bf[j]) + mm(r_k[j], v_bf[j]) for j in pairs]
        yield
        inv_n = 1.0 / HEAD_DIM
        for idx, (bb, j) in enumerate(units):
            cs = slice(j * LANES, (j + 1) * LANES)
            mean = jnp.sum(y[idx], axis=-1, keepdims=True) * inv_n
            yc = jnp.where(own, y[idx] - mean, 0.0)
            var = jnp.sum(yc * yc, axis=-1, keepdims=True) * inv_n
            out = yc * lax.rsqrt(var + GN_EPS) * gg_ref[:, cs] + gb_ref[:, cs] + bonus_ref[slot, idx]
            out = jnp.where(lo, out[:c_len], out[c_len:])
            o_ref[bb, sl, cs] = (out * g_ref[bb, sl, cs]).astype(o_ref.dtype)

    def state_of(c, slots, inv, zero=None):
        for k, slot in enumerate(slots):
            yield from state_stages(c + k, slot, *inv[slot], zero=zero if k == 0 else None)

    n_chunks = ts // c_len
    prep(0, 0)
    prep(1, 1)

    def body(i, carry):
        c0 = 4 * i
        zs = [prep(c0 + 2 + k, 2 + k) for k in range(2)]
        inv_x = inverse_part((0, 1))
        inv_y = inverse_part((2, 3), between=state_of(c0, (0, 1), inv_x))
        zs = [prep(jnp.minimum(c0 + 4 + k, n_chunks - 1), k) for k in range(2)]
        for _ in state_of(c0 + 2, (2, 3), inv_y, zero=zs[0] + zs[1]):
            pass
        return carry

    lax.fori_loop(0, n_chunks // 4, body, 0)


def _rwkv_scan(r, k, v, lw, a, g, kkp, kap, rkp, ggp, gbp, ts=512, nb=2):
    b, s, _ = r.shape
    n_units = nb * RWKV_DIM // LANES
    xspec = pl.BlockSpec((nb, ts, RWKV_DIM), lambda bi, i: (bi, i, 0))
    pspec = pl.BlockSpec((1, RWKV_DIM), lambda bi, i: (0, 0))
    return pl.pallas_call(
        _rwkv_scan_kernel,
        grid=(b // nb, s // ts),
        in_specs=[xspec] * 6 + [pspec] * 5,
        out_specs=xspec,
        out_shape=jax.ShapeDtypeStruct((b, s, RWKV_DIM), BF16),
        scratch_shapes=[
            pltpu.VMEM((n_units, LANES, LANES), F32),
            pltpu.VMEM((SCAN_SLOTS, n_units, 4 * CHUNK, LANES), BF16),
            pltpu.VMEM((SCAN_SLOTS, n_units, 4 * CHUNK, LANES), BF16),
            pltpu.VMEM((SCAN_SLOTS, n_units, 4 * CHUNK, LANES), BF16),
            pltpu.VMEM((SCAN_SLOTS, n_units, 2 * CHUNK, LANES), BF16),
            pltpu.VMEM((SCAN_SLOTS, n_units, 1, LANES), F32),
            pltpu.VMEM((SCAN_SLOTS, n_units, 2 * CHUNK, LANES), F32),
        ],
        compiler_params=_params("parallel", "arbitrary"),
        name="rwkv_scan",
    )(r, k, v, lw, a, g, kkp, kap, rkp, ggp, gbp)


def _swa_kernel(sink_ref, q_ref, kvc_ref, kvp_ref, o_ref):
    n = pl.program_id(1)
    blk = WINDOW
    n_sub = q_ref.shape[1] // blk
    kv_all = jnp.concatenate([kvp_ref[0], kvc_ref[0]], axis=0)
    kmat, vmat = kv_all[:, :KV_DIM], kv_all[:, KV_DIM:]
    lo_kv = lax.broadcasted_iota(jnp.int32, kmat.shape, 1) < HEAD_DIM
    k_sw = pltpu.roll(kmat, HEAD_DIM, axis=1)
    v_sw = pltpu.roll(vmat, HEAD_DIM, axis=1)
    k_dup = [jnp.where(lo_kv, kmat, k_sw).astype(BF16), jnp.where(lo_kv, k_sw, kmat).astype(BF16)]
    v_dup = [jnp.where(lo_kv, vmat, v_sw).astype(BF16), jnp.where(lo_kv, v_sw, vmat).astype(BF16)]

    lo_q = lax.broadcasted_iota(jnp.int32, (blk, LANES), 1) < HEAD_DIM
    rows = lax.broadcasted_iota(jnp.int32, (2 * blk, 2 * blk), 0)
    cols = lax.broadcasted_iota(jnp.int32, (2 * blk, 2 * blk), 1)
    rel = (rows & (blk - 1)) + blk - cols
    band = (rel >= 0) & (rel < WINDOW)
    has_prev = (cols >= blk) | (n > 0)
    relf = rel.astype(F32)
    first = lax.broadcasted_iota(jnp.int32, (2 * blk, 1), 0) < blk

    pairs = range(ATT_Q_HEADS // 2)
    subs = range(n_sub)
    scores = {}
    for sb in subs:
        for j in pairs:
            qp = q_ref[0, sb * blk:(sb + 1) * blk, j * LANES:(j + 1) * LANES] * (HEAD_DIM ** -0.5)
            qs = jnp.concatenate([jnp.where(lo_q, qp, 0.0), jnp.where(lo_q, 0.0, qp)], axis=0)
            keys = k_dup[j // 2][sb * blk:(sb + 2) * blk]
            scores[sb, j] = lax.dot_general(qs.astype(BF16), keys, NT, preferred_element_type=F32)
    probs = {}
    for j in pairs:
        slope = jnp.where(first, 2.0 ** -(2 * j + 1), 2.0 ** -(2 * j + 2))
        sink = jnp.where(first, sink_ref[2 * j], sink_ref[2 * j + 1])
        bias = jnp.where(band, -slope * relf, -jnp.inf)
        for sb in subs:
            s = scores[sb, j] + bias
            if sb == 0:
                s = jnp.where(has_prev, s, -jnp.inf)
            m = jnp.maximum(jnp.max(s, axis=-1, keepdims=True), sink)
            e = jnp.exp(s - m)
            inv = 1.0 / (jnp.sum(e, axis=-1, keepdims=True) + jnp.exp(sink - m))
            probs[sb, j] = (e * inv).astype(BF16)
    for sb in subs:
        for j in pairs:
            o = jnp.dot(probs[sb, j], v_dup[j // 2][sb * blk:(sb + 2) * blk], preferred_element_type=F32)
            o_ref[0, sb * blk:(sb + 1) * blk, j * LANES:(j + 1) * LANES] = (
                jnp.where(lo_q, o[:blk], o[blk:]).astype(o_ref.dtype))


def _swa(p_att3, sinks, n_sub=4):
    b, s, _ = p_att3.shape
    blk = WINDOW
    tq = n_sub * blk
    kv_col = ATT_DIM // (2 * KV_DIM)
    return pl.pallas_call(
        _swa_kernel,
        grid=(b, s // tq),
        in_specs=[
            pl.BlockSpec(memory_space=pltpu.SMEM),
            pl.BlockSpec((1, tq, ATT_DIM), lambda bi, n: (bi, n, 0)),
            pl.BlockSpec((1, tq, 2 * KV_DIM), lambda bi, n: (bi, n, kv_col)),
            pl.BlockSpec((1, blk, 2 * KV_DIM),
                         lambda bi, n: (bi, jnp.maximum(n * n_sub - 1, 0), kv_col)),
        ],
        out_specs=pl.BlockSpec((1, tq, ATT_DIM), lambda bi, n: (bi, n, 0)),
        out_shape=jax.ShapeDtypeStruct((b, s, ATT_DIM), BF16),
        compiler_params=_params("parallel", "parallel"),
        name="swa_attention",
    )(sinks, p_att3, p_att3, p_att3)


def _conv_kernel(u_ref, halo_ref, w_ref, b_ref, lg_ref, lb_ref, o_ref, sh_ref, acc_ref):
    i = pl.program_id(1)
    ts = u_ref.shape[1]
    rb = 128
    sh_ref[0, 0:CONV_HALO, :] = jnp.where(i > 0, halo_ref[0], 0.0)
    sh_ref[0, CONV_HALO:, :] = u_ref[0]
    n_sh = ts + CONV_HALO - SUBLANES
    for q in range(1, SUBLANES):
        for cb in range(D_MODEL // LANES):
            cs = slice(cb * LANES, (cb + 1) * LANES)
            sh_ref[q, 0:n_sh, cs] = sh_ref[0, q:q + n_sh, cs]
    off = CONV_HALO - (CONV_WIDTH - 1)
    for cb in range(D_MODEL // LANES):
        cs = slice(cb * LANES, (cb + 1) * LANES)

        def rows(rblk, carry, cs=cs, qs=(), init=True):
            r0 = pl.multiple_of(rblk * rb, rb)
            acc = (jnp.zeros((rb, LANES), F32) + b_ref[:, cs]) if init else acc_ref[pl.ds(r0, rb), cs]
            for q in qs:
                taps = [j for j in range(CONV_WIDTH) if (off + j) % SUBLANES == q]
                a_max = (off + taps[-1]) // SUBLANES
                x = sh_ref[q, pl.ds(r0, rb + a_max * SUBLANES), cs]
                part = None
                for j in taps:
                    a = (off + j) // SUBLANES
                    term = w_ref[j:j + 1, cs] * x[a * SUBLANES:a * SUBLANES + rb]
                    part = term if part is None else part + term
                acc = acc + part
            acc_ref[pl.ds(r0, rb), cs] = acc
            return carry

        half = SUBLANES // 2
        lax.fori_loop(0, ts // rb, functools.partial(rows, qs=range(half), init=True), 0)
        lax.fori_loop(0, ts // rb, functools.partial(rows, qs=range(half, SUBLANES), init=False), 0)
    y = acc_ref[...]
    mean = jnp.mean(y, axis=-1, keepdims=True)
    yc = y - mean
    var = jnp.mean(yc * yc, axis=-1, keepdims=True)
    yn = yc * lax.rsqrt(var + LN_EPS) * lg_ref[...] + lb_ref[...]
    o_ref[0] = (yn * jax.nn.sigmoid(yn)).astype(o_ref.dtype)


def _conv_ln_silu(u3, dw_w, dw_b, ln_g, ln_b, ts=512):
    b, s, _ = u3.shape
    vec = pl.BlockSpec((1, D_MODEL), lambda bi, i: (0, 0))
    return pl.pallas_call(
        _conv_kernel,
        grid=(b, s // ts),
        in_specs=[
            pl.BlockSpec((1, ts, D_MODEL), lambda bi, i: (bi, i, 0)),
            pl.BlockSpec((1, CONV_HALO, D_MODEL),
                         lambda bi, i: (bi, jnp.maximum(i * (ts // CONV_HALO) - 1, 0), 0)),
            pl.BlockSpec((CONV_WIDTH, D_MODEL), lambda bi, i: (0, 0)),
            vec, vec, vec,
        ],
        out_specs=pl.BlockSpec((1, ts, D_MODEL), lambda bi, i: (bi, i, 0)),
        out_shape=jax.ShapeDtypeStruct((b, s, D_MODEL), BF16),
        scratch_shapes=[pltpu.VMEM((SUBLANES, ts + CONV_HALO, D_MODEL), F32),
                        pltpu.VMEM((ts, D_MODEL), F32)],
        compiler_params=_params("parallel", "parallel"),
        name="conv_ln_silu",
    )(u3, u3, dw_w, dw_b, ln_g, ln_b)


def _proj_mlp_kernel(*refs, n_proj, final_norm):
    x_ref = refs[0]
    y_refs = refs[1:1 + n_proj]
    wp_refs = refs[1 + n_proj:1 + 2 * n_proj]
    pb_ref, g_ref, w1_ref, w2_ref, gf_ref, o_ref, x1_ref, h_ref, acc_ref = refs[1 + 2 * n_proj:]
    j = pl.program_id(1)
    last = pl.num_programs(1) - 1
    tm = x_ref.shape[0]
    all_rows = slice(0, tm)
    halves = (slice(0, tm // 2), slice(tm // 2, tm))

    def residual_in(rows):
        x1 = x_ref[rows, :] + pb_ref[...]
        for y_ref, wp_ref in zip(y_refs, wp_refs):
            x1 = x1 + jnp.dot(y_ref[rows, :], wp_ref[...], preferred_element_type=F32)
        x1_ref[rows, :] = x1
        h_ref[rows, :] = _rms(x1, g_ref[...]).astype(BF16)

    def hidden(rows, w1):
        hid = jnp.dot(h_ref[rows, :], w1, preferred_element_type=F32)
        return jnp.square(jnp.maximum(hid, 0.0)).astype(BF16)

    def residual_out(rows, mlp):
        out = x1_ref[rows, :] + mlp
        if final_norm:
            out = _rms(out, gf_ref[...])
        o_ref[rows, :] = out

    @pl.when(j == 0)
    def _():
        w1, w2 = w1_ref[...].astype(BF16), w2_ref[...].astype(BF16)
        for rows in halves:
            residual_in(rows)
        for rows in halves:
            acc_ref[rows, :] = jnp.dot(hidden(rows, w1), w2, preferred_element_type=F32)

    @pl.when((j > 0) & (j < last))
    def _():
        hid = hidden(all_rows, w1_ref[...].astype(BF16))
        acc_ref[...] += jnp.dot(hid, w2_ref[...].astype(BF16), preferred_element_type=F32)

    @pl.when(j == last)
    def _():
        w2 = w2_ref[...].astype(BF16)
        hid = hidden(all_rows, w1_ref[...].astype(BF16))
        for rows in halves:
            residual_out(rows, acc_ref[rows, :] + jnp.dot(hid[rows], w2, preferred_element_type=F32))


def _proj_mlp(x2, ys, wps, pb, g, w1_all, w2_all, layer, gf, final_norm, tm=1024, tf=1024):
    t = x2.shape[0]
    n_proj = len(ys)
    vec = pl.BlockSpec((1, D_MODEL), lambda i, j: (0, 0))
    in_specs = [pl.BlockSpec((tm, D_MODEL), lambda i, j: (i, 0))]
    in_specs += [pl.BlockSpec((tm, y.shape[1]), lambda i, j: (i, 0)) for y in ys]
    in_specs += [pl.BlockSpec(w.shape, lambda i, j: (0, 0)) for w in wps]
    in_specs += [vec, vec,
                 pl.BlockSpec((None, D_MODEL, tf), lambda i, j: (layer, 0, j)),
                 pl.BlockSpec((None, tf, D_MODEL), lambda i, j: (layer, j, 0)),
                 vec]
    return pl.pallas_call(
        functools.partial(_proj_mlp_kernel, n_proj=n_proj, final_norm=final_norm),
        grid=(t // tm, D_FF // tf),
        in_specs=in_specs,
        out_specs=pl.BlockSpec((tm, D_MODEL), lambda i, j: (i, 0)),
        out_shape=jax.ShapeDtypeStruct((t, D_MODEL), F32),
        scratch_shapes=[pltpu.VMEM((tm, D_MODEL), F32),
                        pltpu.VMEM((tm, D_MODEL), BF16),
                        pltpu.VMEM((tm, D_MODEL), F32)],
        compiler_params=_params("parallel", "arbitrary"),
        name="proj_mlp",
    )(x2, *ys, *wps, pb, g, w1_all, w2_all, gf)


def _row(v):
    return v.reshape(1, -1).astype(F32)


def _hybrid_front(x3, g_mix, w_in, mu, w0, w_up, a0, a_up, g_up):
    c3 = 3 * RWKV_DIM
    w = jnp.concatenate([w_in[:, c3:SHIFT_DIM], w_in[:, :c3], w_in[:, SHIFT_DIM:]], axis=1).astype(BF16)
    mu_p = jnp.concatenate([mu[c3:], mu[:c3]]).reshape(1, -1).astype(F32)
    w_lora = jnp.zeros((LORA_DIM, c3), F32)
    w_lora = w_lora.at[:LORA_W, :RWKV_DIM].set(w_up)
    w_lora = w_lora.at[LORA_W:LORA_W + LORA_A, RWKV_DIM:2 * RWKV_DIM].set(a_up)
    w_lora = w_lora.at[LORA_W + LORA_A:, 2 * RWKV_DIM:].set(g_up)
    bias = jnp.concatenate([w0, a0, jnp.zeros((RWKV_DIM,), F32)]).reshape(1, -1)
    return _in_proj(x3, g_mix, w, mu_p, w_lora.astype(BF16), bias)


def kernel(x, norm_mix_g, norm_ffn_g, final_norm_g, hy_w_in, hy_mu, hy_w0, hy_w_up, hy_a0, hy_a_up, hy_g_up, hy_k_k, hy_k_a, hy_r_k, hy_gn_g, hy_gn_b, hy_sinks, hy_w_out, cv_pw1_w, cv_pw1_b, cv_dw_w, cv_dw_b, cv_ln_g, cv_ln_b, cv_pw2_w, cv_pw2_b, mlp_w1, mlp_w2):
    bsz, seq, d = x.shape
    depth = norm_mix_g.shape[0]
    t = bsz * seq
    x2 = x.reshape(t, d)
    zero_row = jnp.zeros((1, D_MODEL), F32)
    gf = _row(final_norm_g)
    w1_all, w2_all = mlp_w1, mlp_w2

    for layer in range(depth):
        i = layer // 2
        g_mix = _row(norm_mix_g[layer])
        if layer % 2 == 0:
            r, k, v, lw, a, gate, p_att = _hybrid_front(
                x2.reshape(bsz, seq, d), g_mix, hy_w_in[i], hy_mu[i], hy_w0[i], hy_w_up[i],
                hy_a0[i], hy_a_up[i], hy_g_up[i])
            y_rwkv = _rwkv_scan(r, k, v, lw, a, gate, _row(hy_k_k[i]), _row(hy_k_a[i]),
                                _row(hy_r_k[i]), _row(hy_gn_g[i]), _row(hy_gn_b[i]))
            y_att = _swa(p_att, hy_sinks[i].astype(F32))
            w_out = hy_w_out[i].astype(BF16)
            ys = [y_rwkv.reshape(t, RWKV_DIM), y_att.reshape(t, ATT_DIM)]
            wps = [w_out[:RWKV_DIM], w_out[RWKV_DIM:]]
            pb = zero_row
        else:
            u = _norm_glu(x2, g_mix, cv_pw1_w[i].astype(BF16), _row(cv_pw1_b[i]))
            u = _conv_ln_silu(u.reshape(bsz, seq, D_MODEL), cv_dw_w[i].astype(F32),
                              _row(cv_dw_b[i]), _row(cv_ln_g[i]), _row(cv_ln_b[i]))
            ys = [u.reshape(t, D_MODEL)]
            wps = [cv_pw2_w[i].astype(BF16)]
            pb = _row(cv_pw2_b[i])
        x2 = _proj_mlp(x2, ys, wps, pb, _row(norm_ffn_g[layer]), w1_all, w2_all, layer, gf,
                       final_norm=(layer == depth - 1))
    return x2.reshape(bsz, seq, d)
```

```python
import functools

import jax
import jax.numpy as jnp
from jax import lax
from jax.experimental import pallas as pl
from jax.experimental.pallas import tpu as pltpu

D_MODEL = 1024
HEAD_DIM = 64
RWKV_DIM = 512
ATT_Q_HEADS = 8
ATT_DIM = 512
KV_DIM = 128
LORA_W = 64
LORA_A = 64
LORA_G = 128
SHIFT_DIM = 3 * RWKV_DIM + LORA_W + LORA_A + LORA_G
ATT_COLS = ATT_DIM + 2 * KV_DIM
WINDOW = 128
CONV_WIDTH = 31
CONV_HALO = 32
D_FF = 4 * D_MODEL
RMS_EPS = 1e-6
LN_EPS = 1e-5
GN_EPS = 64e-5

LANES = 128
SUBLANES = 8
CHUNK = 64
SCAN_SLOTS = 4
VMEM_LIMIT = 56 * 1024 * 1024

F32 = jnp.float32
BF16 = jnp.bfloat16
NN = (((1,), (0,)), ((), ()))
NT = (((1,), (1,)), ((), ()))
TN = (((0,), (0,)), ((), ()))


def _params(*sem):
    return pltpu.CompilerParams(dimension_semantics=sem, vmem_limit_bytes=VMEM_LIMIT)


def _rms(x, g):
    return x * lax.rsqrt(jnp.mean(x * x, axis=-1, keepdims=True) + RMS_EPS) * g


LORA_DIM = LORA_W + LORA_A + LORA_G


def _in_proj_kernel(x_ref, g_ref, w_ref, mu_ref, wl_ref, bias_ref,
                    r_ref, k_ref, v_ref, lw_ref, a_ref, gate_ref, att_ref, prev_ref):
    tm = x_ref.shape[1]
    c1, c2, c3 = RWKV_DIM, 2 * RWKV_DIM, 3 * RWKV_DIM

    @pl.when(pl.program_id(1) == 0)
    def _():
        prev_ref[...] = jnp.zeros_like(prev_ref)

    def shift_lerp(p, cs):
        first = lax.broadcasted_iota(jnp.int32, p.shape, 0) == 0
        shifted = jnp.where(first, prev_ref[SUBLANES - 1:SUBLANES, cs], pltpu.roll(p, 1, axis=0))
        prev_ref[:, cs] = p[tm - SUBLANES:, :]
        return p + (shifted - p) * mu_ref[:, cs]

    h = _rms(x_ref[0], g_ref[...]).astype(BF16)
    z = jnp.dot(h, w_ref[:, :LORA_DIM], preferred_element_type=F32)
    z = shift_lerp(z, slice(0, LORA_DIM))
    lane = lax.broadcasted_iota(jnp.int32, z.shape, 1)
    act = jnp.where(lane < LORA_W, jnp.tanh(z),
                    jnp.where(lane < LORA_W + LORA_A, z, jax.nn.sigmoid(z)))
    lo = jnp.dot(act.astype(BF16), wl_ref[...], preferred_element_type=F32) + bias_ref[...]
    rkv = jnp.dot(h, w_ref[:, LORA_DIM:SHIFT_DIM], preferred_element_type=F32)
    att_ref[0] = jnp.dot(h, w_ref[:, SHIFT_DIM:], preferred_element_type=F32)
    rkv = shift_lerp(rkv, slice(LORA_DIM, SHIFT_DIM))
    r_ref[0] = rkv[:, :c1]
    k_ref[0] = rkv[:, c1:c2]
    v_ref[0] = rkv[:, c2:c3]
    wpre = lo[:, :c1]
    softplus = jnp.maximum(-wpre, 0.0) + jnp.log(1.0 + jnp.exp(-jnp.abs(wpre)))
    lw_ref[0] = -jnp.exp(-softplus - 0.5)
    a_ref[0] = jax.nn.sigmoid(lo[:, c1:c2])
    gate_ref[0] = lo[:, c2:]


def _in_proj(x3, g, w, mu, w_lora, bias, tm=512):
    b, s, _ = x3.shape
    rspec = pl.BlockSpec((1, tm, RWKV_DIM), lambda bi, i: (bi, i, 0))
    rshape = jax.ShapeDtypeStruct((b, s, RWKV_DIM), F32)
    const = lambda shape: pl.BlockSpec(shape, lambda bi, i: (0, 0))
    return pl.pallas_call(
        _in_proj_kernel,
        grid=(b, s // tm),
        in_specs=[
            pl.BlockSpec((1, tm, D_MODEL), lambda bi, i: (bi, i, 0)),
            const((1, D_MODEL)), const(w.shape), const((1, SHIFT_DIM)),
            const(w_lora.shape), const((1, 3 * RWKV_DIM)),
        ],
        out_specs=[rspec] * 6 + [pl.BlockSpec((1, tm, ATT_COLS), lambda bi, i: (bi, i, 0))],
        out_shape=[rshape] * 6 + [jax.ShapeDtypeStruct((b, s, ATT_COLS), F32)],
        scratch_shapes=[pltpu.VMEM((SUBLANES, SHIFT_DIM), F32)],
        compiler_params=_params("parallel", "arbitrary"),
        name="in_proj",
    )(x3, g, w, mu, w_lora, bias)


def _norm_glu_kernel(x_ref, g_ref, w_ref, b_ref, o_ref):
    h = _rms(x_ref[...], g_ref[...]).astype(BF16)
    acc = jnp.dot(h, w_ref[...], preferred_element_type=F32) + b_ref[...]
    o_ref[...] = acc[:, :D_MODEL] * jax.nn.sigmoid(acc[:, D_MODEL:])


def _norm_glu(x2, g, w, b, tm=1024):
    t = x2.shape[0]
    return pl.pallas_call(
        _norm_glu_kernel,
        grid=(t // tm,),
        in_specs=[
            pl.BlockSpec((tm, D_MODEL), lambda i: (i, 0)),
            pl.BlockSpec((1, D_MODEL), lambda i: (0, 0)),
            pl.BlockSpec((D_MODEL, 2 * D_MODEL), lambda i: (0, 0)),
            pl.BlockSpec((1, 2 * D_MODEL), lambda i: (0, 0)),
        ],
        out_specs=pl.BlockSpec((tm, D_MODEL), lambda i: (i, 0)),
        out_shape=jax.ShapeDtypeStruct((t, D_MODEL), F32),
        compiler_params=_params("parallel"),
        name="norm_glu",
    )(x2, g, w, b)


def _split3(x):
    hi = x.astype(BF16)
    r1 = x - hi.astype(F32)
    mid = r1.astype(BF16)
    lo = (r1 - mid.astype(F32)).astype(BF16)
    return hi, mid, lo


def _rwkv_scan_kernel(r_ref, k_ref, v_ref, lw_ref, a_ref, g_ref,
                      kk_ref, ka_ref, rk_ref, gg_ref, gb_ref, o_ref, state_ref,
                      atrt_ref, btkt_ref, bhkh_ref, vbf_ref, decay_ref, bonus_ref):
    c_len = CHUNK
    n2 = 2 * c_len
    nb, ts = r_ref.shape[0], r_ref.shape[1]
    n_pairs = RWKV_DIM // LANES
    lane = lax.broadcasted_iota(jnp.int32, (c_len, LANES), 1)
    lo = lane < HEAD_DIM
    lo2 = lax.broadcasted_iota(jnp.int32, (n2, LANES), 1) < HEAD_DIM
    top = lax.broadcasted_iota(jnp.int32, (n2, LANES), 0) < c_len
    own = lo2 == top

    def stack(x):
        return jnp.concatenate([jnp.where(lo, x, 0.0), jnp.where(lo, 0.0, x)], axis=0)

    ri = lax.broadcasted_iota(jnp.int32, (n2, n2), 0)
    ci = lax.broadcasted_iota(jnp.int32, (n2, n2), 1)
    strict = ri > ci
    incl = ri >= ci
    eye = (ri == ci).astype(F32)
    blk = [(ri >> l) == (ci >> l) for l in range(7)]
    tri = (lax.broadcasted_iota(jnp.int32, (c_len, c_len), 0)
           >= lax.broadcasted_iota(jnp.int32, (c_len, c_len), 1)).astype(BF16)

    @pl.when(pl.program_id(1) == 0)
    def _():
        state_ref[...] = jnp.zeros_like(state_ref)

    units = [(bb, j) for bb in range(nb) for j in range(n_pairs)]
    pairs = range(len(units))

    def mm(a, b, dims=NN):
        return lax.dot_general(a.astype(BF16), b.astype(BF16), dims, preferred_element_type=F32)

    def exact_zero(vregs):
        while len(vregs) > 1:
            vregs = [a + b for a, b in zip(vregs[::2], vregs[1::2])] + vregs[len(vregs) & ~1:]
        return (jnp.clip(vregs[0], -1.0, 1.0) * 0.0)[0:1]

    def prep(c, slot):
        sl = pl.ds(pl.multiple_of(c * c_len, c_len), c_len)
        lw_all, cum_all = [], []
        for bb in range(nb):
            lw_b = lw_ref[bb, sl, :]
            cum3 = jnp.dot(tri, jnp.concatenate(_split3(lw_b), axis=1), preferred_element_type=F32)
            lw_all.append(lw_b)
            cum_all.append(cum3[:, :RWKV_DIM] + cum3[:, RWKV_DIM:2 * RWKV_DIM] + cum3[:, 2 * RWKV_DIM:])

        folded = []
        for idx, (bb, j) in enumerate(units):
            cs = slice(j * LANES, (j + 1) * LANES)
            cum, lw = cum_all[bb][:, cs], lw_all[bb][:, cs]
            cum_last = cum[c_len - 1:c_len, :]
            r_s, k_s, v_s = stack(r_ref[bb, sl, cs]), stack(k_ref[bb, sl, cs]), stack(v_ref[bb, sl, cs])
            a = a_ref[bb, sl, cs]

            def both(x):
                return jnp.concatenate([x, x], axis=0)

            kk = k_s * kk_ref[:, cs]
            kk = kk * lax.rsqrt(jnp.maximum(jnp.sum(kk * kk, axis=-1, keepdims=True), 1e-24))
            kmod = k_s * both(1.0 + (a - 1.0) * ka_ref[:, cs])
            bvec = kk * both(a)
            einv = both(jnp.exp(-cum))
            edec = both(jnp.exp(cum_last - cum))
            at_rt = jnp.concatenate([kk * both(-jnp.exp(cum - lw)), r_s * both(jnp.exp(cum))], axis=0)
            bh_kh = jnp.concatenate([bvec * edec, kmod * edec], axis=0)
            bonus = jnp.sum(r_s * kmod * rk_ref[:, cs], axis=-1, keepdims=True) * v_s
            bonus = bonus[:c_len] + bonus[c_len:]
            atrt_ref[slot, idx] = at_rt.astype(BF16)
            btkt_ref[slot, idx] = jnp.concatenate([bvec * einv, kmod * einv], axis=0).astype(BF16)
            bhkh_ref[slot, idx] = bh_kh.astype(BF16)
            vbf_ref[slot, idx] = v_s.astype(BF16)
            decay_ref[slot, idx] = jnp.exp(cum_last)
            bonus_ref[slot, idx] = bonus
            for x in (at_rt, bh_kh, bonus):
                folded.extend(x[r:r + SUBLANES] for r in range(0, x.shape[0], SUBLANES))
        return exact_zero(folded)

    def inverse_part(slots, between=None, zero=None):
        def tick():
            if between is not None:
                next(between, None)

        chains = [(slot, j) for slot in slots for j in pairs]
        lhs = [atrt_ref[slot, j] for slot, j in chains]
        if zero is not None:
            lhs[0] = lhs[0] + zero.astype(BF16)
        gram = [mm(lhs[i], btkt_ref[slot, j], NT) for i, (slot, j) in enumerate(chains)]
        tick()
        a_ab =[jnp.where(strict, g[:n2, :n2], 0.0).astype(BF16) for g in gram]
        a_ak = [jnp.where(strict, g[:n2, n2:], 0.0).astype(BF16) for g in gram]
        r_b = [jnp.where(incl, g[n2:, :n2], 0.0).astype(BF16) for g in gram]
        r_k = [jnp.where(incl, g[n2:, n2:], 0.0).astype(BF16) for g in gram]
        tinv = [(eye + jnp.where(strict & blk[1], g[:n2, :n2], 0.0)).astype(BF16) for g in gram]
        for lvl in range(2, 7):
            off = blk[lvl] & ~blk[lvl - 1]
            w = [jnp.where(off, mm(a, t), 0.0).astype(BF16) for a, t in zip(a_ab, tinv)]
            tinv = [t + mm(t, wi).astype(BF16) for t, wi in zip(tinv, w)]
            tick()
        akv = [mm(a_ak[i], vbf_ref[slot, j]) for i, (slot, j) in enumerate(chains)]
        if between is not None:
            for _ in between:
                pass
        n = len(pairs)
        return {slot: (tinv[k * n:(k + 1) * n], akv[k * n:(k + 1) * n], r_b[k * n:(k + 1) * n],
                       r_k[k * n:(k + 1) * n]) for k, slot in enumerate(slots)}

    def state_stages(c, slot, tinv, akv, r_b, r_k, zero=None):
        sl = pl.ds(pl.multiple_of(c * c_len, c_len), c_len)
        at_rt = [atrt_ref[slot, j] for j in pairs]
        v_bf = [vbf_ref[slot, j] for j in pairs]
        state = [state_ref[j] for j in pairs]
        if zero is not None:
            state[0] = state[0] + zero
        s_t = [mm(at_rt[j], state[j], NT) for j in pairs]
        yield
        rhs = [s_t[j][:n2] + akv[j] for j in pairs]
        rhs_hi = [x.astype(BF16) for x in rhs]
        rhs_lo = [(x - h.astype(F32)).astype(BF16) for x, h in zip(rhs, rhs_hi)]
        u = [mm(tinv[j], rhs_hi[j]) + mm(tinv[j], rhs_lo[j]) for j in pairs]
        u_bf = [u[j].astype(BF16) for j in pairs]
        yield
        for j in pairs:
            state_ref[j] = state[j] * decay_ref[slot, j] + mm(
                jnp.concatenate([u_bf[j], v_bf[j]], axis=0), bhkh_ref[slot, j], TN)
        y = [s_t[j][n2:] + mm(r_b[j], u_bf[j]) + mm(r_k[j], v_bf[j]) for j in pairs]
        yield
        inv_n = 1.0 / HEAD_DIM
        for idx, (bb, j) in enumerate(units):
            cs = slice(j * LANES, (j + 1) * LANES)
            mean = jnp.sum(y[idx], axis=-1, keepdims=True) * inv_n
            yc = jnp.where(own, y[idx] - mean, 0.0)
            var = jnp.sum(yc * yc, axis=-1, keepdims=True) * inv_n
            yn = yc * lax.rsqrt(var + GN_EPS)
            yn = yn[:c_len] + yn[c_len:]
            out = yn * gg_ref[:, cs] + gb_ref[:, cs] + bonus_ref[slot, idx]
            o_ref[bb, sl, cs] = (out * g_ref[bb, sl, cs]).astype(o_ref.dtype)

    def state_of(c, slots, inv, zero=None):
        for k, slot in enumerate(slots):
            yield from state_stages(c + k, slot, *inv[slot], zero=zero if k == 0 else None)

    n_chunks = ts // c_len
    prep(0, 0)
    prep(1, 1)

    def body(i, carry):
        c0 = 4 * i
        zs = [prep(c0 + 2 + k, 2 + k) for k in range(2)]
        inv_x = inverse_part((0, 1))
        inv_y = inverse_part((2, 3), between=state_of(c0, (0, 1), inv_x))
        zs = [prep(jnp.minimum(c0 + 4 + k, n_chunks - 1), k) for k in range(2)]
        for _ in state_of(c0 + 2, (2, 3), inv_y, zero=zs[0] + zs[1]):
            pass
        return carry

    lax.fori_loop(0, n_chunks // 4, body, 0)


def _rwkv_scan(r, k, v, lw, a, g, kkp, kap, rkp, ggp, gbp, ts=512, nb=2):
    b, s, _ = r.shape
    n_units = nb * RWKV_DIM // LANES
    xspec = pl.BlockSpec((nb, ts, RWKV_DIM), lambda bi, i: (bi, i, 0))
    pspec = pl.BlockSpec((1, RWKV_DIM), lambda bi, i: (0, 0))
    return pl.pallas_call(
        _rwkv_scan_kernel,
        grid=(b // nb, s // ts),
        in_specs=[xspec] * 6 + [pspec] * 5,
        out_specs=xspec,
        out_shape=jax.ShapeDtypeStruct((b, s, RWKV_DIM), BF16),
        scratch_shapes=[
            pltpu.VMEM((n_units, LANES, LANES), F32),
            pltpu.VMEM((SCAN_SLOTS, n_units, 4 * CHUNK, LANES), BF16),
            pltpu.VMEM((SCAN_SLOTS, n_units, 4 * CHUNK, LANES), BF16),
            pltpu.VMEM((SCAN_SLOTS, n_units, 4 * CHUNK, LANES), BF16),
            pltpu.VMEM((SCAN_SLOTS, n_units, 2 * CHUNK, LANES), BF16),
            pltpu.VMEM((SCAN_SLOTS, n_units, 1, LANES), F32),
            pltpu.VMEM((SCAN_SLOTS, n_units, CHUNK, LANES), F32),
        ],
        compiler_params=_params("parallel", "arbitrary"),
        name="rwkv_scan",
    )(r, k, v, lw, a, g, kkp, kap, rkp, ggp, gbp)


def _swa_kernel(sink_ref, q_ref, kvc_ref, kvp_ref, o_ref):
    n = pl.program_id(1)
    blk = WINDOW
    n_sub = q_ref.shape[1] // blk
    kv_all = jnp.concatenate([kvp_ref[0], kvc_ref[0]], axis=0)
    kmat, vmat = kv_all[:, :KV_DIM], kv_all[:, KV_DIM:]
    lo_kv = lax.broadcasted_iota(jnp.int32, kmat.shape, 1) < HEAD_DIM
    k_sw = pltpu.roll(kmat, HEAD_DIM, axis=1)
    v_sw = pltpu.roll(vmat, HEAD_DIM, axis=1)
    k_dup = [jnp.where(lo_kv, kmat, k_sw).astype(BF16), jnp.where(lo_kv, k_sw, kmat).astype(BF16)]
    v_dup = [jnp.where(lo_kv, vmat, v_sw).astype(BF16), jnp.where(lo_kv, v_sw, vmat).astype(BF16)]

    lo_q = lax.broadcasted_iota(jnp.int32, (blk, LANES), 1) < HEAD_DIM
    rows = lax.broadcasted_iota(jnp.int32, (2 * blk, 2 * blk), 0)
    cols = lax.broadcasted_iota(jnp.int32, (2 * blk, 2 * blk), 1)
    rel = (rows & (blk - 1)) + blk - cols
    band = (rel >= 0) & (rel < WINDOW)
    has_prev = (cols >= blk) | (n > 0)
    relf = rel.astype(F32)
    first = lax.broadcasted_iota(jnp.int32, (2 * blk, 1), 0) < blk

    pairs = range(ATT_Q_HEADS // 2)
    subs = range(n_sub)
    scores = {}
    for sb in subs:
        for j in pairs:
            qp = q_ref[0, sb * blk:(sb + 1) * blk, j * LANES:(j + 1) * LANES] * (HEAD_DIM ** -0.5)
            qs = jnp.concatenate([jnp.where(lo_q, qp, 0.0), jnp.where(lo_q, 0.0, qp)], axis=0)
            keys = k_dup[j // 2][sb * blk:(sb + 2) * blk]
            scores[sb, j] = lax.dot_general(qs.astype(BF16), keys, NT, preferred_element_type=F32)
    probs = {}
    for j in pairs:
        slope = jnp.where(first, 2.0 ** -(2 * j + 1), 2.0 ** -(2 * j + 2))
        sink = jnp.where(first, sink_ref[2 * j], sink_ref[2 * j + 1])
        bias = jnp.where(band, -slope * relf, -jnp.inf)
        for sb in subs:
            s = scores[sb, j] + bias
            if sb == 0:
                s = jnp.where(has_prev, s, -jnp.inf)
            m = jnp.maximum(jnp.max(s, axis=-1, keepdims=True), sink)
            e = jnp.exp(s - m)
            inv = 1.0 / (jnp.sum(e, axis=-1, keepdims=True) + jnp.exp(sink - m))
            probs[sb, j] = (e * inv).astype(BF16)
    for sb in subs:
        for j in pairs:
            o = jnp.dot(probs[sb, j], v_dup[j // 2][sb * blk:(sb + 2) * blk], preferred_element_type=F32)
            o_ref[0, sb * blk:(sb + 1) * blk, j * LANES:(j + 1) * LANES] = (
                jnp.where(lo_q, o[:blk], o[blk:]).astype(o_ref.dtype))


def _swa(p_att3, sinks, n_sub=4):
    b, s, _ = p_att3.shape
    blk = WINDOW
    tq = n_sub * blk
    kv_col = ATT_DIM // (2 * KV_DIM)
    return pl.pallas_call(
        _swa_kernel,
        grid=(b, s // tq),
        in_specs=[
            pl.BlockSpec(memory_space=pltpu.SMEM),
            pl.BlockSpec((1, tq, ATT_DIM), lambda bi, n: (bi, n, 0)),
            pl.BlockSpec((1, tq, 2 * KV_DIM), lambda bi, n: (bi, n, kv_col)),
            pl.BlockSpec((1, blk, 2 * KV_DIM),
                         lambda bi, n: (bi, jnp.maximum(n * n_sub - 1, 0), kv_col)),
        ],
        out_specs=pl.BlockSpec((1, tq, ATT_DIM), lambda bi, n: (bi, n, 0)),
        out_shape=jax.ShapeDtypeStruct((b, s, ATT_DIM), BF16),
        compiler_params=_params("parallel", "parallel"),
        name="swa_attention",
    )(sinks, p_att3, p_att3, p_att3)


def _conv_kernel(u_ref, halo_ref, w_ref, b_ref, lg_ref, lb_ref, o_ref, sh_ref, acc_ref):
    i = pl.program_id(1)
    ts = u_ref.shape[1]
    rb = 128
    sh_ref[0, 0:CONV_HALO, :] = jnp.where(i > 0, halo_ref[0], 0.0)
    sh_ref[0, CONV_HALO:, :] = u_ref[0]
    n_sh = ts + CONV_HALO - SUBLANES
    for q in range(1, SUBLANES):
        for cb in range(D_MODEL // LANES):
            cs = slice(cb * LANES, (cb + 1) * LANES)
            sh_ref[q, 0:n_sh, cs] = sh_ref[0, q:q + n_sh, cs]
    off = CONV_HALO - (CONV_WIDTH - 1)
    for cb in range(D_MODEL // LANES):
        cs = slice(cb * LANES, (cb + 1) * LANES)

        def rows(rblk, carry, cs=cs, qs=(), init=True):
            r0 = pl.multiple_of(rblk * rb, rb)
            acc = (jnp.zeros((rb, LANES), F32) + b_ref[:, cs]) if init else acc_ref[pl.ds(r0, rb), cs]
            for q in qs:
                taps = [j for j in range(CONV_WIDTH) if (off + j) % SUBLANES == q]
                a_max = (off + taps[-1]) // SUBLANES
                x = sh_ref[q, pl.ds(r0, rb + a_max * SUBLANES), cs]
                part = None
                for j in taps:
                    a = (off + j) // SUBLANES
                    term = w_ref[j:j + 1, cs] * x[a * SUBLANES:a * SUBLANES + rb]
                    part = term if part is None else part + term
                acc = acc + part
            acc_ref[pl.ds(r0, rb), cs] = acc
            return carry

        half = SUBLANES // 2
        lax.fori_loop(0, ts // rb, functools.partial(rows, qs=range(half), init=True), 0)
        lax.fori_loop(0, ts // rb, functools.partial(rows, qs=range(half, SUBLANES), init=False), 0)
    y = acc_ref[...]
    mean = jnp.mean(y, axis=-1, keepdims=True)
    yc = y - mean
    var = jnp.mean(yc * yc, axis=-1, keepdims=True)
    yn = yc * lax.rsqrt(var + LN_EPS) * lg_ref[...] + lb_ref[...]
    o_ref[0] = (yn * jax.nn.sigmoid(yn)).astype(o_ref.dtype)


def _conv_ln_silu(u3, dw_w, dw_b, ln_g, ln_b, ts=512):
    b, s, _ = u3.shape
    vec = pl.BlockSpec((1, D_MODEL), lambda bi, i: (0, 0))
    return pl.pallas_call(
        _conv_kernel,
        grid=(b, s // ts),
        in_specs=[
            pl.BlockSpec((1, ts, D_MODEL), lambda bi, i: (bi, i, 0)),
            pl.BlockSpec((1, CONV_HALO, D_MODEL),
                         lambda bi, i: (bi, jnp.maximum(i * (ts // CONV_HALO) - 1, 0), 0)),
            pl.BlockSpec((CONV_WIDTH, D_MODEL), lambda bi, i: (0, 0)),
            vec, vec, vec,
        ],
        out_specs=pl.BlockSpec((1, ts, D_MODEL), lambda bi, i: (bi, i, 0)),
        out_shape=jax.ShapeDtypeStruct((b, s, D_MODEL), BF16),
        scratch_shapes=[pltpu.VMEM((SUBLANES, ts + CONV_HALO, D_MODEL), F32),
                        pltpu.VMEM((ts, D_MODEL), F32)],
        compiler_params=_params("parallel", "parallel"),
        name="conv_ln_silu",
    )(u3, u3, dw_w, dw_b, ln_g, ln_b)


def _proj_mlp_kernel(*refs, n_proj, final_norm):
    x_ref = refs[0]
    y_refs = refs[1:1 + n_proj]
    wp_refs = refs[1 + n_proj:1 + 2 * n_proj]
    pb_ref, g_ref, w1_ref, w2_ref, gf_ref, o_ref, x1_ref, h_ref, acc_ref = refs[1 + 2 * n_proj:]
    j = pl.program_id(1)
    last = pl.num_programs(1) - 1
    tm = x_ref.shape[0]
    all_rows = slice(0, tm)
    halves = (slice(0, tm // 2), slice(tm // 2, tm))

    def residual_in(rows):
        x1 = x_ref[rows, :] + pb_ref[...]
        for y_ref, wp_ref in zip(y_refs, wp_refs):
            x1 = x1 + jnp.dot(y_ref[rows, :], wp_ref[...], preferred_element_type=F32)
        x1_ref[rows, :] = x1
        h_ref[rows, :] = _rms(x1, g_ref[...]).astype(BF16)

    def hidden(rows, w1):
        hid = jnp.dot(h_ref[rows, :], w1, preferred_element_type=F32)
        return jnp.square(jnp.maximum(hid, 0.0)).astype(BF16)

    def residual_out(rows, mlp):
        out = x1_ref[rows, :] + mlp
        if final_norm:
            out = _rms(out, gf_ref[...])
        o_ref[rows, :] = out

    @pl.when(j == 0)
    def _():
        w1, w2 = w1_ref[...].astype(BF16), w2_ref[...].astype(BF16)
        for rows in halves:
            residual_in(rows)
        for rows in halves:
            acc_ref[rows, :] = jnp.dot(hidden(rows, w1), w2, preferred_element_type=F32)

    @pl.when((j > 0) & (j < last))
    def _():
        hid = hidden(all_rows, w1_ref[...].astype(BF16))
        acc_ref[...] += jnp.dot(hid, w2_ref[...].astype(BF16), preferred_element_type=F32)

    @pl.when(j == last)
    def _():
        w2 = w2_ref[...].astype(BF16)
        hid = hidden(all_rows, w1_ref[...].astype(BF16))
        for rows in halves:
            residual_out(rows, acc_ref[rows, :] + jnp.dot(hid[rows], w2, preferred_element_type=F32))


def _proj_mlp(x2, ys, wps, pb, g, w1_all, w2_all, layer, gf, final_norm, tm=1024, tf=1024):
    t = x2.shape[0]
    n_proj = len(ys)
    vec = pl.BlockSpec((1, D_MODEL), lambda i, j: (0, 0))
    in_specs = [pl.BlockSpec((tm, D_MODEL), lambda i, j: (i, 0))]
    in_specs += [pl.BlockSpec((tm, y.shape[1]), lambda i, j: (i, 0)) for y in ys]
    in_specs += [pl.BlockSpec(w.shape, lambda i, j: (0, 0)) for w in wps]
    in_specs += [vec, vec,
                 pl.BlockSpec((None, D_MODEL, tf), lambda i, j: (layer, 0, j)),
                 pl.BlockSpec((None, tf, D_MODEL), lambda i, j: (layer, j, 0)),
                 vec]
    return pl.pallas_call(
        functools.partial(_proj_mlp_kernel, n_proj=n_proj, final_norm=final_norm),
        grid=(t // tm, D_FF // tf),
        in_specs=in_specs,
        out_specs=pl.BlockSpec((tm, D_MODEL), lambda i, j: (i, 0)),
        out_shape=jax.ShapeDtypeStruct((t, D_MODEL), F32),
        scratch_shapes=[pltpu.VMEM((tm, D_MODEL), F32),
                        pltpu.VMEM((tm, D_MODEL), BF16),
                        pltpu.VMEM((tm, D_MODEL), F32)],
        compiler_params=_params("parallel", "arbitrary"),
        name="proj_mlp",
    )(x2, *ys, *wps, pb, g, w1_all, w2_all, gf)


def _row(v):
    return v.reshape(1, -1).astype(F32)


def _hybrid_front(x3, g_mix, w_in, mu, w0, w_up, a0, a_up, g_up):
    c3 = 3 * RWKV_DIM
    w = jnp.concatenate([w_in[:, c3:SHIFT_DIM], w_in[:, :c3], w_in[:, SHIFT_DIM:]], axis=1).astype(BF16)
    mu_p = jnp.concatenate([mu[c3:], mu[:c3]]).reshape(1, -1).astype(F32)
    w_lora = jnp.zeros((LORA_DIM, c3), F32)
    w_lora = w_lora.at[:LORA_W, :RWKV_DIM].set(w_up)
    w_lora = w_lora.at[LORA_W:LORA_W + LORA_A, RWKV_DIM:2 * RWKV_DIM].set(a_up)
    w_lora = w_lora.at[LORA_W + LORA_A:, 2 * RWKV_DIM:].set(g_up)
    bias = jnp.concatenate([w0, a0, jnp.zeros((RWKV_DIM,), F32)]).reshape(1, -1)
    return _in_proj(x3, g_mix, w, mu_p, w_lora.astype(BF16), bias)


def kernel(x, norm_mix_g, norm_ffn_g, final_norm_g, hy_w_in, hy_mu, hy_w0, hy_w_up, hy_a0, hy_a_up, hy_g_up, hy_k_k, hy_k_a, hy_r_k, hy_gn_g, hy_gn_b, hy_sinks, hy_w_out, cv_pw1_w, cv_pw1_b, cv_dw_w, cv_dw_b, cv_ln_g, cv_ln_b, cv_pw2_w, cv_pw2_b, mlp_w1, mlp_w2):
    bsz, seq, d = x.shape
    depth = norm_mix_g.shape[0]
    t = bsz * seq
    x2 = x.reshape(t, d)
    zero_row = jnp.zeros((1, D_MODEL), F32)
    gf = _row(final_norm_g)
    w1_all, w2_all = mlp_w1, mlp_w2

    for layer in range(depth):
        i = layer // 2
        g_mix = _row(norm_mix_g[layer])
        if layer % 2 == 0:
            r, k, v, lw, a, gate, p_att = _hybrid_front(
                x2.reshape(bsz, seq, d), g_mix, hy_w_in[i], hy_mu[i], hy_w0[i], hy_w_up[i],
                hy_a0[i], hy_a_up[i], hy_g_up[i])
            y_rwkv = _rwkv_scan(r, k, v, lw, a, gate, _row(hy_k_k[i]), _row(hy_k_a[i]),
                                _row(hy_r_k[i]), _row(hy_gn_g[i]), _row(hy_gn_b[i]))
            y_att = _swa(p_att, hy_sinks[i].astype(F32))
            w_out = hy_w_out[i].astype(BF16)
            ys = [y_rwkv.reshape(t, RWKV_DIM), y_att.reshape(t, ATT_DIM)]
            wps = [w_out[:RWKV_DIM], w_out[RWKV_DIM:]]
            pb = zero_row
        else:
            u = _norm_glu(x2, g_mix, cv_pw1_w[i].astype(BF16), _row(cv_pw1_b[i]))
            u = _conv_ln_silu(u.reshape(bsz, seq, D_MODEL), cv_dw_w[i].astype(F32),
                              _row(cv_dw_b[i]), _row(cv_ln_g[i]), _row(cv_ln_b[i]))
            ys = [u.reshape(t, D_MODEL)]
            wps = [cv_pw2_w[i].astype(BF16)]
            pb = _row(cv_pw2_b[i])
        x2 = _proj_mlp(x2, ys, wps, pb, _row(norm_ffn_g[layer]), w1_all, w2_all, layer, gf,
                       final_norm=(layer == depth - 1))
    return x2.reshape(bsz, seq, d)
```

```python
import functools

import jax
import jax.numpy as jnp
from jax import lax
from jax.experimental import pallas as pl
from jax.experimental.pallas import tpu as pltpu

D_MODEL = 1024
HEAD_DIM = 64
RWKV_DIM = 512
ATT_Q_HEADS = 8
ATT_DIM = 512
KV_DIM = 128
LORA_W = 64
LORA_A = 64
LORA_G = 128
SHIFT_DIM = 3 * RWKV_DIM + LORA_W + LORA_A + LORA_G
ATT_COLS = ATT_DIM + 2 * KV_DIM
WINDOW = 128
CONV_WIDTH = 31
CONV_HALO = 32
D_FF = 4 * D_MODEL
RMS_EPS = 1e-6
LN_EPS = 1e-5
GN_EPS = 64e-5

LANES = 128
SUBLANES = 8
CHUNK = 64
SCAN_SLOTS = 4
VMEM_LIMIT = 56 * 1024 * 1024

F32 = jnp.float32
BF16 = jnp.bfloat16
NN = (((1,), (0,)), ((), ()))
NT = (((1,), (1,)), ((), ()))
TN = (((0,), (0,)), ((), ()))


def _params(*sem):
    return pltpu.CompilerParams(dimension_semantics=sem, vmem_limit_bytes=VMEM_LIMIT)


def _rms(x, g):
    return x * lax.rsqrt(jnp.mean(x * x, axis=-1, keepdims=True) + RMS_EPS) * g


LORA_DIM = LORA_W + LORA_A + LORA_G


def _in_proj_kernel(x_ref, g_ref, w_ref, mu_ref, wl_ref, bias_ref,
                    r_ref, k_ref, v_ref, lw_ref, a_ref, gate_ref, att_ref, prev_ref):
    tm = x_ref.shape[1]
    c1, c2, c3 = RWKV_DIM, 2 * RWKV_DIM, 3 * RWKV_DIM

    @pl.when(pl.program_id(1) == 0)
    def _():
        prev_ref[...] = jnp.zeros_like(prev_ref)

    def shift_lerp(p, cs):
        first = lax.broadcasted_iota(jnp.int32, p.shape, 0) == 0
        shifted = jnp.where(first, prev_ref[SUBLANES - 1:SUBLANES, cs], pltpu.roll(p, 1, axis=0))
        prev_ref[:, cs] = p[tm - SUBLANES:, :]
        return p + (shifted - p) * mu_ref[:, cs]

    h = _rms(x_ref[0], g_ref[...]).astype(BF16)
    z = jnp.dot(h, w_ref[:, :LORA_DIM], preferred_element_type=F32)
    z = shift_lerp(z, slice(0, LORA_DIM))
    lane = lax.broadcasted_iota(jnp.int32, z.shape, 1)
    act = jnp.where(lane < LORA_W, jnp.tanh(z),
                    jnp.where(lane < LORA_W + LORA_A, z, jax.nn.sigmoid(z)))
    lo = jnp.dot(act.astype(BF16), wl_ref[...], preferred_element_type=F32) + bias_ref[...]
    rkv = jnp.dot(h, w_ref[:, LORA_DIM:SHIFT_DIM], preferred_element_type=F32)
    att_ref[0] = jnp.dot(h, w_ref[:, SHIFT_DIM:], preferred_element_type=F32)
    rkv = shift_lerp(rkv, slice(LORA_DIM, SHIFT_DIM))
    r_ref[0] = rkv[:, :c1]
    k_ref[0] = rkv[:, c1:c2]
    v_ref[0] = rkv[:, c2:c3]
    wpre = lo[:, :c1]
    softplus = jnp.maximum(-wpre, 0.0) + jnp.log(1.0 + jnp.exp(-jnp.abs(wpre)))
    lw_ref[0] = -jnp.exp(-softplus - 0.5)
    a_ref[0] = jax.nn.sigmoid(lo[:, c1:c2])
    gate_ref[0] = lo[:, c2:]


def _in_proj(x3, g, w, mu, w_lora, bias, tm=512):
    b, s, _ = x3.shape
    assert s % tm == 0 and tm % SUBLANES == 0
    rspec = pl.BlockSpec((1, tm, RWKV_DIM), lambda bi, i: (bi, i, 0))
    rshape = jax.ShapeDtypeStruct((b, s, RWKV_DIM), F32)
    const = lambda shape: pl.BlockSpec(shape, lambda bi, i: (0, 0))
    return pl.pallas_call(
        _in_proj_kernel,
        grid=(b, s // tm),
        in_specs=[
            pl.BlockSpec((1, tm, D_MODEL), lambda bi, i: (bi, i, 0)),
            const((1, D_MODEL)), const(w.shape), const((1, SHIFT_DIM)),
            const(w_lora.shape), const((1, 3 * RWKV_DIM)),
        ],
        out_specs=[rspec] * 6 + [pl.BlockSpec((1, tm, ATT_COLS), lambda bi, i: (bi, i, 0))],
        out_shape=[rshape] * 6 + [jax.ShapeDtypeStruct((b, s, ATT_COLS), F32)],
        scratch_shapes=[pltpu.VMEM((SUBLANES, SHIFT_DIM), F32)],
        compiler_params=_params("parallel", "arbitrary"),
        name="in_proj",
    )(x3, g, w, mu, w_lora, bias)


def _norm_glu_kernel(x_ref, g_ref, w_ref, b_ref, o_ref):
    h = _rms(x_ref[...], g_ref[...]).astype(BF16)
    acc = jnp.dot(h, w_ref[...], preferred_element_type=F32) + b_ref[...]
    o_ref[...] = acc[:, :D_MODEL] * jax.nn.sigmoid(acc[:, D_MODEL:])


def _norm_glu(x2, g, w, b, tm=1024):
    t = x2.shape[0]
    assert t % tm == 0
    return pl.pallas_call(
        _norm_glu_kernel,
        grid=(t // tm,),
        in_specs=[
            pl.BlockSpec((tm, D_MODEL), lambda i: (i, 0)),
            pl.BlockSpec((1, D_MODEL), lambda i: (0, 0)),
            pl.BlockSpec((D_MODEL, 2 * D_MODEL), lambda i: (0, 0)),
            pl.BlockSpec((1, 2 * D_MODEL), lambda i: (0, 0)),
        ],
        out_specs=pl.BlockSpec((tm, D_MODEL), lambda i: (i, 0)),
        out_shape=jax.ShapeDtypeStruct((t, D_MODEL), F32),
        compiler_params=_params("parallel"),
        name="norm_glu",
    )(x2, g, w, b)


def _split3(x):
    hi = x.astype(BF16)
    r1 = x - hi.astype(F32)
    mid = r1.astype(BF16)
    lo = (r1 - mid.astype(F32)).astype(BF16)
    return hi, mid, lo


def _rwkv_scan_kernel(r_ref, k_ref, v_ref, lw_ref, a_ref, g_ref,
                      kk_ref, ka_ref, rk_ref, gg_ref, gb_ref, o_ref, state_ref,
                      atrt_ref, btkt_ref, bhkh_ref, vbf_ref, decay_ref, bonus_ref):
    c_len = CHUNK
    n2 = 2 * c_len
    nb, ts = r_ref.shape[0], r_ref.shape[1]
    n_pairs = RWKV_DIM // LANES
    lane = lax.broadcasted_iota(jnp.int32, (c_len, LANES), 1)
    lo = lane < HEAD_DIM
    lo2 = lax.broadcasted_iota(jnp.int32, (n2, LANES), 1) < HEAD_DIM
    top = lax.broadcasted_iota(jnp.int32, (n2, LANES), 0) < c_len
    own = lo2 == top

    def stack(x):
        return jnp.concatenate([jnp.where(lo, x, 0.0), jnp.where(lo, 0.0, x)], axis=0)

    ri = lax.broadcasted_iota(jnp.int32, (n2, n2), 0)
    ci = lax.broadcasted_iota(jnp.int32, (n2, n2), 1)
    strict = ri > ci
    incl = ri >= ci
    eye = (ri == ci).astype(F32)
    blk = [(ri >> l) == (ci >> l) for l in range(7)]
    tri = (lax.broadcasted_iota(jnp.int32, (c_len, c_len), 0)
           >= lax.broadcasted_iota(jnp.int32, (c_len, c_len), 1)).astype(BF16)

    @pl.when(pl.program_id(1) == 0)
    def _():
        state_ref[...] = jnp.zeros_like(state_ref)

    units = [(bb, j) for bb in range(nb) for j in range(n_pairs)]
    pairs = range(len(units))

    def mm(a, b, dims=NN):
        return lax.dot_general(a.astype(BF16), b.astype(BF16), dims, preferred_element_type=F32)

    def exact_zero(vregs):
        while len(vregs) > 1:
            vregs = [a + b for a, b in zip(vregs[::2], vregs[1::2])] + vregs[len(vregs) & ~1:]
        return (jnp.clip(vregs[0], -1.0, 1.0) * 0.0)[0:1]

    def prep(c, slot):
        sl = pl.ds(pl.multiple_of(c * c_len, c_len), c_len)
        lw_all, cum_all = [], []
        for bb in range(nb):
            lw_b = lw_ref[bb, sl, :]
            cum3 = jnp.dot(tri, jnp.concatenate(_split3(lw_b), axis=1), preferred_element_type=F32)
            lw_all.append(lw_b)
            cum_all.append(cum3[:, :RWKV_DIM] + cum3[:, RWKV_DIM:2 * RWKV_DIM] + cum3[:, 2 * RWKV_DIM:])

        folded = []
        for idx, (bb, j) in enumerate(units):
            cs = slice(j * LANES, (j + 1) * LANES)
            cum, lw = cum_all[bb][:, cs], lw_all[bb][:, cs]
            cum_last = cum[c_len - 1:c_len, :]
            r_s, k_s, v_s = stack(r_ref[bb, sl, cs]), stack(k_ref[bb, sl, cs]), stack(v_ref[bb, sl, cs])
            a = a_ref[bb, sl, cs]

            def both(x):
                return jnp.concatenate([x, x], axis=0)

            kk = k_s * kk_ref[:, cs]
            kk = kk * lax.rsqrt(jnp.maximum(jnp.sum(kk * kk, axis=-1, keepdims=True), 1e-24))
            kmod = k_s * both(1.0 + (a - 1.0) * ka_ref[:, cs])
            bvec = kk * both(a)
            einv = both(jnp.exp(-cum))
            edec = both(jnp.exp(cum_last - cum))
            at_rt = jnp.concatenate([kk * both(-jnp.exp(cum - lw)), r_s * both(jnp.exp(cum))], axis=0)
            bh_kh = jnp.concatenate([bvec * edec, kmod * edec], axis=0)
            bonus = jnp.sum(r_s * kmod * rk_ref[:, cs], axis=-1, keepdims=True) * v_s
            bonus = bonus[:c_len] + bonus[c_len:]
            atrt_ref[slot, idx] = at_rt.astype(BF16)
            btkt_ref[slot, idx] = jnp.concatenate([bvec * einv, kmod * einv], axis=0).astype(BF16)
            bhkh_ref[slot, idx] = bh_kh.astype(BF16)
            vbf_ref[slot, idx] = v_s.astype(BF16)
            decay_ref[slot, idx] = jnp.exp(cum_last)
            bonus_ref[slot, idx] = bonus
            for x in (at_rt, bh_kh, bonus):
                folded.extend(x[r:r + SUBLANES] for r in range(0, x.shape[0], SUBLANES))
        return exact_zero(folded)

    def inverse_part(slots, between=None, zero=None):
        def tick():
            if between is not None:
                next(between, None)

        chains = [(slot, j) for slot in slots for j in pairs]
        lhs = [atrt_ref[slot, j] for slot, j in chains]
        if zero is not None:
            lhs[0] = lhs[0] + zero.astype(BF16)
        gram = [mm(lhs[i], btkt_ref[slot, j], NT) for i, (slot, j) in enumerate(chains)]
        tick()
        a_ab = [jnp.where(strict, g[:n2, :n2], 0.0).astype(BF16) for g in gram]
        a_ak = [jnp.where(strict, g[:n2, n2:], 0.0).astype(BF16) for g in gram]
        r_b = [jnp.where(incl, g[n2:, :n2], 0.0).astype(BF16) for g in gram]
        r_k = [jnp.where(incl, g[n2:, n2:], 0.0).astype(BF16) for g in gram]
        tinv = [(eye + jnp.where(strict & blk[1], g[:n2, :n2], 0.0)).astype(BF16) for g in gram]
        for lvl in range(2, 7):
            off = blk[lvl] & ~blk[lvl - 1]
            w = [jnp.where(off, mm(a, t), 0.0).astype(BF16) for a, t in zip(a_ab, tinv)]
            tinv = [t + mm(t, wi).astype(BF16) for t, wi in zip(tinv, w)]
            tick()
        akv = [mm(a_ak[i], vbf_ref[slot, j]) for i, (slot, j) in enumerate(chains)]
        if between is not None:
            for _ in between:
                pass
        n = len(pairs)
        return {slot: (tinv[k * n:(k + 1) * n], akv[k * n:(k + 1) * n], r_b[k * n:(k + 1) * n],
                       r_k[k * n:(k + 1) * n]) for k, slot in enumerate(slots)}

    def state_stages(c, slot, tinv, akv, r_b, r_k, zero=None):
        sl = pl.ds(pl.multiple_of(c * c_len, c_len), c_len)
        at_rt = [atrt_ref[slot, j] for j in pairs]
        v_bf = [vbf_ref[slot, j] for j in pairs]
        state = [state_ref[j] for j in pairs]
        if zero is not None:
            state[0] = state[0] + zero
        s_t = [mm(at_rt[j], state[j], NT) for j in pairs]
        yield
        rhs = [s_t[j][:n2] + akv[j] for j in pairs]
        rhs_hi = [x.astype(BF16) for x in rhs]
        rhs_lo = [(x - h.astype(F32)).astype(BF16) for x, h in zip(rhs, rhs_hi)]
        u = [mm(tinv[j], rhs_hi[j]) + mm(tinv[j], rhs_lo[j]) for j in pairs]
        u_bf = [u[j].astype(BF16) for j in pairs]
        yield
        for j in pairs:
            state_ref[j] = state[j] * decay_ref[slot, j] + mm(
                jnp.concatenate([u_bf[j], v_bf[j]], axis=0), bhkh_ref[slot, j], TN)
        y = [s_t[j][n2:] + mm(r_b[j], u_bf[j]) + mm(r_k[j], v_bf[j]) for j in pairs]
        yield
        inv_n = 1.0 / HEAD_DIM
        for idx, (bb, j) in enumerate(units):
            cs = slice(j * LANES, (j + 1) * LANES)
            mean = jnp.sum(y[idx], axis=-1, keepdims=True) * inv_n
            yc = jnp.where(own, y[idx] - mean, 0.0)
            var = jnp.sum(yc * yc, axis=-1, keepdims=True) * inv_n
            yn = yc * lax.rsqrt(var + GN_EPS)
            yn = yn[:c_len] + yn[c_len:]
            out = yn * gg_ref[:, cs] + gb_ref[:, cs] + bonus_ref[slot, idx]
            o_ref[bb, sl, cs] = (out * g_ref[bb, sl, cs]).astype(o_ref.dtype)

    def state_of(c, slots, inv, zero=None):
        for k, slot in enumerate(slots):
            yield from state_stages(c + k, slot, *inv[slot], zero=zero if k == 0 else None)

    n_chunks = ts // c_len
    prep(0, 0)
    prep(1, 1)

    def body(i, carry):
        c0 = 4 * i
        zs = [prep(c0 + 2 + k, 2 + k) for k in range(2)]
        inv_x = inverse_part((0, 1))
        inv_y = inverse_part((2, 3), between=state_of(c0, (0, 1), inv_x))
        zs = [prep(jnp.minimum(c0 + 4 + k, n_chunks - 1), k) for k in range(2)]
        for _ in state_of(c0 + 2, (2, 3), inv_y, zero=zs[0] + zs[1]):
            pass
        return carry

    lax.fori_loop(0, n_chunks // 4, body, 0)


def _rwkv_scan(r, k, v, lw, a, g, kkp, kap, rkp, ggp, gbp, ts=512, nb=2):
    b, s, _ = r.shape
    assert b % nb == 0 and s % ts == 0 and (ts // CHUNK) % SCAN_SLOTS == 0
    n_units = nb * RWKV_DIM // LANES
    xspec = pl.BlockSpec((nb, ts, RWKV_DIM), lambda bi, i: (bi, i, 0))
    pspec = pl.BlockSpec((1, RWKV_DIM), lambda bi, i: (0, 0))
    return pl.pallas_call(
        _rwkv_scan_kernel,
        grid=(b // nb, s // ts),
        in_specs=[xspec] * 6 + [pspec] * 5,
        out_specs=xspec,
        out_shape=jax.ShapeDtypeStruct((b, s, RWKV_DIM), BF16),
        scratch_shapes=[
            pltpu.VMEM((n_units, LANES, LANES), F32),
            pltpu.VMEM((SCAN_SLOTS, n_units, 4 * CHUNK, LANES), BF16),
            pltpu.VMEM((SCAN_SLOTS, n_units, 4 * CHUNK, LANES), BF16),
            pltpu.VMEM((SCAN_SLOTS, n_units, 4 * CHUNK, LANES), BF16),
            pltpu.VMEM((SCAN_SLOTS, n_units, 2 * CHUNK, LANES), BF16),
            pltpu.VMEM((SCAN_SLOTS, n_units, 1, LANES), F32),
            pltpu.VMEM((SCAN_SLOTS, n_units, CHUNK, LANES), F32),
        ],
        compiler_params=_params("parallel", "arbitrary"),
        name="rwkv_scan",
    )(r, k, v, lw, a, g, kkp, kap, rkp, ggp, gbp)


def _swa_kernel(sink_ref, q_ref, kvc_ref, kvp_ref, o_ref):
    n = pl.program_id(1)
    blk = WINDOW
    n_sub = q_ref.shape[1] // blk
    kv_all = jnp.concatenate([kvp_ref[0], kvc_ref[0]], axis=0)
    kmat, vmat = kv_all[:, :KV_DIM], kv_all[:, KV_DIM:]
    lo_kv = lax.broadcasted_iota(jnp.int32, kmat.shape, 1) < HEAD_DIM
    k_sw = pltpu.roll(kmat, HEAD_DIM, axis=1)
    v_sw = pltpu.roll(vmat, HEAD_DIM, axis=1)
    k_dup = [jnp.where(lo_kv, kmat, k_sw).astype(BF16), jnp.where(lo_kv, k_sw, kmat).astype(BF16)]
    v_dup = [jnp.where(lo_kv, vmat, v_sw).astype(BF16), jnp.where(lo_kv, v_sw, vmat).astype(BF16)]

    lo_q = lax.broadcasted_iota(jnp.int32, (blk, LANES), 1) < HEAD_DIM
    rows = lax.broadcasted_iota(jnp.int32, (2 * blk, 2 * blk), 0)
    cols = lax.broadcasted_iota(jnp.int32, (2 * blk, 2 * blk), 1)
    rel = (rows & (blk - 1)) + blk - cols
    band = (rel >= 0) & (rel < WINDOW)
    has_prev = (cols >= blk) | (n > 0)
    relf = rel.astype(F32)
    first = lax.broadcasted_iota(jnp.int32, (2 * blk, 1), 0) < blk

    pairs = range(ATT_Q_HEADS // 2)
    subs = range(n_sub)
    scores = {}
    for sb in subs:
        for j in pairs:
            qp = q_ref[0, sb * blk:(sb + 1) * blk, j * LANES:(j + 1) * LANES] * (HEAD_DIM ** -0.5)
            qs = jnp.concatenate([jnp.where(lo_q, qp, 0.0), jnp.where(lo_q, 0.0, qp)], axis=0)
            keys = k_dup[j // 2][sb * blk:(sb + 2) * blk]
            scores[sb, j] = lax.dot_general(qs.astype(BF16), keys, NT, preferred_element_type=F32)
    probs, inv = {}, {}
    for j in pairs:
        slope = jnp.where(first, 2.0 ** -(2 * j + 1), 2.0 ** -(2 * j + 2))
        sink = jnp.where(first, sink_ref[2 * j], sink_ref[2 * j + 1])
        bias = jnp.where(band, -slope * relf, -jnp.inf)
        for sb in subs:
            s = scores[sb, j] + bias
            if sb == 0:
                s = jnp.where(has_prev, s, -jnp.inf)
            m = jnp.maximum(jnp.max(s, axis=-1, keepdims=True), sink)
            e = jnp.exp(s - m)
            inv[sb, j] = 1.0 / (jnp.sum(e, axis=-1, keepdims=True) + jnp.exp(sink - m))
            probs[sb, j] = e.astype(BF16)
    for sb in subs:
        for j in pairs:
            o = jnp.dot(probs[sb, j], v_dup[j // 2][sb * blk:(sb + 2) * blk], preferred_element_type=F32)
            o = o * inv[sb, j]
            o_ref[0, sb * blk:(sb + 1) * blk, j * LANES:(j + 1) * LANES] = (
                jnp.where(lo_q, o[:blk], o[blk:]).astype(o_ref.dtype))


def _swa(p_att3, sinks, n_sub=4):
    b, s, _ = p_att3.shape
    blk = WINDOW
    tq = n_sub * blk
    assert s % tq == 0
    kv_col = ATT_DIM // (2 * KV_DIM)
    return pl.pallas_call(
        _swa_kernel,
        grid=(b, s // tq),
        in_specs=[
            pl.BlockSpec(memory_space=pltpu.SMEM),
            pl.BlockSpec((1, tq, ATT_DIM), lambda bi, n: (bi, n, 0)),
            pl.BlockSpec((1, tq, 2 * KV_DIM), lambda bi, n: (bi, n, kv_col)),
            pl.BlockSpec((1, blk, 2 * KV_DIM),
                         lambda bi, n: (bi, jnp.maximum(n * n_sub - 1, 0), kv_col)),
        ],
        out_specs=pl.BlockSpec((1, tq, ATT_DIM), lambda bi, n: (bi, n, 0)),
        out_shape=jax.ShapeDtypeStruct((b, s, ATT_DIM), BF16),
        compiler_params=_params("parallel", "parallel"),
        name="swa_attention",
    )(sinks, p_att3, p_att3, p_att3)


def _conv_kernel(u_ref, halo_ref, w_ref, b_ref, lg_ref, lb_ref, o_ref, sh_ref, acc_ref):
    i = pl.program_id(1)
    ts = u_ref.shape[1]
    rb = 128
    sh_ref[0, 0:CONV_HALO, :] = jnp.where(i > 0, halo_ref[0], 0.0)
    sh_ref[0, CONV_HALO:, :] = u_ref[0]
    n_sh = ts + CONV_HALO - SUBLANES
    for q in range(1, SUBLANES):
        for cb in range(D_MODEL // LANES):
            cs = slice(cb * LANES, (cb + 1) * LANES)
            sh_ref[q, 0:n_sh, cs] = sh_ref[0, q:q + n_sh, cs]
    off = CONV_HALO - (CONV_WIDTH - 1)
    for cb in range(D_MODEL // LANES):
        cs = slice(cb * LANES, (cb + 1) * LANES)

        def rows(rblk, carry, cs=cs, qs=(), init=True):
            r0 = pl.multiple_of(rblk * rb, rb)
            acc = (jnp.zeros((rb, LANES), F32) + b_ref[:, cs]) if init else acc_ref[pl.ds(r0, rb), cs]
            for q in qs:
                taps = [j for j in range(CONV_WIDTH) if (off + j) % SUBLANES == q]
                a_max = (off + taps[-1]) // SUBLANES
                x = sh_ref[q, pl.ds(r0, rb + a_max * SUBLANES), cs]
                part = None
                for j in taps:
                    a = (off + j) // SUBLANES
                    term = w_ref[j:j + 1, cs] * x[a * SUBLANES:a * SUBLANES + rb]
                    part = term if part is None else part + term
                acc = acc + part
            acc_ref[pl.ds(r0, rb), cs] = acc
            return carry

        half = SUBLANES // 2
        lax.fori_loop(0, ts // rb, functools.partial(rows, qs=range(half), init=True), 0)
        lax.fori_loop(0, ts // rb, functools.partial(rows, qs=range(half, SUBLANES), init=False), 0)
    y = acc_ref[...]
    mean = jnp.mean(y, axis=-1, keepdims=True)
    yc = y - mean
    var = jnp.mean(yc * yc, axis=-1, keepdims=True)
    yn = yc * lax.rsqrt(var + LN_EPS) * lg_ref[...] + lb_ref[...]
    o_ref[0] = (yn * jax.nn.sigmoid(yn)).astype(o_ref.dtype)


def _conv_ln_silu(u3, dw_w, dw_b, ln_g, ln_b, ts=512):
    b, s, _ = u3.shape
    assert s % ts == 0 and ts % CONV_HALO == 0 and ts % 128 == 0
    vec = pl.BlockSpec((1, D_MODEL), lambda bi, i: (0, 0))
    return pl.pallas_call(
        _conv_kernel,
        grid=(b, s // ts),
        in_specs=[
            pl.BlockSpec((1, ts, D_MODEL), lambda bi, i: (bi, i, 0)),
            pl.BlockSpec((1, CONV_HALO, D_MODEL),
                         lambda bi, i: (bi, jnp.maximum(i * (ts // CONV_HALO) - 1, 0), 0)),
            pl.BlockSpec((CONV_WIDTH, D_MODEL), lambda bi, i: (0, 0)),
            vec, vec, vec,
        ],
        out_specs=pl.BlockSpec((1, ts, D_MODEL), lambda bi, i: (bi, i, 0)),
        out_shape=jax.ShapeDtypeStruct((b, s, D_MODEL), BF16),
        scratch_shapes=[pltpu.VMEM((SUBLANES, ts + CONV_HALO, D_MODEL), F32),
                        pltpu.VMEM((ts, D_MODEL), F32)],
        compiler_params=_params("parallel", "parallel"),
        name="conv_ln_silu",
    )(u3, u3, dw_w, dw_b, ln_g, ln_b)


def _proj_mlp_kernel(*refs, n_proj, final_norm):
    x_ref = refs[0]
    y_refs = refs[1:1 + n_proj]
    wp_refs = refs[1 + n_proj:1 + 2 * n_proj]
    pb_ref, g_ref, w1_ref, w2_ref, gf_ref, o_ref, x1_ref, h_ref, acc_ref = refs[1 + 2 * n_proj:]
    j = pl.program_id(1)
    last = pl.num_programs(1) - 1
    tm = x_ref.shape[0]
    all_rows = slice(0, tm)
    halves = (slice(0, tm // 2), slice(tm // 2, tm))

    def residual_in(rows):
        x1 = x_ref[rows, :] + pb_ref[...]
        for y_ref, wp_ref in zip(y_refs, wp_refs):
            x1 = x1 + jnp.dot(y_ref[rows, :], wp_ref[...], preferred_element_type=F32)
        x1_ref[rows, :] = x1
        h_ref[rows, :] = _rms(x1, g_ref[...]).astype(BF16)

    def hidden(rows, w1):
        hid = jnp.dot(h_ref[rows, :], w1, preferred_element_type=F32)
        return jnp.square(jnp.maximum(hid, 0.0)).astype(BF16)

    def residual_out(rows, mlp):
        out = x1_ref[rows, :] + mlp
        if final_norm:
            out = _rms(out, gf_ref[...])
        o_ref[rows, :] = out

    @pl.when(j == 0)
    def _():
        w1, w2 = w1_ref[...].astype(BF16), w2_ref[...].astype(BF16)
        for rows in halves:
            residual_in(rows)
        for rows in halves:
            acc_ref[rows, :] = jnp.dot(hidden(rows, w1), w2, preferred_element_type=F32)

    @pl.when((j > 0) & (j < last))
    def _():
        hid = hidden(all_rows, w1_ref[...].astype(BF16))
        acc_ref[...] += jnp.dot(hid, w2_ref[...].astype(BF16), preferred_element_type=F32)

    @pl.when(j == last)
    def _():
        w2 = w2_ref[...].astype(BF16)
        hid = hidden(all_rows, w1_ref[...].astype(BF16))
        for rows in halves:
            residual_out(rows, acc_ref[rows, :] + jnp.dot(hid[rows], w2, preferred_element_type=F32))


def _proj_mlp(x2, ys, wps, pb, g, w1_all, w2_all, layer, gf, final_norm, tm=1024, tf=1024):
    t = x2.shape[0]
    n_proj = len(ys)
    assert t % tm == 0 and D_FF % tf == 0 and D_FF // tf >= 2
    vec = pl.BlockSpec((1, D_MODEL), lambda i, j: (0, 0))
    in_specs = [pl.BlockSpec((tm, D_MODEL), lambda i, j: (i, 0))]
    in_specs += [pl.BlockSpec((tm, y.shape[1]), lambda i, j: (i, 0)) for y in ys]
    in_specs += [pl.BlockSpec(w.shape, lambda i, j: (0, 0)) for w in wps]
    in_specs += [vec, vec,
                 pl.BlockSpec((None, D_MODEL, tf), lambda i, j: (layer, 0, j)),
                 pl.BlockSpec((None, tf, D_MODEL), lambda i, j: (layer, j, 0)),
                 vec]
    return pl.pallas_call(
        functools.partial(_proj_mlp_kernel, n_proj=n_proj, final_norm=final_norm),
        grid=(t // tm, D_FF // tf),
        in_specs=in_specs,
        out_specs=pl.BlockSpec((tm, D_MODEL), lambda i, j: (i, 0)),
        out_shape=jax.ShapeDtypeStruct((t, D_MODEL), F32),
        scratch_shapes=[pltpu.VMEM((tm, D_MODEL), F32),
                        pltpu.VMEM((tm, D_MODEL), BF16),
                        pltpu.VMEM((tm, D_MODEL), F32)],
        compiler_params=_params("parallel", "arbitrary"),
        name="proj_mlp",
    )(x2, *ys, *wps, pb, g, w1_all, w2_all, gf)


def _row(v):
    return v.reshape(1, -1).astype(F32)


def _hybrid_front(x3, g_mix, w_in, mu, w0, w_up, a0, a_up, g_up):
    c3 = 3 * RWKV_DIM
    w = jnp.concatenate([w_in[:, c3:SHIFT_DIM], w_in[:, :c3], w_in[:, SHIFT_DIM:]], axis=1).astype(BF16)
    mu_p = jnp.concatenate([mu[c3:], mu[:c3]]).reshape(1, -1).astype(F32)
    w_lora = jnp.zeros((LORA_DIM, c3), F32)
    w_lora = w_lora.at[:LORA_W, :RWKV_DIM].set(w_up)
    w_lora = w_lora.at[LORA_W:LORA_W + LORA_A, RWKV_DIM:2 * RWKV_DIM].set(a_up)
    w_lora = w_lora.at[LORA_W + LORA_A:, 2 * RWKV_DIM:].set(g_up)
    bias = jnp.concatenate([w0, a0, jnp.zeros((RWKV_DIM,), F32)]).reshape(1, -1)
    return _in_proj(x3, g_mix, w, mu_p, w_lora.astype(BF16), bias)


def kernel(x, norm_mix_g, norm_ffn_g, final_norm_g, hy_w_in, hy_mu, hy_w0, hy_w_up, hy_a0, hy_a_up, hy_g_up, hy_k_k, hy_k_a, hy_r_k, hy_gn_g, hy_gn_b, hy_sinks, hy_w_out, cv_pw1_w, cv_pw1_b, cv_dw_w, cv_dw_b, cv_ln_g, cv_ln_b, cv_pw2_w, cv_pw2_b, mlp_w1, mlp_w2):
    bsz, seq, d = x.shape
    depth = norm_mix_g.shape[0]
    t = bsz * seq
    x2 = x.reshape(t, d)
    zero_row = jnp.zeros((1, D_MODEL), F32)
    gf = _row(final_norm_g)
    w1_all, w2_all = mlp_w1, mlp_w2

    for layer in range(depth):
        i = layer // 2
        g_mix = _row(norm_mix_g[layer])
        if layer % 2 == 0:
            r, k, v, lw, a, gate, p_att = _hybrid_front(
                x2.reshape(bsz, seq, d), g_mix, hy_w_in[i], hy_mu[i], hy_w0[i], hy_w_up[i],
                hy_a0[i], hy_a_up[i], hy_g_up[i])
            y_rwkv = _rwkv_scan(r, k, v, lw, a, gate, _row(hy_k_k[i]), _row(hy_k_a[i]),
                                _row(hy_r_k[i]), _row(hy_gn_g[i]), _row(hy_gn_b[i]))
            y_att = _swa(p_att, hy_sinks[i].astype(F32))
            w_out = hy_w_out[i].astype(BF16)
            ys = [y_rwkv.reshape(t, RWKV_DIM), y_att.reshape(t, ATT_DIM)]
            wps = [w_out[:RWKV_DIM], w_out[RWKV_DIM:]]
            pb = zero_row
        else:
            u = _norm_glu(x2, g_mix, cv_pw1_w[i].astype(BF16), _row(cv_pw1_b[i]))
            u = _conv_ln_silu(u.reshape(bsz, seq, D_MODEL), cv_dw_w[i].astype(F32),
                              _row(cv_dw_b[i]), _row(cv_ln_g[i]), _row(cv_ln_b[i]))
            ys = [u.reshape(t, D_MODEL)]
            wps = [cv_pw2_w[i].astype(BF16)]
            pb = _row(cv_pw2_b[i])
        x2 = _proj_mlp(x2, ys, wps, pb, _row(norm_ffn_g[layer]), w1_all, w2_all, layer, gf,
                       final_norm=(layer == depth - 1))
    return x2.reshape(bsz, seq, d)
```

```python
import functools

import jax
import jax.numpy as jnp
from jax import lax
from jax.experimental import pallas as pl
from jax.experimental.pallas import tpu as pltpu

D_MODEL = 1024
HEAD_DIM = 64
RWKV_DIM = 512
ATT_Q_HEADS = 8
ATT_DIM = 512
KV_DIM = 128
LORA_W = 64
LORA_A = 64
LORA_G = 128
SHIFT_DIM = 3 * RWKV_DIM + LORA_W + LORA_A + LORA_G
ATT_COLS = ATT_DIM + 2 * KV_DIM
WINDOW = 128
CONV_WIDTH = 31
CONV_HALO = 32
D_FF = 4 * D_MODEL
RMS_EPS = 1e-6
LN_EPS = 1e-5
GN_EPS = 64e-5

LANES = 128
SUBLANES = 8
CHUNK = 64
SCAN_SLOTS = 4
VMEM_LIMIT = 56 * 1024 * 1024

F32 = jnp.float32
BF16 = jnp.bfloat16
NN = (((1,), (0,)), ((), ()))
NT = (((1,), (1,)), ((), ()))
TN = (((0,), (0,)), ((), ()))


def _params(*sem):
    return pltpu.CompilerParams(dimension_semantics=sem, vmem_limit_bytes=VMEM_LIMIT)


def _rms(x, g):
    return x * lax.rsqrt(jnp.mean(x * x, axis=-1, keepdims=True) + RMS_EPS) * g


LORA_DIM = LORA_W + LORA_A + LORA_G


def _in_proj_kernel(x_ref, g_ref, w_ref, mu_ref, wl_ref, bias_ref,
                    r_ref, k_ref, v_ref, lw_ref, a_ref, gate_ref, att_ref, prev_ref):
    tm = x_ref.shape[1]
    c1, c2, c3 = RWKV_DIM, 2 * RWKV_DIM, 3 * RWKV_DIM

    @pl.when(pl.program_id(1) == 0)
    def _():
        prev_ref[...] = jnp.zeros_like(prev_ref)

    def shift_lerp(p, cs):
        first = lax.broadcasted_iota(jnp.int32, p.shape, 0) == 0
        shifted = jnp.where(first, prev_ref[SUBLANES - 1:SUBLANES, cs], pltpu.roll(p, 1, axis=0))
        prev_ref[:, cs] = p[tm - SUBLANES:, :]
        return p + (shifted - p) * mu_ref[:, cs]

    h = _rms(x_ref[0], g_ref[...]).astype(BF16)
    z = jnp.dot(h, w_ref[:, :LORA_DIM], preferred_element_type=F32)
    z = shift_lerp(z, slice(0, LORA_DIM))
    lane = lax.broadcasted_iota(jnp.int32, z.shape, 1)
    act = jnp.where(lane < LORA_W, jnp.tanh(z),
                    jnp.where(lane < LORA_W + LORA_A, z, jax.nn.sigmoid(z)))
    lo = jnp.dot(act.astype(BF16), wl_ref[...], preferred_element_type=F32) + bias_ref[...]
    rkv = jnp.dot(h, w_ref[:, LORA_DIM:SHIFT_DIM], preferred_element_type=F32)
    att_ref[0] = jnp.dot(h, w_ref[:, SHIFT_DIM:], preferred_element_type=F32)
    rkv = shift_lerp(rkv, slice(LORA_DIM, SHIFT_DIM))
    def put(ref, val):
        for j in range(RWKV_DIM // LANES):
            ref[0, j] = val[:, j * LANES:(j + 1) * LANES]

    put(r_ref, rkv[:, :c1])
    put(k_ref, rkv[:, c1:c2])
    put(v_ref, rkv[:, c2:c3])
    wpre = lo[:, :c1]
    softplus = jnp.maximum(-wpre, 0.0) + jnp.log(1.0 + jnp.exp(-jnp.abs(wpre)))
    put(lw_ref, -jnp.exp(-softplus - 0.5))
    put(a_ref, jax.nn.sigmoid(lo[:, c1:c2]))
    put(gate_ref, lo[:, c2:])


def _in_proj(x3, g, w, mu, w_lora, bias, tm=512):
    b, s, _ = x3.shape
    assert s % tm == 0 and tm % SUBLANES == 0
    rspec = pl.BlockSpec((1, RWKV_DIM // LANES, tm, LANES), lambda bi, i: (bi, 0, i, 0))
    rshape = jax.ShapeDtypeStruct((b, RWKV_DIM // LANES, s, LANES), F32)
    const = lambda shape: pl.BlockSpec(shape, lambda bi, i: (0, 0))
    return pl.pallas_call(
        _in_proj_kernel,
        grid=(b, s // tm),
        in_specs=[
            pl.BlockSpec((1, tm, D_MODEL), lambda bi, i: (bi, i, 0)),
            const((1, D_MODEL)), const(w.shape), const((1, SHIFT_DIM)),
            const(w_lora.shape), const((1, 3 * RWKV_DIM)),
        ],
        out_specs=[rspec] * 6 + [pl.BlockSpec((1, tm, ATT_COLS), lambda bi, i: (bi, i, 0))],
        out_shape=[rshape] * 6 + [jax.ShapeDtypeStruct((b, s, ATT_COLS), F32)],
        scratch_shapes=[pltpu.VMEM((SUBLANES, SHIFT_DIM), F32)],
        compiler_params=_params("parallel", "arbitrary"),
        name="in_proj",
    )(x3, g, w, mu, w_lora, bias)


def _norm_glu_kernel(x_ref, g_ref, w_ref, b_ref, o_ref):
    h = _rms(x_ref[...], g_ref[...]).astype(BF16)
    acc = jnp.dot(h, w_ref[...], preferred_element_type=F32) + b_ref[...]
    o_ref[...] = acc[:, :D_MODEL] * jax.nn.sigmoid(acc[:, D_MODEL:])


def _norm_glu(x2, g, w, b, tm=1024):
    t = x2.shape[0]
    assert t % tm == 0
    return pl.pallas_call(
        _norm_glu_kernel,
        grid=(t // tm,),
        in_specs=[
            pl.BlockSpec((tm, D_MODEL), lambda i: (i, 0)),
            pl.BlockSpec((1, D_MODEL), lambda i: (0, 0)),
            pl.BlockSpec((D_MODEL, 2 * D_MODEL), lambda i: (0, 0)),
            pl.BlockSpec((1, 2 * D_MODEL), lambda i: (0, 0)),
        ],
        out_specs=pl.BlockSpec((tm, D_MODEL), lambda i: (i, 0)),
        out_shape=jax.ShapeDtypeStruct((t, D_MODEL), F32),
        compiler_params=_params("parallel"),
        name="norm_glu",
    )(x2, g, w, b)


def _split3(x):
    hi = x.astype(BF16)
    r1 = x - hi.astype(F32)
    mid = r1.astype(BF16)
    lo = (r1 - mid.astype(F32)).astype(BF16)
    return hi, mid, lo


def _rwkv_scan_kernel(r_ref, k_ref, v_ref, lw_ref, a_ref, g_ref,
                      kk_ref, ka_ref, rk_ref, gg_ref, gb_ref, o_ref, state_ref,
                      atrt_ref, btkt_ref, bhkh_ref, vbf_ref, decay_ref, bonus_ref):
    c_len = CHUNK
    n2 = 2 * c_len
    nb, ts = r_ref.shape[0], r_ref.shape[2]
    n_pairs = RWKV_DIM // LANES
    lane = lax.broadcasted_iota(jnp.int32, (c_len, LANES), 1)
    lo = lane < HEAD_DIM
    lo2 = lax.broadcasted_iota(jnp.int32, (n2, LANES), 1) < HEAD_DIM
    top = lax.broadcasted_iota(jnp.int32, (n2, LANES), 0) < c_len
    own = lo2 == top

    def stack(x):
        return jnp.concatenate([jnp.where(lo, x, 0.0), jnp.where(lo, 0.0, x)], axis=0)

    ri = lax.broadcasted_iota(jnp.int32, (n2, n2), 0)
    ci = lax.broadcasted_iota(jnp.int32, (n2, n2), 1)
    strict = ri > ci
    incl = ri >= ci
    eye = (ri == ci).astype(F32)
    blk = [(ri >> l) == (ci >> l) for l in range(7)]
    tri = (lax.broadcasted_iota(jnp.int32, (c_len, c_len), 0)
           >= lax.broadcasted_iota(jnp.int32, (c_len, c_len), 1)).astype(BF16)

    @pl.when(pl.program_id(1) == 0)
    def _():
        state_ref[...] = jnp.zeros_like(state_ref)

    units = [(bb, j) for bb in range(nb) for j in range(n_pairs)]
    pairs = range(len(units))

    def mm(a, b, dims=NN):
        return lax.dot_general(a.astype(BF16), b.astype(BF16), dims, preferred_element_type=F32)

    def exact_zero(vregs):
        while len(vregs) > 1:
            vregs = [a + b for a, b in zip(vregs[::2], vregs[1::2])] + vregs[len(vregs) & ~1:]
        return (jnp.clip(vregs[0], -1.0, 1.0) * 0.0)[0:1]

    def prep(c, slot):
        sl = pl.ds(pl.multiple_of(c * c_len, c_len), c_len)
        lw_all, cum_all = [], []
        for bb in range(nb):
            lw_b = jnp.concatenate([lw_ref[bb, j, sl, :] for j in range(n_pairs)], axis=1)
            cum3 = jnp.dot(tri, jnp.concatenate(_split3(lw_b), axis=1), preferred_element_type=F32)
            lw_all.append(lw_b)
            cum_all.append(cum3[:, :RWKV_DIM] + cum3[:, RWKV_DIM:2 * RWKV_DIM] + cum3[:, 2 * RWKV_DIM:])

        folded = []
        for idx, (bb, j) in enumerate(units):
            cs = slice(j * LANES, (j + 1) * LANES)
            cum, lw = cum_all[bb][:, cs], lw_all[bb][:, cs]
            cum_last = cum[c_len - 1:c_len, :]
            r_s, k_s, v_s = stack(r_ref[bb, j, sl, :]), stack(k_ref[bb, j, sl, :]), stack(v_ref[bb, j, sl, :])
            a = a_ref[bb, j, sl, :]

            def both(x):
                return jnp.concatenate([x, x], axis=0)

            kk = k_s * kk_ref[:, cs]
            kk = kk * lax.rsqrt(jnp.maximum(jnp.sum(kk * kk, axis=-1, keepdims=True), 1e-24))
            kmod = k_s * both(1.0 + (a - 1.0) * ka_ref[:, cs])
            bvec = kk * both(a)
            einv = both(jnp.exp(-cum))
            edec = both(jnp.exp(cum_last - cum))
            at_rt = jnp.concatenate([kk * both(-jnp.exp(cum - lw)), r_s * both(jnp.exp(cum))], axis=0)
            bh_kh = jnp.concatenate([bvec * edec, kmod * edec], axis=0)
            bonus = jnp.sum(r_s * kmod * rk_ref[:, cs], axis=-1, keepdims=True) * v_s
            bonus = bonus[:c_len] + bonus[c_len:]
            atrt_ref[slot, idx] = at_rt.astype(BF16)
            btkt_ref[slot, idx] = jnp.concatenate([bvec * einv, kmod * einv], axis=0).astype(BF16)
            bhkh_ref[slot, idx] = bh_kh.astype(BF16)
            vbf_ref[slot, idx] = v_s.astype(BF16)
            decay_ref[slot, idx] = jnp.exp(cum_last)
            bonus_ref[slot, idx] = bonus
            for x in (at_rt, bh_kh, bonus):
                folded.extend(x[r:r + SUBLANES] for r in range(0, x.shape[0], SUBLANES))
        return exact_zero(folded)

    def inverse_part(slots, between=None, zero=None):
        def tick():
            if between is not None:
                next(between, None)

        chains = [(slot, j) for slot in slots for j in pairs]
        lhs = [atrt_ref[slot, j] for slot, j in chains]
        if zero is not None:
            lhs[0] = lhs[0] + zero.astype(BF16)
        gram = [mm(lhs[i], btkt_ref[slot, j], NT) for i, (slot, j) in enumerate(chains)]
        tick()
        a_ab = [jnp.where(strict, g[:n2, :n2], 0.0).astype(BF16) for g in gram]
        a_ak = [jnp.where(strict, g[:n2, n2:], 0.0).astype(BF16) for g in gram]
        r_b = [jnp.where(incl, g[n2:, :n2], 0.0).astype(BF16) for g in gram]
        r_k = [jnp.where(incl, g[n2:, n2:], 0.0).astype(BF16) for g in gram]
        tinv = [(eye + jnp.where(strict & blk[1], g[:n2, :n2], 0.0)).astype(BF16) for g in gram]
        for lvl in range(2, 7):
            off = blk[lvl] & ~blk[lvl - 1]
            w = [jnp.where(off, mm(a, t), 0.0).astype(BF16) for a, t in zip(a_ab, tinv)]
            tinv = [t + mm(t, wi).astype(BF16) for t, wi in zip(tinv, w)]
            tick()
        akv = [mm(a_ak[i], vbf_ref[slot, j]) for i, (slot, j) in enumerate(chains)]
        if between is not None:
            for _ in between:
                pass
        n = len(pairs)
        return {slot: (tinv[k * n:(k + 1) * n], akv[k * n:(k + 1) * n], r_b[k * n:(k + 1) * n],
                       r_k[k * n:(k + 1) * n]) for k, slot in enumerate(slots)}

    def state_stages(c, slot, tinv, akv, r_b, r_k, zero=None):
        sl = pl.ds(pl.multiple_of(c * c_len, c_len), c_len)
        at_rt = [atrt_ref[slot, j] for j in pairs]
        v_bf = [vbf_ref[slot, j] for j in pairs]
        state = [state_ref[j] for j in pairs]
        if zero is not None:
            state[0] = state[0] + zero
        s_t = [mm(at_rt[j], state[j], NT) for j in pairs]
        yield
        rhs = [s_t[j][:n2] + akv[j] for j in pairs]
        rhs_hi = [x.astype(BF16) for x in rhs]
        rhs_lo = [(x - h.astype(F32)).astype(BF16) for x, h in zip(rhs, rhs_hi)]
        u = [mm(tinv[j], rhs_hi[j]) + mm(tinv[j], rhs_lo[j]) for j in pairs]
        u_bf = [u[j].astype(BF16) for j in pairs]
        yield
        for j in pairs:
            state_ref[j] = state[j] * decay_ref[slot, j] + mm(
                jnp.concatenate([u_bf[j], v_bf[j]], axis=0), bhkh_ref[slot, j], TN)
        y = [s_t[j][n2:] + mm(r_b[j], u_bf[j]) + mm(r_k[j], v_bf[j]) for j in pairs]
        yield
        inv_n = 1.0 / HEAD_DIM
        for idx, (bb, j) in enumerate(units):
            cs = slice(j * LANES, (j + 1) * LANES)
            mean = jnp.sum(y[idx], axis=-1, keepdims=True) * inv_n
            yc = jnp.where(own, y[idx] - mean, 0.0)
            var = jnp.sum(yc * yc, axis=-1, keepdims=True) * inv_n
            yn = yc * lax.rsqrt(var + GN_EPS)
            yn = yn[:c_len] + yn[c_len:]
            out = yn * gg_ref[:, cs] + gb_ref[:, cs] + bonus_ref[slot, idx]
            o_ref[bb, sl, cs] = (out * g_ref[bb, j, sl, :]).astype(o_ref.dtype)

    def state_of(c, slots, inv, zero=None):
        for k, slot in enumerate(slots):
            yield from state_stages(c + k, slot, *inv[slot], zero=zero if k == 0 else None)

    n_chunks = ts // c_len
    prep(0, 0)
    prep(1, 1)

    def body(i, carry):
        c0 = 4 * i
        zs = [prep(c0 + 2 + k, 2 + k) for k in range(2)]
        inv_x = inverse_part((0, 1))
        inv_y = inverse_part((2, 3), between=state_of(c0, (0, 1), inv_x))
        zs = [prep(jnp.minimum(c0 + 4 + k, n_chunks - 1), k) for k in range(2)]
        for _ in state_of(c0 + 2, (2, 3), inv_y, zero=zs[0] + zs[1]):
            pass
        return carry

    lax.fori_loop(0, n_chunks // 4, body, 0)


def _rwkv_scan(r, k, v, lw, a, g, kkp, kap, rkp, ggp, gbp, ts=512, nb=2):
    b, n_pairs, s, _ = r.shape
    assert b % nb == 0 and s % ts == 0 and (ts // CHUNK) % SCAN_SLOTS == 0
    n_units = nb * n_pairs
    xspec = pl.BlockSpec((nb, n_pairs, ts, LANES), lambda bi, i: (bi, 0, i, 0))
    pspec = pl.BlockSpec((1, RWKV_DIM), lambda bi, i: (0, 0))
    return pl.pallas_call(
        _rwkv_scan_kernel,
        grid=(b // nb, s // ts),
        in_specs=[xspec] * 6 + [pspec] * 5,
        out_specs=pl.BlockSpec((nb, ts, RWKV_DIM), lambda bi, i: (bi, i, 0)),
        out_shape=jax.ShapeDtypeStruct((b, s, RWKV_DIM), BF16),
        scratch_shapes=[
            pltpu.VMEM((n_units, LANES, LANES), F32),
            pltpu.VMEM((SCAN_SLOTS, n_units, 4 * CHUNK, LANES), BF16),
            pltpu.VMEM((SCAN_SLOTS, n_units, 4 * CHUNK, LANES), BF16),
            pltpu.VMEM((SCAN_SLOTS, n_units, 4 * CHUNK, LANES), BF16),
            pltpu.VMEM((SCAN_SLOTS, n_units, 2 * CHUNK, LANES), BF16),
            pltpu.VMEM((SCAN_SLOTS, n_units, 1, LANES), F32),
            pltpu.VMEM((SCAN_SLOTS, n_units, CHUNK, LANES), F32),
        ],
        compiler_params=_params("parallel", "arbitrary"),
        name="rwkv_scan",
    )(r, k, v, lw, a, g, kkp, kap, rkp, ggp, gbp)


def _swa_kernel(sink_ref, q_ref, kvc_ref, kvp_ref, o_ref):
    n = pl.program_id(1)
    blk = WINDOW
    n_sub = q_ref.shape[1] // blk
    kv_all = jnp.concatenate([kvp_ref[0], kvc_ref[0]], axis=0)
    kmat, vmat = kv_all[:, :KV_DIM], kv_all[:, KV_DIM:]
    lo_kv = lax.broadcasted_iota(jnp.int32, kmat.shape, 1) < HEAD_DIM
    k_sw = pltpu.roll(kmat, HEAD_DIM, axis=1)
    v_sw = pltpu.roll(vmat, HEAD_DIM, axis=1)
    k_dup = [jnp.where(lo_kv, kmat, k_sw).astype(BF16), jnp.where(lo_kv, k_sw, kmat).astype(BF16)]
    v_dup = [jnp.where(lo_kv, vmat, v_sw).astype(BF16), jnp.where(lo_kv, v_sw, vmat).astype(BF16)]

    lo_q = lax.broadcasted_iota(jnp.int32, (blk, LANES), 1) < HEAD_DIM
    rows = lax.broadcasted_iota(jnp.int32, (2 * blk, 2 * blk), 0)
    cols = lax.broadcasted_iota(jnp.int32, (2 * blk, 2 * blk), 1)
    rel = (rows & (blk - 1)) + blk - cols
    band = (rel >= 0) & (rel < WINDOW)
    has_prev = (cols >= blk) | (n > 0)
    relf = rel.astype(F32)
    first = lax.broadcasted_iota(jnp.int32, (2 * blk, 1), 0) < blk

    pairs = range(ATT_Q_HEADS // 2)
    subs = range(n_sub)
    scores = {}
    for sb in subs:
        for j in pairs:
            qp = q_ref[0, sb * blk:(sb + 1) * blk, j * LANES:(j + 1) * LANES] * (HEAD_DIM ** -0.5)
            qs = jnp.concatenate([jnp.where(lo_q, qp, 0.0), jnp.where(lo_q, 0.0, qp)], axis=0)
            keys = k_dup[j // 2][sb * blk:(sb + 2) * blk]
            scores[sb, j] = lax.dot_general(qs.astype(BF16), keys, NT, preferred_element_type=F32)
    probs, inv = {}, {}
    for j in pairs:
        slope = jnp.where(first, 2.0 ** -(2 * j + 1), 2.0 ** -(2 * j + 2))
        sink = jnp.where(first, sink_ref[2 * j], sink_ref[2 * j + 1])
        bias = jnp.where(band, -slope * relf, -jnp.inf)
        for sb in subs:
            s = scores[sb, j] + bias
            if sb == 0:
                s = jnp.where(has_prev, s, -jnp.inf)
            m = jnp.maximum(jnp.max(s, axis=-1, keepdims=True), sink)
            e = jnp.exp(s - m)
            inv[sb, j] = 1.0 / (jnp.sum(e, axis=-1, keepdims=True) + jnp.exp(sink - m))
            probs[sb, j] = e.astype(BF16)
    for sb in subs:
        for j in pairs:
            o = jnp.dot(probs[sb, j], v_dup[j // 2][sb * blk:(sb + 2) * blk], preferred_element_type=F32)
            o = o * inv[sb, j]
            o_ref[0, sb * blk:(sb + 1) * blk, j * LANES:(j + 1) * LANES] = (
                jnp.where(lo_q, o[:blk], o[blk:]).astype(o_ref.dtype))


def _swa(p_att3, sinks, n_sub=4):
    b, s, _ = p_att3.shape
    blk = WINDOW
    tq = n_sub * blk
    assert s % tq == 0
    kv_col = ATT_DIM // (2 * KV_DIM)
    return pl.pallas_call(
        _swa_kernel,
        grid=(b, s // tq),
        in_specs=[
            pl.BlockSpec(memory_space=pltpu.SMEM),
            pl.BlockSpec((1, tq, ATT_DIM), lambda bi, n: (bi, n, 0)),
            pl.BlockSpec((1, tq, 2 * KV_DIM), lambda bi, n: (bi, n, kv_col)),
            pl.BlockSpec((1, blk, 2 * KV_DIM),
                         lambda bi, n: (bi, jnp.maximum(n * n_sub - 1, 0), kv_col)),
        ],
        out_specs=pl.BlockSpec((1, tq, ATT_DIM), lambda bi, n: (bi, n, 0)),
        out_shape=jax.ShapeDtypeStruct((b, s, ATT_DIM), BF16),
        compiler_params=_params("parallel", "parallel"),
        name="swa_attention",
    )(sinks, p_att3, p_att3, p_att3)


def _conv_kernel(u_ref, halo_ref, w_ref, b_ref, lg_ref, lb_ref, o_ref, sh_ref, acc_ref):
    i = pl.program_id(1)
    ts = u_ref.shape[1]
    rb = 128
    n_blocks = D_MODEL // LANES
    n_sh = ts + CONV_HALO - SUBLANES
    for cb in range(n_blocks):
        cs = slice(cb * LANES, (cb + 1) * LANES)
        sh_ref[0, cb, 0:CONV_HALO, :] = jnp.where(i > 0, halo_ref[0, :, cs], 0.0)
        sh_ref[0, cb, CONV_HALO:, :] = u_ref[0, :, cs]
    for q in range(1, SUBLANES):
        for cb in range(n_blocks):
            sh_ref[q, cb, 0:n_sh, :] = sh_ref[0, cb, q:q + n_sh, :]
    off = CONV_HALO - (CONV_WIDTH - 1)
    for cb in range(n_blocks):
        cs = slice(cb * LANES, (cb + 1) * LANES)

        def rows(rblk, carry, cb=cb, cs=cs, qs=(), init=True):
            r0 = pl.multiple_of(rblk * rb, rb)
            acc = (jnp.zeros((rb, LANES), F32) + b_ref[:, cs]) if init else acc_ref[cb, pl.ds(r0, rb), :]
            for q in qs:
                taps = [j for j in range(CONV_WIDTH) if (off + j) % SUBLANES == q]
                a_max = (off + taps[-1]) // SUBLANES
                x = sh_ref[q, cb, pl.ds(r0, rb + a_max * SUBLANES), :]
                part = None
                for j in taps:
                    a = (off + j) // SUBLANES
                    term = w_ref[j:j + 1, cs] * x[a * SUBLANES:a * SUBLANES + rb]
                    part = term if part is None else part + term
                acc = acc + part
            acc_ref[cb, pl.ds(r0, rb), :] = acc
            return carry

        half = SUBLANES // 2
        lax.fori_loop(0, ts // rb, functools.partial(rows, qs=range(half), init=True), 0)
        lax.fori_loop(0, ts // rb, functools.partial(rows, qs=range(half, SUBLANES), init=False), 0)
    y = jnp.concatenate([acc_ref[cb] for cb in range(n_blocks)], axis=1)
    mean = jnp.mean(y, axis=-1, keepdims=True)
    yc = y - mean
    var = jnp.mean(yc * yc, axis=-1, keepdims=True)
    yn = yc * lax.rsqrt(var + LN_EPS) * lg_ref[...] + lb_ref[...]
    o_ref[0] = (yn * jax.nn.sigmoid(yn)).astype(o_ref.dtype)


def _conv_ln_silu(u3, dw_w, dw_b, ln_g, ln_b, ts=512):
    b, s, _ = u3.shape
    assert s % ts == 0 and ts % CONV_HALO == 0 and ts % 128 == 0
    vec = pl.BlockSpec((1, D_MODEL), lambda bi, i: (0, 0))
    return pl.pallas_call(
        _conv_kernel,
        grid=(b, s // ts),
        in_specs=[
            pl.BlockSpec((1, ts, D_MODEL), lambda bi, i: (bi, i, 0)),
            pl.BlockSpec((1, CONV_HALO, D_MODEL),
                         lambda bi, i: (bi, jnp.maximum(i * (ts // CONV_HALO) - 1, 0), 0)),
            pl.BlockSpec((CONV_WIDTH, D_MODEL), lambda bi, i: (0, 0)),
            vec, vec, vec,
        ],
        out_specs=pl.BlockSpec((1, ts, D_MODEL), lambda bi, i: (bi, i, 0)),
        out_shape=jax.ShapeDtypeStruct((b, s, D_MODEL), BF16),
        scratch_shapes=[pltpu.VMEM((SUBLANES, D_MODEL // LANES, ts + CONV_HALO, LANES), F32),
                        pltpu.VMEM((D_MODEL // LANES, ts, LANES), F32)],
        compiler_params=_params("parallel", "parallel"),
        name="conv_ln_silu",
    )(u3, u3, dw_w, dw_b, ln_g, ln_b)


def _proj_mlp_kernel(*refs, n_proj, final_norm):
    x_ref = refs[0]
    y_refs = refs[1:1 + n_proj]
    wp_refs = refs[1 + n_proj:1 + 2 * n_proj]
    pb_ref, g_ref, w1_ref, w2_ref, gf_ref, o_ref, x1_ref, h_ref, acc_ref = refs[1 + 2 * n_proj:]
    j = pl.program_id(1)
    last = pl.num_programs(1) - 1
    tm = x_ref.shape[0]
    all_rows = slice(0, tm)
    halves = (slice(0, tm // 2), slice(tm // 2, tm))

    def residual_in(rows):
        x1 = x_ref[rows, :] + pb_ref[...]
        for y_ref, wp_ref in zip(y_refs, wp_refs):
            x1 = x1 + jnp.dot(y_ref[rows, :], wp_ref[...], preferred_element_type=F32)
        x1_ref[rows, :] = x1
        h_ref[rows, :] = _rms(x1, g_ref[...]).astype(BF16)

    def hidden(rows, w1):
        hid = jnp.dot(h_ref[rows, :], w1, preferred_element_type=F32)
        return jnp.square(jnp.maximum(hid, 0.0)).astype(BF16)

    def residual_out(rows, mlp):
        out = x1_ref[rows, :] + mlp
        if final_norm:
            out = _rms(out, gf_ref[...])
        o_ref[rows, :] = out

    @pl.when(j == 0)
    def _():
        w1, w2 = w1_ref[...].astype(BF16), w2_ref[...].astype(BF16)
        for rows in halves:
            residual_in(rows)
        for rows in halves:
            acc_ref[rows, :] = jnp.dot(hidden(rows, w1), w2, preferred_element_type=F32)

    @pl.when((j > 0) & (j < last))
    def _():
        hid = hidden(all_rows, w1_ref[...].astype(BF16))
        acc_ref[...] += jnp.dot(hid, w2_ref[...].astype(BF16), preferred_element_type=F32)

    @pl.when(j == last)
    def _():
        w2 = w2_ref[...].astype(BF16)
        hid = hidden(all_rows, w1_ref[...].astype(BF16))
        for rows in halves:
            residual_out(rows, acc_ref[rows, :] + jnp.dot(hid[rows], w2, preferred_element_type=F32))


def _proj_mlp(x2, ys, wps, pb, g, w1_all, w2_all, layer, gf, final_norm, tm=1024, tf=1024):
    t = x2.shape[0]
    n_proj = len(ys)
    assert t % tm == 0 and D_FF % tf == 0 and D_FF // tf >= 2
    vec = pl.BlockSpec((1, D_MODEL), lambda i, j: (0, 0))
    in_specs = [pl.BlockSpec((tm, D_MODEL), lambda i, j: (i, 0))]
    in_specs += [pl.BlockSpec((tm, y.shape[1]), lambda i, j: (i, 0)) for y in ys]
    in_specs += [pl.BlockSpec(w.shape, lambda i, j: (0, 0)) for w in wps]
    in_specs += [vec, vec,
                 pl.BlockSpec((None, D_MODEL, tf), lambda i, j: (layer, 0, j)),
                 pl.BlockSpec((None, tf, D_MODEL), lambda i, j: (layer, j, 0)),
                 vec]
    return pl.pallas_call(
        functools.partial(_proj_mlp_kernel, n_proj=n_proj, final_norm=final_norm),
        grid=(t // tm, D_FF // tf),
        in_specs=in_specs,
        out_specs=pl.BlockSpec((tm, D_MODEL), lambda i, j: (i, 0)),
        out_shape=jax.ShapeDtypeStruct((t, D_MODEL), F32),
        scratch_shapes=[pltpu.VMEM((tm, D_MODEL), F32),
                        pltpu.VMEM((tm, D_MODEL), BF16),
                        pltpu.VMEM((tm, D_MODEL), F32)],
        compiler_params=_params("parallel", "arbitrary"),
        name="proj_mlp",
    )(x2, *ys, *wps, pb, g, w1_all, w2_all, gf)


def _row(v):
    return v.reshape(1, -1).astype(F32)


def _hybrid_front(x3, g_mix, w_in, mu, w0, w_up, a0, a_up, g_up):
    c3 = 3 * RWKV_DIM
    w = jnp.concatenate([w_in[:, c3:SHIFT_DIM], w_in[:, :c3], w_in[:, SHIFT_DIM:]], axis=1).astype(BF16)
    mu_p = jnp.concatenate([mu[c3:], mu[:c3]]).reshape(1, -1).astype(F32)
    w_lora = jnp.zeros((LORA_DIM, c3), F32)
    w_lora = w_lora.at[:LORA_W, :RWKV_DIM].set(w_up)
    w_lora = w_lora.at[LORA_W:LORA_W + LORA_A, RWKV_DIM:2 * RWKV_DIM].set(a_up)
    w_lora = w_lora.at[LORA_W + LORA_A:, 2 * RWKV_DIM:].set(g_up)
    bias = jnp.concatenate([w0, a0, jnp.zeros((RWKV_DIM,), F32)]).reshape(1, -1)
    return _in_proj(x3, g_mix, w, mu_p, w_lora.astype(BF16), bias)


def kernel(x, norm_mix_g, norm_ffn_g, final_norm_g, hy_w_in, hy_mu, hy_w0, hy_w_up, hy_a0, hy_a_up, hy_g_up, hy_k_k, hy_k_a, hy_r_k, hy_gn_g, hy_gn_b, hy_sinks, hy_w_out, cv_pw1_w, cv_pw1_b, cv_dw_w, cv_dw_b, cv_ln_g, cv_ln_b, cv_pw2_w, cv_pw2_b, mlp_w1, mlp_w2):
    bsz, seq, d = x.shape
    depth = norm_mix_g.shape[0]
    t = bsz * seq
    x2 = x.reshape(t, d)
    zero_row = jnp.zeros((1, D_MODEL), F32)
    gf = _row(final_norm_g)
    w1_all, w2_all = mlp_w1, mlp_w2

    for layer in range(depth):
        i = layer // 2
        g_mix = _row(norm_mix_g[layer])
        if layer % 2 == 0:
            r, k, v, lw, a, gate, p_att = _hybrid_front(
                x2.reshape(bsz, seq, d), g_mix, hy_w_in[i], hy_mu[i], hy_w0[i], hy_w_up[i],
                hy_a0[i], hy_a_up[i], hy_g_up[i])
            y_rwkv = _rwkv_scan(r, k, v, lw, a, gate, _row(hy_k_k[i]), _row(hy_k_a[i]),
                                _row(hy_r_k[i]), _row(hy_gn_g[i]), _row(hy_gn_b[i]))
            y_att = _swa(p_att, hy_sinks[i].astype(F32))
            w_out = hy_w_out[i].astype(BF16)
            ys = [y_rwkv.reshape(t, RWKV_DIM), y_att.reshape(t, ATT_DIM)]
            wps = [w_out[:RWKV_DIM], w_out[RWKV_DIM:]]
            pb = zero_row
        else:
            u = _norm_glu(x2, g_mix, cv_pw1_w[i].astype(BF16), _row(cv_pw1_b[i]))
            u = _conv_ln_silu(u.reshape(bsz, seq, D_MODEL), cv_dw_w[i].astype(F32),
                              _row(cv_dw_b[i]), _row(cv_ln_g[i]), _row(cv_ln_b[i]))
            ys = [u.reshape(t, D_MODEL)]
            wps = [cv_pw2_w[i].astype(BF16)]
            pb = _row(cv_pw2_b[i])
        x2 = _proj_mlp(x2, ys, wps, pb, _row(norm_ffn_g[layer]), w1_all, w2_all, layer, gf,
                       final_norm=(layer == depth - 1))
    return x2.reshape(bsz, seq, d)
```

```python
import functools

import jax
import jax.numpy as jnp
from jax import lax
from jax.experimental import pallas as pl
from jax.experimental.pallas import tpu as pltpu

D_MODEL = 1024
HEAD_DIM = 64
RWKV_DIM = 512
ATT_Q_HEADS = 8
ATT_DIM = 512
KV_DIM = 128
LORA_W = 64
LORA_A = 64
LORA_G = 128
SHIFT_DIM = 3 * RWKV_DIM + LORA_W + LORA_A + LORA_G
ATT_COLS = ATT_DIM + 2 * KV_DIM
WINDOW = 128
CONV_WIDTH = 31
CONV_HALO = 32
D_FF = 4 * D_MODEL
RMS_EPS = 1e-6
LN_EPS = 1e-5
GN_EPS = 64e-5

LANES = 128
SUBLANES = 8
CHUNK = 64
SCAN_SLOTS = 4
VMEM_LIMIT = 56 * 1024 * 1024

F32 = jnp.float32
BF16 = jnp.bfloat16
NN = (((1,), (0,)), ((), ()))
NT = (((1,), (1,)), ((), ()))
TN = (((0,), (0,)), ((), ()))


def _params(*sem):
    return pltpu.CompilerParams(dimension_semantics=sem, vmem_limit_bytes=VMEM_LIMIT)


def _rms(x, g):
    return x * lax.rsqrt(jnp.mean(x * x, axis=-1, keepdims=True) + RMS_EPS) * g


LORA_DIM = LORA_W + LORA_A + LORA_G


def _in_proj_kernel(x_ref, g_ref, w_ref, mu_ref, wl_ref, bias_ref,
                    r_ref, k_ref, v_ref, lw_ref, a_ref, gate_ref, att_ref, prev_ref):
    tm = x_ref.shape[1]
    c1, c2, c3 = RWKV_DIM, 2 * RWKV_DIM, 3 * RWKV_DIM

    @pl.when(pl.program_id(1) == 0)
    def _():
        prev_ref[...] = jnp.zeros_like(prev_ref)

    def shift_lerp(p, cs):
        first = lax.broadcasted_iota(jnp.int32, p.shape, 0) == 0
        shifted = jnp.where(first, prev_ref[SUBLANES - 1:SUBLANES, cs], pltpu.roll(p, 1, axis=0))
        prev_ref[:, cs] = p[tm - SUBLANES:, :]
        return p + (shifted - p) * mu_ref[:, cs]

    h = _rms(x_ref[0], g_ref[...]).astype(BF16)
    z = jnp.dot(h, w_ref[:, :LORA_DIM], preferred_element_type=F32)
    z = shift_lerp(z, slice(0, LORA_DIM))
    lane = lax.broadcasted_iota(jnp.int32, z.shape, 1)
    act = jnp.where(lane < LORA_W, jnp.tanh(z),
                    jnp.where(lane < LORA_W + LORA_A, z, jax.nn.sigmoid(z)))
    lo = jnp.dot(act.astype(BF16), wl_ref[...], preferred_element_type=F32) + bias_ref[...]
    rkv = jnp.dot(h, w_ref[:, LORA_DIM:SHIFT_DIM], preferred_element_type=F32)
    att_ref[0] = jnp.dot(h, w_ref[:, SHIFT_DIM:], preferred_element_type=F32)
    rkv = shift_lerp(rkv, slice(LORA_DIM, SHIFT_DIM))
    r_ref[0] = rkv[:, :c1]
    k_ref[0] = rkv[:, c1:c2]
    v_ref[0] = rkv[:, c2:c3]
    wpre = lo[:, :c1]
    softplus = jnp.maximum(-wpre, 0.0) + jnp.log(1.0 + jnp.exp(-jnp.abs(wpre)))
    lw_ref[0] = -jnp.exp(-softplus - 0.5)
    a_ref[0] = jax.nn.sigmoid(lo[:, c1:c2])
    gate_ref[0] = lo[:, c2:]


def _in_proj(x3, g, w, mu, w_lora, bias, tm=512):
    b, s, _ = x3.shape
    assert s % tm == 0 and tm % SUBLANES == 0
    rspec = pl.BlockSpec((1, tm, RWKV_DIM), lambda bi, i: (bi, i, 0))
    rshape = jax.ShapeDtypeStruct((b, s, RWKV_DIM), F32)
    const = lambda shape: pl.BlockSpec(shape, lambda bi, i: (0, 0))
    return pl.pallas_call(
        _in_proj_kernel,
        grid=(b, s // tm),
        in_specs=[
            pl.BlockSpec((1, tm, D_MODEL), lambda bi, i: (bi, i, 0)),
            const((1, D_MODEL)), const(w.shape), const((1, SHIFT_DIM)),
            const(w_lora.shape), const((1, 3 * RWKV_DIM)),
        ],
        out_specs=[rspec] * 6 + [pl.BlockSpec((1, tm, ATT_COLS), lambda bi, i: (bi, i, 0))],
        out_shape=[rshape] * 6 + [jax.ShapeDtypeStruct((b, s, ATT_COLS), F32)],
        scratch_shapes=[pltpu.VMEM((SUBLANES, SHIFT_DIM), F32)],
        compiler_params=_params("parallel", "arbitrary"),
        name="in_proj",
    )(x3, g, w, mu, w_lora, bias)


def _norm_glu_kernel(x_ref, g_ref, w_ref, b_ref, o_ref):
    h = _rms(x_ref[...], g_ref[...]).astype(BF16)
    acc = jnp.dot(h, w_ref[...], preferred_element_type=F32) + b_ref[...]
    o_ref[...] = acc[:, :D_MODEL] * jax.nn.sigmoid(acc[:, D_MODEL:])


def _norm_glu(x2, g, w, b, tm=1024):
    t = x2.shape[0]
    assert t % tm == 0
    return pl.pallas_call(
        _norm_glu_kernel,
        grid=(t // tm,),
        in_specs=[
            pl.BlockSpec((tm, D_MODEL), lambda i: (i, 0)),
            pl.BlockSpec((1, D_MODEL), lambda i: (0, 0)),
            pl.BlockSpec((D_MODEL, 2 * D_MODEL), lambda i: (0, 0)),
            pl.BlockSpec((1, 2 * D_MODEL), lambda i: (0, 0)),
        ],
        out_specs=pl.BlockSpec((tm, D_MODEL), lambda i: (i, 0)),
        out_shape=jax.ShapeDtypeStruct((t, D_MODEL), F32),
        compiler_params=_params("parallel"),
        name="norm_glu",
    )(x2, g, w, b)


def _split3(x):
    hi = x.astype(BF16)
    r1 = x - hi.astype(F32)
    mid = r1.astype(BF16)
    lo = (r1 - mid.astype(F32)).astype(BF16)
    return hi, mid, lo


def _rwkv_scan_kernel(r_ref, k_ref, v_ref, lw_ref, a_ref, g_ref,
                      kk_ref, ka_ref, rk_ref, gg_ref, gb_ref, o_ref, state_ref,
                      atrt_ref, btkt_ref, bhkh_ref, vbf_ref, decay_ref, bonus_ref):
    c_len = CHUNK
    n2 = 2 * c_len
    nb, ts = r_ref.shape[0], r_ref.shape[1]
    n_pairs = RWKV_DIM // LANES
    lane = lax.broadcasted_iota(jnp.int32, (c_len, LANES), 1)
    lo = lane < HEAD_DIM
    lo2 = lax.broadcasted_iota(jnp.int32, (n2, LANES), 1) < HEAD_DIM
    top = lax.broadcasted_iota(jnp.int32, (n2, LANES), 0) < c_len
    own = lo2 == top

    def stack(x):
        return jnp.concatenate([jnp.where(lo, x, 0.0), jnp.where(lo, 0.0, x)], axis=0)

    ri = lax.broadcasted_iota(jnp.int32, (n2, n2), 0)
    ci = lax.broadcasted_iota(jnp.int32, (n2, n2), 1)
    strict = ri > ci
    incl = ri >= ci
    eye = (ri == ci).astype(F32)
    blk = [(ri >> l) == (ci >> l) for l in range(7)]
    tri = (lax.broadcasted_iota(jnp.int32, (c_len, c_len), 0)
           >= lax.broadcasted_iota(jnp.int32, (c_len, c_len), 1)).astype(BF16)

    @pl.when(pl.program_id(1) == 0)
    def _():
        state_ref[...] = jnp.zeros_like(state_ref)

    units = [(bb, j) for bb in range(nb) for j in range(n_pairs)]
    pairs = range(len(units))

    def mm(a, b, dims=NN):
        return lax.dot_general(a.astype(BF16), b.astype(BF16), dims, preferred_element_type=F32)

    def exact_zero(vregs):
        while len(vregs) > 1:
            vregs = [a + b for a, b in zip(vregs[::2], vregs[1::2])] + vregs[len(vregs) & ~1:]
        return (jnp.clip(vregs[0], -1.0, 1.0) * 0.0)[0:1]

    def prep(c, slot):
        sl = pl.ds(pl.multiple_of(c * c_len, c_len), c_len)
        lw_all, cum_all = [], []
        for bb in range(nb):
            lw_b = lw_ref[bb, sl, :]
            cum3 = jnp.dot(tri, jnp.concatenate(_split3(lw_b), axis=1), preferred_element_type=F32)
            lw_all.append(lw_b)
            cum_all.append(cum3[:, :RWKV_DIM] + cum3[:, RWKV_DIM:2 * RWKV_DIM] + cum3[:, 2 * RWKV_DIM:])

        folded = []
        for idx, (bb, j) in enumerate(units):
            cs = slice(j * LANES, (j + 1) * LANES)
            cum, lw = cum_all[bb][:, cs], lw_all[bb][:, cs]
            cum_last = cum[c_len - 1:c_len, :]
            r_s, k_s, v_s = stack(r_ref[bb, sl, cs]), stack(k_ref[bb, sl, cs]), stack(v_ref[bb, sl, cs])
            a = a_ref[bb, sl, cs]

            def both(x):
                return jnp.concatenate([x, x], axis=0)

            kk = k_s * kk_ref[:, cs]
            kk = kk * lax.rsqrt(jnp.maximum(jnp.sum(kk * kk, axis=-1, keepdims=True), 1e-24))
            kmod = k_s * both(1.0 + (a - 1.0) * ka_ref[:, cs])
            bvec = kk * both(a)
            einv = both(jnp.exp(-cum))
            edec = both(jnp.exp(cum_last - cum))
            at_rt = jnp.concatenate([kk * both(-jnp.exp(cum - lw)), r_s * both(jnp.exp(cum))], axis=0)
            bh_kh = jnp.concatenate([bvec * edec, kmod * edec], axis=0)
            bonus = jnp.sum(r_s * kmod * rk_ref[:, cs], axis=-1, keepdims=True) * v_s
            bonus = bonus[:c_len] + bonus[c_len:]
            atrt_ref[slot, idx] = at_rt.astype(BF16)
            btkt_ref[slot, idx] = jnp.concatenate([bvec * einv, kmod * einv], axis=0).astype(BF16)
            bhkh_ref[slot, idx] = bh_kh.astype(BF16)
            vbf_ref[slot, idx] = v_s.astype(BF16)
            decay_ref[slot, idx] = jnp.exp(cum_last)
            bonus_ref[slot, idx] = bonus
            for x in (at_rt, bh_kh, bonus):
                folded.extend(x[r:r + SUBLANES] for r in range(0, x.shape[0], SUBLANES))
        return exact_zero(folded)

    def inverse_part(slots, between=None, zero=None):
        def tick():
            if between is not None:
                next(between, None)

        chains = [(slot, j) for slot in slots for j in pairs]
        lhs = [atrt_ref[slot, j] for slot, j in chains]
        if zero is not None:
            lhs[0] = lhs[0] + zero.astype(BF16)
        gram = [mm(lhs[i], btkt_ref[slot, j], NT) for i, (slot, j) in enumerate(chains)]
        tick()
        a_ab = [jnp.where(strict, g[:n2, :n2], 0.0).astype(BF16) for g in gram]
        a_ak = [jnp.where(strict, g[:n2, n2:], 0.0).astype(BF16) for g in gram]
        r_b = [jnp.where(incl, g[n2:, :n2], 0.0).astype(BF16) for g in gram]
        r_k = [jnp.where(incl, g[n2:, n2:], 0.0).astype(BF16) for g in gram]
        tinv = [(eye + jnp.where(strict & blk[1], g[:n2, :n2], 0.0)).astype(BF16) for g in gram]
        for lvl in range(2, 7):
            off = blk[lvl] & ~blk[lvl - 1]
            w = [jnp.where(off, mm(a, t), 0.0).astype(BF16) for a, t in zip(a_ab, tinv)]
            tinv = [t + mm(t, wi).astype(BF16) for t, wi in zip(tinv, w)]
            tick()
        akv = [mm(a_ak[i], vbf_ref[slot, j]) for i, (slot, j) in enumerate(chains)]
        if between is not None:
            for _ in between:
                pass
        n = len(pairs)
        return {slot: (tinv[k * n:(k + 1) * n], akv[k * n:(k + 1) * n], r_b[k * n:(k + 1) * n],
                       r_k[k * n:(k + 1) * n]) for k, slot in enumerate(slots)}

    def state_stages(c, slot, tinv, akv, r_b, r_k, zero=None):
        sl = pl.ds(pl.multiple_of(c * c_len, c_len), c_len)
        at_rt = [atrt_ref[slot, j] for j in pairs]
        v_bf = [vbf_ref[slot, j] for j in pairs]
        state = [state_ref[j] for j in pairs]
        if zero is not None:
            state[0] = state[0] + zero
        s_t = [mm(at_rt[j], state[j], NT) for j in pairs]
        yield
        rhs = [s_t[j][:n2] + akv[j] for j in pairs]
        rhs_hi = [x.astype(BF16) for x in rhs]
        rhs_lo = [(x - h.astype(F32)).astype(BF16) for x, h in zip(rhs, rhs_hi)]
        u = [mm(tinv[j], rhs_hi[j]) + mm(tinv[j], rhs_lo[j]) for j in pairs]
        u_bf = [u[j].astype(BF16) for j in pairs]
        yield
        for j in pairs:
            state_ref[j] = state[j] * decay_ref[slot, j] + mm(
                jnp.concatenate([u_bf[j], v_bf[j]], axis=0), bhkh_ref[slot, j], TN)
        y = [s_t[j][n2:] + mm(r_b[j], u_bf[j]) + mm(r_k[j], v_bf[j]) for j in pairs]
        yield
        inv_n = 1.0 / HEAD_DIM
        for idx, (bb, j) in enumerate(units):
            cs = slice(j * LANES, (j + 1) * LANES)
            mean = jnp.sum(y[idx], axis=-1, keepdims=True) * inv_n
            yc = jnp.where(own, y[idx] - mean, 0.0)
            var = jnp.sum(yc * yc, axis=-1, keepdims=True) * inv_n
            yn = yc * lax.rsqrt(var + GN_EPS)
            yn = yn[:c_len] + yn[c_len:]
            out = yn * gg_ref[:, cs] + gb_ref[:, cs] + bonus_ref[slot, idx]
            o_ref[bb, sl, cs] = (out * g_ref[bb, sl, cs]).astype(o_ref.dtype)

    def state_of(c, slots, inv, zero=None):
        for k, slot in enumerate(slots):
            yield from state_stages(c + k, slot, *inv[slot], zero=zero if k == 0 else None)

    n_chunks = ts // c_len
    prep(0, 0)
    prep(1, 1)

    def body(i, carry):
        c0 = 4 * i
        zs = [prep(c0 + 2 + k, 2 + k) for k in range(2)]
        inv_x = inverse_part((0, 1))
        inv_y = inverse_part((2, 3), between=state_of(c0, (0, 1), inv_x))
        zs = [prep(jnp.minimum(c0 + 4 + k, n_chunks - 1), k) for k in range(2)]
        for _ in state_of(c0 + 2, (2, 3), inv_y, zero=zs[0] + zs[1]):
            pass
        return carry

    lax.fori_loop(0, n_chunks // 4, body, 0)


def _rwkv_scan(r, k, v, lw, a, g, kkp, kap, rkp, ggp, gbp, ts=512, nb=2):
    b, s, _ = r.shape
    assert b % nb == 0 and s % ts == 0 and (ts // CHUNK) % SCAN_SLOTS == 0
    n_units = nb * RWKV_DIM // LANES
    xspec = pl.BlockSpec((nb, ts, RWKV_DIM), lambda bi, i: (bi, i, 0))
    pspec = pl.BlockSpec((1, RWKV_DIM), lambda bi, i: (0, 0))
    return pl.pallas_call(
        _rwkv_scan_kernel,
        grid=(b // nb, s // ts),
        in_specs=[xspec] * 6 + [pspec] * 5,
        out_specs=xspec,
        out_shape=jax.ShapeDtypeStruct((b, s, RWKV_DIM), BF16),
        scratch_shapes=[
            pltpu.VMEM((n_units, LANES, LANES), F32),
            pltpu.VMEM((SCAN_SLOTS, n_units, 4 * CHUNK, LANES), BF16),
            pltpu.VMEM((SCAN_SLOTS, n_units, 4 * CHUNK, LANES), BF16),
            pltpu.VMEM((SCAN_SLOTS, n_units, 4 * CHUNK, LANES), BF16),
            pltpu.VMEM((SCAN_SLOTS, n_units, 2 * CHUNK, LANES), BF16),
            pltpu.VMEM((SCAN_SLOTS, n_units, 1, LANES), F32),
            pltpu.VMEM((SCAN_SLOTS, n_units, CHUNK, LANES), F32),
        ],
        compiler_params=_params("parallel", "arbitrary"),
        name="rwkv_scan",
    )(r, k, v, lw, a, g, kkp, kap, rkp, ggp, gbp)


def _swa_kernel(sink_ref, q_ref, kvc_ref, kvp_ref, o_ref):
    n = pl.program_id(1)
    blk = WINDOW
    n_sub = q_ref.shape[1] // blk
    kv_all = jnp.concatenate([kvp_ref[0], kvc_ref[0]], axis=0)
    kmat, vmat = kv_all[:, :KV_DIM], kv_all[:, KV_DIM:]
    lo_kv = lax.broadcasted_iota(jnp.int32, kmat.shape, 1) < HEAD_DIM
    k_sw = pltpu.roll(kmat, HEAD_DIM, axis=1)
    v_sw = pltpu.roll(vmat, HEAD_DIM, axis=1)
    k_dup = [jnp.where(lo_kv, kmat, k_sw).astype(BF16), jnp.where(lo_kv, k_sw, kmat).astype(BF16)]
    v_dup = [jnp.where(lo_kv, vmat, v_sw).astype(BF16), jnp.where(lo_kv, v_sw, vmat).astype(BF16)]

    lo_q = lax.broadcasted_iota(jnp.int32, (blk, LANES), 1) < HEAD_DIM
    rows = lax.broadcasted_iota(jnp.int32, (2 * blk, 2 * blk), 0)
    cols = lax.broadcasted_iota(jnp.int32, (2 * blk, 2 * blk), 1)
    rel = (rows & (blk - 1)) + blk - cols
    band = (rel >= 0) & (rel < WINDOW)
    has_prev = (cols >= blk) | (n > 0)
    relf = rel.astype(F32)
    first = lax.broadcasted_iota(jnp.int32, (2 * blk, 1), 0) < blk

    pairs = range(ATT_Q_HEADS // 2)
    subs = range(n_sub)
    scores = {}
    for sb in subs:
        for j in pairs:
            qp = q_ref[0, sb * blk:(sb + 1) * blk, j * LANES:(j + 1) * LANES] * (HEAD_DIM ** -0.5)
            qs = jnp.concatenate([jnp.where(lo_q, qp, 0.0), jnp.where(lo_q, 0.0, qp)], axis=0)
            keys = k_dup[j // 2][sb * blk:(sb + 2) * blk]
            scores[sb, j] = lax.dot_general(qs.astype(BF16), keys, NT, preferred_element_type=F32)
    probs, inv = {}, {}
    for j in pairs:
        slope = jnp.where(first, 2.0 ** -(2 * j + 1), 2.0 ** -(2 * j + 2))
        sink = jnp.where(first, sink_ref[2 * j], sink_ref[2 * j + 1])
        bias = jnp.where(band, -slope * relf, -jnp.inf)
        for sb in subs:
            s = scores[sb, j] + bias
            if sb == 0:
                s = jnp.where(has_prev, s, -jnp.inf)
            m = jnp.maximum(jnp.max(s, axis=-1, keepdims=True), sink)
            e = jnp.exp(s - m)
            inv[sb, j] = 1.0 / (jnp.sum(e, axis=-1, keepdims=True) + jnp.exp(sink - m))
            probs[sb, j] = e.astype(BF16)
    for sb in subs:
        for j in pairs:
            o = jnp.dot(probs[sb, j], v_dup[j // 2][sb * blk:(sb + 2) * blk], preferred_element_type=F32)
            o = o * inv[sb, j]
            o_ref[0, sb * blk:(sb + 1) * blk, j * LANES:(j + 1) * LANES] = (
                jnp.where(lo_q, o[:blk], o[blk:]).astype(o_ref.dtype))


def _swa(p_att3, sinks, n_sub=4):
    b, s, _ = p_att3.shape
    blk = WINDOW
    tq = n_sub * blk
    assert s % tq == 0
    kv_col = ATT_DIM // (2 * KV_DIM)
    return pl.pallas_call(
        _swa_kernel,
        grid=(b, s // tq),
        in_specs=[
            pl.BlockSpec(memory_space=pltpu.SMEM),
            pl.BlockSpec((1, tq, ATT_DIM), lambda bi, n: (bi, n, 0)),
            pl.BlockSpec((1, tq, 2 * KV_DIM), lambda bi, n: (bi, n, kv_col)),
            pl.BlockSpec((1, blk, 2 * KV_DIM),
                         lambda bi, n: (bi, jnp.maximum(n * n_sub - 1, 0), kv_col)),
        ],
        out_specs=pl.BlockSpec((1, tq, ATT_DIM), lambda bi, n: (bi, n, 0)),
        out_shape=jax.ShapeDtypeStruct((b, s, ATT_DIM), BF16),
        compiler_params=_params("parallel", "parallel"),
        name="swa_attention",
    )(sinks, p_att3, p_att3, p_att3)


def _conv_kernel(u_ref, halo_ref, w_ref, b_ref, lg_ref, lb_ref, o_ref, sh_ref, acc_ref):
    i = pl.program_id(1)
    ts = u_ref.shape[1]
    rb = 128
    n_blocks = D_MODEL // LANES
    n_sh = ts + CONV_HALO - SUBLANES
    for cb in range(n_blocks):
        cs = slice(cb * LANES, (cb + 1) * LANES)
        sh_ref[0, cb, 0:CONV_HALO, :] = jnp.where(i > 0, halo_ref[0, :, cs], 0.0)
        sh_ref[0, cb, CONV_HALO:, :] = u_ref[0, :, cs]
    for q in range(1, SUBLANES):
        for cb in range(n_blocks):
            sh_ref[q, cb, 0:n_sh, :] = sh_ref[0, cb, q:q + n_sh, :]
    off = CONV_HALO - (CONV_WIDTH - 1)
    for cb in range(n_blocks):
        cs = slice(cb * LANES, (cb + 1) * LANES)

        def rows(rblk, carry, cb=cb, cs=cs, qs=(), init=True):
            r0 = pl.multiple_of(rblk * rb, rb)
            acc = (jnp.zeros((rb, LANES), F32) + b_ref[:, cs]) if init else acc_ref[cb, pl.ds(r0, rb), :]
            for q in qs:
                taps = [j for j in range(CONV_WIDTH) if (off + j) % SUBLANES == q]
                a_max = (off + taps[-1]) // SUBLANES
                x = sh_ref[q, cb, pl.ds(r0, rb + a_max * SUBLANES), :]
                part = None
                for j in taps:
                    a = (off + j) // SUBLANES
                    term = w_ref[j:j + 1, cs] * x[a * SUBLANES:a * SUBLANES + rb]
                    part = term if part is None else part + term
                acc = acc + part
            acc_ref[cb, pl.ds(r0, rb), :] = acc
            return carry

        half = SUBLANES // 2
        lax.fori_loop(0, ts // rb, functools.partial(rows, qs=range(half), init=True), 0)
        lax.fori_loop(0, ts // rb, functools.partial(rows, qs=range(half, SUBLANES), init=False), 0)
    y = jnp.concatenate([acc_ref[cb] for cb in range(n_blocks)], axis=1)
    mean = jnp.mean(y, axis=-1, keepdims=True)
    yc = y - mean
    var = jnp.mean(yc * yc, axis=-1, keepdims=True)
    yn = yc * lax.rsqrt(var + LN_EPS) * lg_ref[...] + lb_ref[...]
    o_ref[0] = (yn * jax.nn.sigmoid(yn)).astype(o_ref.dtype)


def _conv_ln_silu(u3, dw_w, dw_b, ln_g, ln_b, ts=512):
    b, s, _ = u3.shape
    assert s % ts == 0 and ts % CONV_HALO == 0 and ts % 128 == 0
    vec = pl.BlockSpec((1, D_MODEL), lambda bi, i: (0, 0))
    return pl.pallas_call(
        _conv_kernel,
        grid=(b, s // ts),
        in_specs=[
            pl.BlockSpec((1, ts, D_MODEL), lambda bi, i: (bi, i, 0)),
            pl.BlockSpec((1, CONV_HALO, D_MODEL),
                         lambda bi, i: (bi, jnp.maximum(i * (ts // CONV_HALO) - 1, 0), 0)),
            pl.BlockSpec((CONV_WIDTH, D_MODEL), lambda bi, i: (0, 0)),
            vec, vec, vec,
        ],
        out_specs=pl.BlockSpec((1, ts, D_MODEL), lambda bi, i: (bi, i, 0)),
        out_shape=jax.ShapeDtypeStruct((b, s, D_MODEL), BF16),
        scratch_shapes=[pltpu.VMEM((SUBLANES, D_MODEL // LANES, ts + CONV_HALO, LANES), F32),
                        pltpu.VMEM((D_MODEL // LANES, ts, LANES), F32)],
        compiler_params=_params("parallel", "parallel"),
        name="conv_ln_silu",
    )(u3, u3, dw_w, dw_b, ln_g, ln_b)


def _proj_mlp_kernel(*refs, n_proj, final_norm):
    x_ref = refs[0]
    y_refs = refs[1:1 + n_proj]
    wp_refs = refs[1 + n_proj:1 + 2 * n_proj]
    pb_ref, g_ref, w1_ref, w2_ref, gf_ref, o_ref, x1_ref, h_ref, acc_ref = refs[1 + 2 * n_proj:]
    j = pl.program_id(1)
    last = pl.num_programs(1) - 1
    tm = x_ref.shape[0]
    all_rows = slice(0, tm)
    halves = (slice(0, tm // 2), slice(tm // 2, tm))

    def residual_in(rows):
        x1 = x_ref[rows, :] + pb_ref[...]
        for y_ref, wp_ref in zip(y_refs, wp_refs):
            x1 = x1 + jnp.dot(y_ref[rows, :], wp_ref[...], preferred_element_type=F32)
        x1_ref[rows, :] = x1
        h_ref[rows, :] = _rms(x1, g_ref[...]).astype(BF16)

    def hidden(rows, w1):
        hid = jnp.dot(h_ref[rows, :], w1, preferred_element_type=F32)
        return jnp.square(jnp.maximum(hid, 0.0)).astype(BF16)

    def residual_out(rows, mlp):
        out = x1_ref[rows, :] + mlp
        if final_norm:
            out = _rms(out, gf_ref[...])
        o_ref[rows, :] = out

    @pl.when(j == 0)
    def _():
        w1, w2 = w1_ref[...].astype(BF16), w2_ref[...].astype(BF16)
        for rows in halves:
            residual_in(rows)
        for rows in halves:
            acc_ref[rows, :] = jnp.dot(hidden(rows, w1), w2, preferred_element_type=F32)

    @pl.when((j > 0) & (j < last))
    def _():
        hid = hidden(all_rows, w1_ref[...].astype(BF16))
        acc_ref[...] += jnp.dot(hid, w2_ref[...].astype(BF16), preferred_element_type=F32)

    @pl.when(j == last)
    def _():
        w2 = w2_ref[...].astype(BF16)
        hid = hidden(all_rows, w1_ref[...].astype(BF16))
        for rows in halves:
            residual_out(rows, acc_ref[rows, :] + jnp.dot(hid[rows], w2, preferred_element_type=F32))


def _proj_mlp(x2, ys, wps, pb, g, w1_all, w2_all, layer, gf, final_norm, tm=1024, tf=1024):
    t = x2.shape[0]
    n_proj = len(ys)
    assert t % tm == 0 and D_FF % tf == 0 and D_FF // tf >= 2
    vec = pl.BlockSpec((1, D_MODEL), lambda i, j: (0, 0))
    in_specs = [pl.BlockSpec((tm, D_MODEL), lambda i, j: (i, 0))]
    in_specs += [pl.BlockSpec((tm, y.shape[1]), lambda i, j: (i, 0)) for y in ys]
    in_specs += [pl.BlockSpec(w.shape, lambda i, j: (0, 0)) for w in wps]
    in_specs += [vec, vec,
                 pl.BlockSpec((None, D_MODEL, tf), lambda i, j: (layer, 0, j)),
                 pl.BlockSpec((None, tf, D_MODEL), lambda i, j: (layer, j, 0)),
                 vec]
    return pl.pallas_call(
        functools.partial(_proj_mlp_kernel, n_proj=n_proj, final_norm=final_norm),
        grid=(t // tm, D_FF // tf),
        in_specs=in_specs,
        out_specs=pl.BlockSpec((tm, D_MODEL), lambda i, j: (i, 0)),
        out_shape=jax.ShapeDtypeStruct((t, D_MODEL), F32),
        scratch_shapes=[pltpu.VMEM((tm, D_MODEL), F32),
                        pltpu.VMEM((tm, D_MODEL), BF16),
                        pltpu.VMEM((tm, D_MODEL), F32)],
        compiler_params=_params("parallel", "arbitrary"),
        name="proj_mlp",
    )(x2, *ys, *wps, pb, g, w1_all, w2_all, gf)


def _row(v):
    return v.reshape(1, -1).astype(F32)


def _hybrid_front(x3, g_mix, w_in, mu, w0, w_up, a0, a_up, g_up):
    c3 = 3 * RWKV_DIM
    w = jnp.concatenate([w_in[:, c3:SHIFT_DIM], w_in[:, :c3], w_in[:, SHIFT_DIM:]], axis=1).astype(BF16)
    mu_p = jnp.concatenate([mu[c3:], mu[:c3]]).reshape(1, -1).astype(F32)
    w_lora = jnp.zeros((LORA_DIM, c3), F32)
    w_lora = w_lora.at[:LORA_W, :RWKV_DIM].set(w_up)
    w_lora = w_lora.at[LORA_W:LORA_W + LORA_A, RWKV_DIM:2 * RWKV_DIM].set(a_up)
    w_lora = w_lora.at[LORA_W + LORA_A:, 2 * RWKV_DIM:].set(g_up)
    bias = jnp.concatenate([w0, a0, jnp.zeros((RWKV_DIM,), F32)]).reshape(1, -1)
    return _in_proj(x3, g_mix, w, mu_p, w_lora.astype(BF16), bias)


def kernel(x, norm_mix_g, norm_ffn_g, final_norm_g, hy_w_in, hy_mu, hy_w0, hy_w_up, hy_a0, hy_a_up, hy_g_up, hy_k_k, hy_k_a, hy_r_k, hy_gn_g, hy_gn_b, hy_sinks, hy_w_out, cv_pw1_w, cv_pw1_b, cv_dw_w, cv_dw_b, cv_ln_g, cv_ln_b, cv_pw2_w, cv_pw2_b, mlp_w1, mlp_w2):
    bsz, seq, d = x.shape
    depth = norm_mix_g.shape[0]
    t = bsz * seq
    x2 = x.reshape(t, d)
    zero_row = jnp.zeros((1, D_MODEL), F32)
    gf = _row(final_norm_g)
    w1_all, w2_all = mlp_w1, mlp_w2

    for layer in range(depth):
        i = layer // 2
        g_mix = _row(norm_mix_g[layer])
        if layer % 2 == 0:
            r, k, v, lw, a, gate, p_att = _hybrid_front(
                x2.reshape(bsz, seq, d), g_mix, hy_w_in[i], hy_mu[i], hy_w0[i], hy_w_up[i],
                hy_a0[i], hy_a_up[i], hy_g_up[i])
            y_rwkv = _rwkv_scan(r, k, v, lw, a, gate, _row(hy_k_k[i]), _row(hy_k_a[i]),
                                _row(hy_r_k[i]), _row(hy_gn_g[i]), _row(hy_gn_b[i]))
            y_att = _swa(p_att, hy_sinks[i].astype(F32))
            w_out = hy_w_out[i].astype(BF16)
            ys = [y_rwkv.reshape(t, RWKV_DIM), y_att.reshape(t, ATT_DIM)]
            wps = [w_out[:RWKV_DIM], w_out[RWKV_DIM:]]
            pb = zero_row
        else:
            u = _norm_glu(x2, g_mix, cv_pw1_w[i].astype(BF16), _row(cv_pw1_b[i]))
            u = _conv_ln_silu(u.reshape(bsz, seq, D_MODEL), cv_dw_w[i].astype(F32),
                              _row(cv_dw_b[i]), _row(cv_ln_g[i]), _row(cv_ln_b[i]))
            ys = [u.reshape(t, D_MODEL)]
            wps = [cv_pw2_w[i].astype(BF16)]
            pb = _row(cv_pw2_b[i])
        x2 = _proj_mlp(x2, ys, wps, pb, _row(norm_ffn_g[layer]), w1_all, w2_all, layer, gf,
                       final_norm=(layer == depth - 1))
    return x2.reshape(bsz, seq, d)
```

```python
import functools

import jax
import jax.numpy as jnp
from jax import lax
from jax.experimental import pallas as pl
from jax.experimental.pallas import tpu as pltpu

D_MODEL = 1024
HEAD_DIM = 64
RWKV_DIM = 512
ATT_Q_HEADS = 8
ATT_DIM = 512
KV_DIM = 128
LORA_W = 64
LORA_A = 64
LORA_G = 128
SHIFT_DIM = 3 * RWKV_DIM + LORA_W + LORA_A + LORA_G
ATT_COLS = ATT_DIM + 2 * KV_DIM
WINDOW = 128
CONV_WIDTH = 31
CONV_HALO = 32
D_FF = 4 * D_MODEL
RMS_EPS = 1e-6
LN_EPS = 1e-5
GN_EPS = 64e-5

LANES = 128
SUBLANES = 8
CHUNK = 64
SCAN_SLOTS = 4
VMEM_LIMIT = 56 * 1024 * 1024

F32 = jnp.float32
BF16 = jnp.bfloat16
NN = (((1,), (0,)), ((), ()))
NT = (((1,), (1,)), ((), ()))
TN = (((0,), (0,)), ((), ()))


def _params(*sem):
    return pltpu.CompilerParams(dimension_semantics=sem, vmem_limit_bytes=VMEM_LIMIT)


def _rms(x, g):
    return x * lax.rsqrt(jnp.mean(x * x, axis=-1, keepdims=True) + RMS_EPS) * g


LORA_DIM = LORA_W + LORA_A + LORA_G


def _in_proj_kernel(x_ref, g_ref, w_ref, mu_ref, wl_ref, bias_ref,
                    r_ref, k_ref, v_ref, lw_ref, a_ref, gate_ref, att_ref, prev_ref):
    tm = x_ref.shape[1]
    c1, c2, c3 = RWKV_DIM, 2 * RWKV_DIM, 3 * RWKV_DIM

    @pl.when(pl.program_id(1) == 0)
    def _():
        prev_ref[...] = jnp.zeros_like(prev_ref)

    def shift_lerp(p, cs):
        first = lax.broadcasted_iota(jnp.int32, p.shape, 0) == 0
        shifted = jnp.where(first, prev_ref[SUBLANES - 1:SUBLANES, cs], pltpu.roll(p, 1, axis=0))
        prev_ref[:, cs] = p[tm - SUBLANES:, :]
        return p + (shifted - p) * mu_ref[:, cs]

    h = _rms(x_ref[0], g_ref[...]).astype(BF16)
    z = jnp.dot(h, w_ref[:, :LORA_DIM], preferred_element_type=F32)
    z = shift_lerp(z, slice(0, LORA_DIM))
    lane = lax.broadcasted_iota(jnp.int32, z.shape, 1)
    act = jnp.where(lane < LORA_W, jnp.tanh(z),
                    jnp.where(lane < LORA_W + LORA_A, z, jax.nn.sigmoid(z)))
    lo = jnp.dot(act.astype(BF16), wl_ref[...], preferred_element_type=F32) + bias_ref[...]
    rkv = jnp.dot(h, w_ref[:, LORA_DIM:SHIFT_DIM], preferred_element_type=F32)
    att_ref[0] = jnp.dot(h, w_ref[:, SHIFT_DIM:], preferred_element_type=F32)
    rkv = shift_lerp(rkv, slice(LORA_DIM, SHIFT_DIM))
    r_ref[0] = rkv[:, :c1]
    k_ref[0] = rkv[:, c1:c2]
    v_ref[0] = rkv[:, c2:c3]
    wpre = lo[:, :c1]
    softplus = jnp.maximum(-wpre, 0.0) + jnp.log(1.0 + jnp.exp(-jnp.abs(wpre)))
    lw_ref[0] = -jnp.exp(-softplus - 0.5)
    a_ref[0] = jax.nn.sigmoid(lo[:, c1:c2])
    gate_ref[0] = lo[:, c2:]


def _in_proj(x3, g, w, mu, w_lora, bias, tm=512):
    b, s, _ = x3.shape
    assert s % tm == 0 and tm % SUBLANES == 0
    rspec = pl.BlockSpec((1, tm, RWKV_DIM), lambda bi, i: (bi, i, 0))
    rshape = jax.ShapeDtypeStruct((b, s, RWKV_DIM), F32)
    const = lambda shape: pl.BlockSpec(shape, lambda bi, i: (0, 0))
    return pl.pallas_call(
        _in_proj_kernel,
        grid=(b, s // tm),
        in_specs=[
            pl.BlockSpec((1, tm, D_MODEL), lambda bi, i: (bi, i, 0)),
            const((1, D_MODEL)), const(w.shape), const((1, SHIFT_DIM)),
            const(w_lora.shape), const((1, 3 * RWKV_DIM)),
        ],
        out_specs=[rspec] * 6 + [pl.BlockSpec((1, tm, ATT_COLS), lambda bi, i: (bi, i, 0))],
        out_shape=[rshape] * 6 + [jax.ShapeDtypeStruct((b, s, ATT_COLS), F32)],
        scratch_shapes=[pltpu.VMEM((SUBLANES, SHIFT_DIM), F32)],
        compiler_params=_params("parallel", "arbitrary"),
        name="in_proj",
    )(x3, g, w, mu, w_lora, bias)


def _norm_glu_kernel(x_ref, g_ref, w_ref, b_ref, o_ref):
    h = _rms(x_ref[...], g_ref[...]).astype(BF16)
    acc = jnp.dot(h, w_ref[...], preferred_element_type=F32) + b_ref[...]
    o_ref[...] = acc[:, :D_MODEL] * jax.nn.sigmoid(acc[:, D_MODEL:])


def _norm_glu(x2, g, w, b, tm=1024):
    t = x2.shape[0]
    assert t % tm == 0
    return pl.pallas_call(
        _norm_glu_kernel,
        grid=(t // tm,),
        in_specs=[
            pl.BlockSpec((tm, D_MODEL), lambda i: (i, 0)),
            pl.BlockSpec((1, D_MODEL), lambda i: (0, 0)),
            pl.BlockSpec((D_MODEL, 2 * D_MODEL), lambda i: (0, 0)),
            pl.BlockSpec((1, 2 * D_MODEL), lambda i: (0, 0)),
        ],
        out_specs=pl.BlockSpec((tm, D_MODEL), lambda i: (i, 0)),
        out_shape=jax.ShapeDtypeStruct((t, D_MODEL), F32),
        compiler_params=_params("parallel"),
        name="norm_glu",
    )(x2, g, w, b)


def _split3(x):
    hi = x.astype(BF16)
    r1 = x - hi.astype(F32)
    mid = r1.astype(BF16)
    lo = (r1 - mid.astype(F32)).astype(BF16)
    return hi, mid, lo


def _rwkv_scan_kernel(r_ref, k_ref, v_ref, lw_ref, a_ref, g_ref,
                      kk_ref, ka_ref, rk_ref, gg_ref, gb_ref, o_ref, state_ref,
                      atrt_ref, btkt_ref, bhkh_ref, vbf_ref, decay_ref, bonus_ref):
    c_len = CHUNK
    n2 = 2 * c_len
    nb, ts = r_ref.shape[0], r_ref.shape[1]
    n_pairs = RWKV_DIM // LANES
    lane = lax.broadcasted_iota(jnp.int32, (c_len, LANES), 1)
    lo = lane < HEAD_DIM
    lo2 = lax.broadcasted_iota(jnp.int32, (n2, LANES), 1) < HEAD_DIM
    top = lax.broadcasted_iota(jnp.int32, (n2, LANES), 0) < c_len
    own = lo2 == top

    def stack(x):
        return jnp.concatenate([jnp.where(lo, x, 0.0), jnp.where(lo, 0.0, x)], axis=0)

    ri = lax.broadcasted_iota(jnp.int32, (n2, n2), 0)
    ci = lax.broadcasted_iota(jnp.int32, (n2, n2), 1)
    strict = ri > ci
    incl = ri >= ci
    eye = (ri == ci).astype(F32)
    blk = [(ri >> l) == (ci >> l) for l in range(7)]
    tri = (lax.broadcasted_iota(jnp.int32, (c_len, c_len), 0)
           >= lax.broadcasted_iota(jnp.int32, (c_len, c_len), 1)).astype(BF16)

    @pl.when(pl.program_id(1) == 0)
    def _():
        state_ref[...] = jnp.zeros_like(state_ref)

    units = [(bb, j) for bb in range(nb) for j in range(n_pairs)]
    pairs = range(len(units))

    def mm(a, b, dims=NN):
        return lax.dot_general(a.astype(BF16), b.astype(BF16), dims, preferred_element_type=F32)

    def exact_zero(vregs):
        while len(vregs) > 1:
            vregs = [a + b for a, b in zip(vregs[::2], vregs[1::2])] + vregs[len(vregs) & ~1:]
        return (jnp.clip(vregs[0], -1.0, 1.0) * 0.0)[0:1]

    def prep(c, slot):
        sl = pl.ds(pl.multiple_of(c * c_len, c_len), c_len)
        lw_all, cum_all = [], []
        for bb in range(nb):
            lw_b = lw_ref[bb, sl, :]
            cum3 = jnp.dot(tri, jnp.concatenate(_split3(lw_b), axis=1), preferred_element_type=F32)
            lw_all.append(lw_b)
            cum_all.append(cum3[:, :RWKV_DIM] + cum3[:, RWKV_DIM:2 * RWKV_DIM] + cum3[:, 2 * RWKV_DIM:])

        folded = []
        for idx, (bb, j) in enumerate(units):
            cs = slice(j * LANES, (j + 1) * LANES)
            cum, lw = cum_all[bb][:, cs], lw_all[bb][:, cs]
            cum_last = cum[c_len - 1:c_len, :]
            r_s, k_s, v_s = stack(r_ref[bb, sl, cs]), stack(k_ref[bb, sl, cs]), stack(v_ref[bb, sl, cs])
            a = a_ref[bb, sl, cs]

            def both(x):
                return jnp.concatenate([x, x], axis=0)

            kk = k_s * kk_ref[:, cs]
            kk = kk * lax.rsqrt(jnp.maximum(jnp.sum(kk * kk, axis=-1, keepdims=True), 1e-24))
            kmod = k_s * both(1.0 + (a - 1.0) * ka_ref[:, cs])
            bvec = kk * both(a)
            einv = both(jnp.exp(-cum))
            edec = both(jnp.exp(cum_last - cum))
            at_rt = jnp.concatenate([kk * both(-jnp.exp(cum - lw)), r_s * both(jnp.exp(cum))], axis=0)
            bh_kh = jnp.concatenate([bvec * edec, kmod * edec], axis=0)
            bonus = jnp.sum(r_s * kmod * rk_ref[:, cs], axis=-1, keepdims=True) * v_s
            bonus = bonus[:c_len] + bonus[c_len:]
            atrt_ref[slot, idx] = at_rt.astype(BF16)
            btkt_ref[slot, idx] = jnp.concatenate([bvec * einv, kmod * einv], axis=0).astype(BF16)
            bhkh_ref[slot, idx] = bh_kh.astype(BF16)
            vbf_ref[slot, idx] = v_s.astype(BF16)
            decay_ref[slot, idx] = jnp.exp(cum_last)
            bonus_ref[slot, idx] = bonus
            for x in (at_rt, bh_kh, bonus):
                folded.extend(x[r:r + SUBLANES] for r in range(0, x.shape[0], SUBLANES))
        return exact_zero(folded)

    def inverse_part(slots, between=None, zero=None):
        def tick():
            if between is not None:
                next(between, None)

        chains = [(slot, j) for slot in slots for j in pairs]
        lhs = [atrt_ref[slot, j] for slot, j in chains]
        if zero is not None:
            lhs[0] = lhs[0] + zero.astype(BF16)
        gram = [mm(lhs[i], btkt_ref[slot, j], NT) for i, (slot, j) in enumerate(chains)]
        tick()
        a_ab = [jnp.where(strict, g[:n2, :n2], 0.0).astype(BF16) for g in gram]
        a_ak = [jnp.where(strict, g[:n2, n2:], 0.0).astype(BF16) for g in gram]
        r_b = [jnp.where(incl, g[n2:, :n2], 0.0).astype(BF16) for g in gram]
        r_k = [jnp.where(incl, g[n2:, n2:], 0.0).astype(BF16) for g in gram]
        tinv = [(eye + jnp.where(strict & blk[1], g[:n2, :n2], 0.0)).astype(BF16) for g in gram]
        for lvl in range(2, 7):
            off = blk[lvl] & ~blk[lvl - 1]
            w = [jnp.where(off, mm(a, t), 0.0).astype(BF16) for a, t in zip(a_ab, tinv)]
            tinv = [t + mm(t, wi).astype(BF16) for t, wi in zip(tinv, w)]
            tick()
        akv = [mm(a_ak[i], vbf_ref[slot, j]) for i, (slot, j) in enumerate(chains)]
        if between is not None:
            for _ in between:
                pass
        n = len(pairs)
        return {slot: (tinv[k * n:(k + 1) * n], akv[k * n:(k + 1) * n], r_b[k * n:(k + 1) * n],
                       r_k[k * n:(k + 1) * n]) for k, slot in enumerate(slots)}

    def state_stages(c, slot, tinv, akv, r_b, r_k, zero=None):
        sl = pl.ds(pl.multiple_of(c * c_len, c_len), c_len)
        at_rt = [atrt_ref[slot, j] for j in pairs]
        v_bf = [vbf_ref[slot, j] for j in pairs]
        state = [state_ref[j] for j in pairs]
        if zero is not None:
            state[0] = state[0] + zero
        s_t = [mm(at_rt[j], state[j], NT) for j in pairs]
        yield
        rhs = [s_t[j][:n2] + akv[j] for j in pairs]
        rhs_hi = [x.astype(BF16) for x in rhs]
        rhs_lo = [(x - h.astype(F32)).astype(BF16) for x, h in zip(rhs, rhs_hi)]
        u = [mm(tinv[j], rhs_hi[j]) + mm(tinv[j], rhs_lo[j]) for j in pairs]
        u_bf = [u[j].astype(BF16) for j in pairs]
        yield
        for j in pairs:
            state_ref[j] = state[j] * decay_ref[slot, j] + mm(
                jnp.concatenate([u_bf[j], v_bf[j]], axis=0), bhkh_ref[slot, j], TN)
        y = [s_t[j][n2:] + mm(r_b[j], u_bf[j]) + mm(r_k[j], v_bf[j]) for j in pairs]
        yield
        inv_n = 1.0 / HEAD_DIM
        for idx, (bb, j) in enumerate(units):
            cs = slice(j * LANES, (j + 1) * LANES)
            mean = jnp.sum(y[idx], axis=-1, keepdims=True) * inv_n
            yc = jnp.where(own, y[idx] - mean, 0.0)
            var = jnp.sum(yc * yc, axis=-1, keepdims=True) * inv_n
            yn = yc * lax.rsqrt(var + GN_EPS)
            yn = yn[:c_len] + yn[c_len:]
            out = yn * gg_ref[:, cs] + gb_ref[:, cs] + bonus_ref[slot, idx]
            o_ref[bb, sl, cs] = (out * g_ref[bb, sl, cs]).astype(o_ref.dtype)

    def state_of(c, slots, inv, zero=None):
        for k, slot in enumerate(slots):
            yield from state_stages(c + k, slot, *inv[slot], zero=zero if k == 0 else None)

    n_chunks = ts // c_len
    prep(0, 0)
    prep(1, 1)

    def body(i, carry):
        c0 = 4 * i
        zs = [prep(c0 + 2 + k, 2 + k) for k in range(2)]
        inv_x = inverse_part((0, 1))
        inv_y = inverse_part((2, 3), between=state_of(c0, (0, 1), inv_x))
        zs = [prep(jnp.minimum(c0 + 4 + k, n_chunks - 1), k) for k in range(2)]
        for _ in state_of(c0 + 2, (2, 3), inv_y, zero=zs[0] + zs[1]):
            pass
        return carry

    lax.fori_loop(0, n_chunks // 4, body, 0)


def _rwkv_scan(r, k, v, lw, a, g, kkp, kap, rkp, ggp, gbp, ts=512, nb=2):
    b, s, _ = r.shape
    assert b % nb == 0 and s % ts == 0 and (ts // CHUNK) % SCAN_SLOTS == 0
    n_units = nb * RWKV_DIM // LANES
    xspec = pl.BlockSpec((nb, ts, RWKV_DIM), lambda bi, i: (bi, i, 0))
    pspec = pl.BlockSpec((1, RWKV_DIM), lambda bi, i: (0, 0))
    return pl.pallas_call(
        _rwkv_scan_kernel,
        grid=(b // nb, s // ts),
        in_specs=[xspec] * 6 + [pspec] * 5,
        out_specs=xspec,
        out_shape=jax.ShapeDtypeStruct((b, s, RWKV_DIM), BF16),
        scratch_shapes=[
            pltpu.VMEM((n_units, LANES, LANES), F32),
            pltpu.VMEM((SCAN_SLOTS, n_units, 4 * CHUNK, LANES), BF16),
            pltpu.VMEM((SCAN_SLOTS, n_units, 4 * CHUNK, LANES), BF16),
            pltpu.VMEM((SCAN_SLOTS, n_units, 4 * CHUNK, LANES), BF16),
            pltpu.VMEM((SCAN_SLOTS, n_units, 2 * CHUNK, LANES), BF16),
            pltpu.VMEM((SCAN_SLOTS, n_units, 1, LANES), F32),
            pltpu.VMEM((SCAN_SLOTS, n_units, CHUNK, LANES), F32),
        ],
        compiler_params=_params("parallel", "arbitrary"),
        name="rwkv_scan",
    )(r, k, v, lw, a, g, kkp, kap, rkp, ggp, gbp)


def _swa_kernel(sink_ref, q_ref, kvc_ref, kvp_ref, o_ref):
    n = pl.program_id(1)
    blk = WINDOW
    n_sub = q_ref.shape[1] // blk
    kv_all = jnp.concatenate([kvp_ref[0], kvc_ref[0]], axis=0)
    kmat, vmat = kv_all[:, :KV_DIM], kv_all[:, KV_DIM:]
    lo_kv = lax.broadcasted_iota(jnp.int32, kmat.shape, 1) < HEAD_DIM
    k_sw = pltpu.roll(kmat, HEAD_DIM, axis=1)
    v_sw = pltpu.roll(vmat, HEAD_DIM, axis=1)
    k_dup = [jnp.where(lo_kv, kmat, k_sw).astype(BF16), jnp.where(lo_kv, k_sw, kmat).astype(BF16)]
    v_dup = [jnp.where(lo_kv, vmat, v_sw).astype(BF16), jnp.where(lo_kv, v_sw, vmat).astype(BF16)]

    lo_q = lax.broadcasted_iota(jnp.int32, (blk, LANES), 1) < HEAD_DIM
    rows = lax.broadcasted_iota(jnp.int32, (2 * blk, 2 * blk), 0)
    cols = lax.broadcasted_iota(jnp.int32, (2 * blk, 2 * blk), 1)
    rel = (rows & (blk - 1)) + blk - cols
    band = (rel >= 0) & (rel < WINDOW)
    has_prev = (cols >= blk) | (n > 0)
    relf = rel.astype(F32)
    first = lax.broadcasted_iota(jnp.int32, (2 * blk, 1), 0) < blk

    pairs = range(ATT_Q_HEADS // 2)
    subs = range(n_sub)
    scores = {}
    for sb in subs:
        for j in pairs:
            qp = q_ref[0, sb * blk:(sb + 1) * blk, j * LANES:(j + 1) * LANES] * (HEAD_DIM ** -0.5)
            qs = jnp.concatenate([jnp.where(lo_q, qp, 0.0), jnp.where(lo_q, 0.0, qp)], axis=0)
            keys = k_dup[j // 2][sb * blk:(sb + 2) * blk]
            scores[sb, j] = lax.dot_general(qs.astype(BF16), keys, NT, preferred_element_type=F32)
    probs, inv = {}, {}
    for j in pairs:
        slope = jnp.where(first, 2.0 ** -(2 * j + 1), 2.0 ** -(2 * j + 2))
        sink = jnp.where(first, sink_ref[2 * j], sink_ref[2 * j + 1])
        bias = jnp.where(band, -slope * relf, -jnp.inf)
        for sb in subs:
            s = scores[sb, j] + bias
            if sb == 0:
                s = jnp.where(has_prev, s, -jnp.inf)
            m = jnp.maximum(jnp.max(s, axis=-1, keepdims=True), sink)
            e = jnp.exp(s - m)
            inv[sb, j] = 1.0 / (jnp.sum(e, axis=-1, keepdims=True) + jnp.exp(sink - m))
            probs[sb, j] = e.astype(BF16)
    for sb in subs:
        for j in pairs:
            o = jnp.dot(probs[sb, j], v_dup[j // 2][sb * blk:(sb + 2) * blk], preferred_element_type=F32)
            o = o * inv[sb, j]
            o_ref[0, sb * blk:(sb + 1) * blk, j * LANES:(j + 1) * LANES] = (
                jnp.where(lo_q, o[:blk], o[blk:]).astype(o_ref.dtype))


def _swa(p_att3, sinks, n_sub=4):
    b, s, _ = p_att3.shape
    blk = WINDOW
    tq = n_sub * blk
    assert s % tq == 0
    kv_col = ATT_DIM // (2 * KV_DIM)
    return pl.pallas_call(
        _swa_kernel,
        grid=(b, s // tq),
        in_specs=[
            pl.BlockSpec(memory_space=pltpu.SMEM),
            pl.BlockSpec((1, tq, ATT_DIM), lambda bi, n: (bi, n, 0)),
            pl.BlockSpec((1, tq, 2 * KV_DIM), lambda bi, n: (bi, n, kv_col)),
            pl.BlockSpec((1, blk, 2 * KV_DIM),
                         lambda bi, n: (bi, jnp.maximum(n * n_sub - 1, 0), kv_col)),
        ],
        out_specs=pl.BlockSpec((1, tq, ATT_DIM), lambda bi, n: (bi, n, 0)),
        out_shape=jax.ShapeDtypeStruct((b, s, ATT_DIM), BF16),
        compiler_params=_params("parallel", "parallel"),
        name="swa_attention",
    )(sinks, p_att3, p_att3, p_att3)


def _conv_kernel(u_ref, halo_ref, w_ref, b_ref, lg_ref, lb_ref, o_ref, sh_ref, acc_ref):
    i = pl.program_id(1)
    ts = u_ref.shape[1]
    rb = 128
    half = SUBLANES // 2
    n_blocks = D_MODEL // LANES
    n_sh = ts + CONV_HALO
    for cb in range(n_blocks):
        cs = slice(cb * LANES, (cb + 1) * LANES)
        sh_ref[0, cb, 0:CONV_HALO, :] = jnp.where(i > 0, halo_ref[0, :, cs], 0.0)
        sh_ref[0, cb, CONV_HALO:n_sh, :] = u_ref[0, :, cs]
        sh_ref[0, cb, n_sh:, :] = jnp.zeros((SUBLANES, LANES), F32)
    for q in range(1, half):
        for cb in range(n_blocks):
            sh_ref[q, cb, 0:n_sh, :] = sh_ref[0, cb, q:q + n_sh, :]
    off = CONV_HALO - (CONV_WIDTH - 1)
    for cb in range(n_blocks):
        cs = slice(cb * LANES, (cb + 1) * LANES)

        def rows(rblk, carry, cb=cb, cs=cs, shifted=False):
            r0 = pl.multiple_of(rblk * rb, rb)
            n = rb + (SUBLANES if shifted else 0)
            total = None
            for q in range(half):
                taps = [j for j in range(CONV_WIDTH)
                        if (off + j) % SUBLANES == q + (half if shifted else 0)]
                a_max = (off + taps[-1]) // SUBLANES
                x = sh_ref[q, cb, pl.ds(r0, n + a_max * SUBLANES), :]
                part = None
                for j in taps:
                    a = (off + j) // SUBLANES
                    term = w_ref[j:j + 1, cs] * x[a * SUBLANES:a * SUBLANES + n]
                    part = term if part is None else part + term
                total = part if total is None else total + part
            if shifted:
                acc = acc_ref[cb, pl.ds(r0, rb), :] + pltpu.roll(total, n - half, axis=0)[:rb]
            else:
                acc = total + b_ref[:, cs]
            acc_ref[cb, pl.ds(r0, rb), :] = acc
            return carry

        lax.fori_loop(0, ts // rb, functools.partial(rows, shifted=False), 0)
        lax.fori_loop(0, ts // rb, functools.partial(rows, shifted=True), 0)
    y = jnp.concatenate([acc_ref[cb] for cb in range(n_blocks)], axis=1)
    mean = jnp.mean(y, axis=-1, keepdims=True)
    yc = y - mean
    var = jnp.mean(yc * yc, axis=-1, keepdims=True)
    yn = yc * lax.rsqrt(var + LN_EPS) * lg_ref[...] + lb_ref[...]
    o_ref[0] = (yn * jax.nn.sigmoid(yn)).astype(o_ref.dtype)


def _conv_ln_silu(u3, dw_w, dw_b, ln_g, ln_b, ts=512):
    b, s, _ = u3.shape
    assert s % ts == 0 and ts % CONV_HALO == 0 and ts % 128 == 0
    vec = pl.BlockSpec((1, D_MODEL), lambda bi, i: (0, 0))
    return pl.pallas_call(
        _conv_kernel,
        grid=(b, s // ts),
        in_specs=[
            pl.BlockSpec((1, ts, D_MODEL), lambda bi, i: (bi, i, 0)),
            pl.BlockSpec((1, CONV_HALO, D_MODEL),
                         lambda bi, i: (bi, jnp.maximum(i * (ts // CONV_HALO) - 1, 0), 0)),
            pl.BlockSpec((CONV_WIDTH, D_MODEL), lambda bi, i: (0, 0)),
            vec, vec, vec,
        ],
        out_specs=pl.BlockSpec((1, ts, D_MODEL), lambda bi, i: (bi, i, 0)),
        out_shape=jax.ShapeDtypeStruct((b, s, D_MODEL), BF16),
        scratch_shapes=[pltpu.VMEM((SUBLANES // 2, D_MODEL // LANES, ts + CONV_HALO + SUBLANES, LANES), F32),
                        pltpu.VMEM((D_MODEL // LANES, ts, LANES), F32)],
        compiler_params=_params("parallel", "parallel"),
        name="conv_ln_silu",
    )(u3, u3, dw_w, dw_b, ln_g, ln_b)


def _proj_mlp_kernel(*refs, n_proj, final_norm):
    x_ref = refs[0]
    y_refs = refs[1:1 + n_proj]
    wp_refs = refs[1 + n_proj:1 + 2 * n_proj]
    pb_ref, g_ref, w1_ref, w2_ref, gf_ref, o_ref, x1_ref, h_ref, acc_ref = refs[1 + 2 * n_proj:]
    j = pl.program_id(1)
    last = pl.num_programs(1) - 1
    tm = x_ref.shape[0]
    all_rows = slice(0, tm)
    halves = (slice(0, tm // 2), slice(tm // 2, tm))

    def residual_in(rows):
        x1 = x_ref[rows, :] + pb_ref[...]
        for y_ref, wp_ref in zip(y_refs, wp_refs):
            x1 = x1 + jnp.dot(y_ref[rows, :], wp_ref[...], preferred_element_type=F32)
        x1_ref[rows, :] = x1
        h_ref[rows, :] = _rms(x1, g_ref[...]).astype(BF16)

    def hidden(rows, w1):
        hid = jnp.dot(h_ref[rows, :], w1, preferred_element_type=F32)
        return jnp.square(jnp.maximum(hid, 0.0)).astype(BF16)

    def residual_out(rows, mlp):
        out = x1_ref[rows, :] + mlp
        if final_norm:
            out = _rms(out, gf_ref[...])
        o_ref[rows, :] = out

    @pl.when(j == 0)
    def _():
        w1, w2 = w1_ref[...].astype(BF16), w2_ref[...].astype(BF16)
        for rows in halves:
            residual_in(rows)
        for rows in halves:
            acc_ref[rows, :] = jnp.dot(hidden(rows, w1), w2, preferred_element_type=F32)

    @pl.when((j > 0) & (j < last))
    def _():
        hid = hidden(all_rows, w1_ref[...].astype(BF16))
        acc_ref[...] += jnp.dot(hid, w2_ref[...].astype(BF16), preferred_element_type=F32)

    @pl.when(j == last)
    def _():
        w2 = w2_ref[...].astype(BF16)
        hid = hidden(all_rows, w1_ref[...].astype(BF16))
        for rows in halves:
            residual_out(rows, acc_ref[rows, :] + jnp.dot(hid[rows], w2, preferred_element_type=F32))


def _proj_mlp(x2, ys, wps, pb, g, w1_all, w2_all, layer, gf, final_norm, tm=1024, tf=1024):
    t = x2.shape[0]
    n_proj = len(ys)
    assert t % tm == 0 and D_FF % tf == 0 and D_FF // tf >= 2
    vec = pl.BlockSpec((1, D_MODEL), lambda i, j: (0, 0))
    in_specs = [pl.BlockSpec((tm, D_MODEL), lambda i, j: (i, 0))]
    in_specs += [pl.BlockSpec((tm, y.shape[1]), lambda i, j: (i, 0)) for y in ys]
    in_specs += [pl.BlockSpec(w.shape, lambda i, j: (0, 0)) for w in wps]
    in_specs += [vec, vec,
                 pl.BlockSpec((None, D_MODEL, tf), lambda i, j: (layer, 0, j)),
                 pl.BlockSpec((None, tf, D_MODEL), lambda i, j: (layer, j, 0)),
                 vec]
    return pl.pallas_call(
        functools.partial(_proj_mlp_kernel, n_proj=n_proj, final_norm=final_norm),
        grid=(t // tm, D_FF // tf),
        in_specs=in_specs,
        out_specs=pl.BlockSpec((tm, D_MODEL), lambda i, j: (i, 0)),
        out_shape=jax.ShapeDtypeStruct((t, D_MODEL), F32),
        scratch_shapes=[pltpu.VMEM((tm, D_MODEL), F32),
                        pltpu.VMEM((tm, D_MODEL), BF16),
                        pltpu.VMEM((tm, D_MODEL), F32)],
        compiler_params=_params("parallel", "arbitrary"),
        name="proj_mlp",
    )(x2, *ys, *wps, pb, g, w1_all, w2_all, gf)


def _row(v):
    return v.reshape(1, -1).astype(F32)


def _hybrid_front(x3, g_mix, w_in, mu, w0, w_up, a0, a_up, g_up):
    c3 = 3 * RWKV_DIM
    w = jnp.concatenate([w_in[:, c3:SHIFT_DIM], w_in[:, :c3], w_in[:, SHIFT_DIM:]], axis=1).astype(BF16)
    mu_p = jnp.concatenate([mu[c3:], mu[:c3]]).reshape(1, -1).astype(F32)
    w_lora = jnp.zeros((LORA_DIM, c3), F32)
    w_lora = w_lora.at[:LORA_W, :RWKV_DIM].set(w_up)
    w_lora = w_lora.at[LORA_W:LORA_W + LORA_A, RWKV_DIM:2 * RWKV_DIM].set(a_up)
    w_lora = w_lora.at[LORA_W + LORA_A:, 2 * RWKV_DIM:].set(g_up)
    bias = jnp.concatenate([w0, a0, jnp.zeros((RWKV_DIM,), F32)]).reshape(1, -1)
    return _in_proj(x3, g_mix, w, mu_p, w_lora.astype(BF16), bias)


def kernel(x, norm_mix_g, norm_ffn_g, final_norm_g, hy_w_in, hy_mu, hy_w0, hy_w_up, hy_a0, hy_a_up, hy_g_up, hy_k_k, hy_k_a, hy_r_k, hy_gn_g, hy_gn_b, hy_sinks, hy_w_out, cv_pw1_w, cv_pw1_b, cv_dw_w, cv_dw_b, cv_ln_g, cv_ln_b, cv_pw2_w, cv_pw2_b, mlp_w1, mlp_w2):
    bsz, seq, d = x.shape
    depth = norm_mix_g.shape[0]
    t = bsz * seq
    x2 = x.reshape(t, d)
    zero_row = jnp.zeros((1, D_MODEL), F32)
    gf = _row(final_norm_g)
    w1_all, w2_all = mlp_w1, mlp_w2

    for layer in range(depth):
        i = layer // 2
        g_mix = _row(norm_mix_g[layer])
        if layer % 2 == 0:
            r, k, v, lw, a, gate, p_att = _hybrid_front(
                x2.reshape(bsz, seq, d), g_mix, hy_w_in[i], hy_mu[i], hy_w0[i], hy_w_up[i],
                hy_a0[i], hy_a_up[i], hy_g_up[i])
            y_rwkv = _rwkv_scan(r, k, v, lw, a, gate, _row(hy_k_k[i]), _row(hy_k_a[i]),
                                _row(hy_r_k[i]), _row(hy_gn_g[i]), _row(hy_gn_b[i]))
            y_att = _swa(p_att, hy_sinks[i].astype(F32))
            w_out = hy_w_out[i].astype(BF16)
            ys = [y_rwkv.reshape(t, RWKV_DIM), y_att.reshape(t, ATT_DIM)]
            wps = [w_out[:RWKV_DIM], w_out[RWKV_DIM:]]
            pb = zero_row
        else:
            u = _norm_glu(x2, g_mix, cv_pw1_w[i].astype(BF16), _row(cv_pw1_b[i]))
            u = _conv_ln_silu(u.reshape(bsz, seq, D_MODEL), cv_dw_w[i].astype(F32),
                              _row(cv_dw_b[i]), _row(cv_ln_g[i]), _row(cv_ln_b[i]))
            ys = [u.reshape(t, D_MODEL)]
            wps = [cv_pw2_w[i].astype(BF16)]
            pb = _row(cv_pw2_b[i])
        x2 = _proj_mlp(x2, ys, wps, pb, _row(norm_ffn_g[layer]), w1_all, w2_all, layer, gf,
                       final_norm=(layer == depth - 1))
    return x2.reshape(bsz, seq, d)
```
